```python
import math
import jax
import jax.numpy as jnp
from jax import lax
import numpy as np

D_MODEL = 1024
BATCH = 4
SEQ = 4096
DEPTH = 2
DEC_BATCH = 32
DEC_SEQ = 64
PAST_LEN = 4096

CHUNK = 64
Q_BLOCK = 128
SB_HEADS = 4
SB_HD = 64
SB_W = SB_HEADS * SB_HD
GLA_HEADS = 4
GLA_DK = 32
GLA_DV = 64
GLA_QK_W = GLA_HEADS * GLA_DK
GLA_W = GLA_HEADS * GLA_DV
GLA_GATE_RANK = 16
GLA_TAU = 16.0
GLA_BLOCK = 16
MLA_HEADS = 8
MLA_Q_LORA = 256
MLA_KV_LORA = 128
MLA_NOPE = 64
MLA_ROPE = 32
MLA_V = 64
MLA_W = MLA_HEADS * MLA_V
ROPE_THETA = 10000.0
MIX_W = SB_W + GLA_W + MLA_W
N_IN = 3 * SB_W + 2 * GLA_QK_W + 2 * GLA_W + GLA_GATE_RANK + MLA_Q_LORA + MLA_KV_LORA + MLA_ROPE
MEM_LEN = 256
MEM_HEADS = 4
MEM_HD = 128
MEM_W = MEM_HEADS * MEM_HD
D_FF = -(-8 * D_MODEL // (3 * 256)) * 256
NORM_EPS = 1e-6

kernel_name = 'hybrid_sb_gla_mla_stream_step'


def rmsnorm(x, g):
    xf = x.astype(jnp.float32)
    y = xf * lax.rsqrt(jnp.mean(xf * xf, axis=-1, keepdims=True) + NORM_EPS)
    return (y * g.astype(jnp.float32)).astype(x.dtype)


def rope(x, pos):
    half = x.shape[-1] // 2
    freqs = ROPE_THETA ** (-jnp.arange(half, dtype=jnp.float32) / half)
    ang = pos.astype(jnp.float32)[:, None] * freqs[None, :]
    shape = (pos.shape[0],) + (1,) * (x.ndim - 3) + (half,)
    cos = jnp.cos(ang).reshape(shape)
    sin = jnp.sin(ang).reshape(shape)
    xf = x.astype(jnp.float32)
    x1, x2 = xf[..., :half], xf[..., half:]
    return jnp.concatenate([x1 * cos - x2 * sin, x1 * sin + x2 * cos], axis=-1).astype(x.dtype)


def to_blocks(a, qb):
    B, T = a.shape[0], a.shape[1]
    return jnp.swapaxes(a.reshape((B, T // qb, qb) + a.shape[2:]), 0, 1)


def from_blocks(a):
    nb, B, qb = a.shape[:3]
    return jnp.swapaxes(a, 0, 1).reshape((B, nb * qb) + a.shape[3:])


def stick_breaking_attention(q, k, v, q_pos, k_pos):
    T = q.shape[1]
    qb = Q_BLOCK if T % Q_BLOCK == 0 else T
    scale = 1.0 / math.sqrt(q.shape[-1])

    def block(args):
        qblk, pblk = args
        z = jnp.einsum('bqhd,bkhd->bhqk', qblk, k).astype(jnp.float32) * scale
        visible = k_pos[None, :] < pblk[:, None]
        log_keep = jnp.where(visible, jax.nn.log_sigmoid(-z), 0.0)
        later = lax.cumsum(log_keep, axis=3, reverse=True) - log_keep
        wts = jnp.where(visible, jnp.exp(jax.nn.log_sigmoid(z) + later), 0.0)
        return jnp.einsum('bhqk,bkhd->bqhd', wts.astype(v.dtype), v)

    out = lax.map(block, (to_blocks(q, qb), q_pos.reshape(-1, qb)))
    return from_blocks(out)


def gla_recurrence(q, k, v, log_a, s0):
    T, dk = q.shape[1], q.shape[-1]
    pad = (-T) % GLA_BLOCK
    scale = dk ** -0.5

    def prep(a):
        a = jnp.pad(a.astype(jnp.float32), ((0, 0), (0, pad), (0, 0), (0, 0)))
        return to_blocks(a, GLA_BLOCK)

    tril = jnp.tril(jnp.ones((GLA_BLOCK, GLA_BLOCK), dtype=bool))

    def step(S, blk):
        qc, kc, vc, ac = blk
        b = jnp.cumsum(ac, axis=1)
        qt = qc * jnp.exp(b) * scale
        kt = kc * jnp.exp(-b)
        att = jnp.where(tril, jnp.einsum('bchk,bshk->bhcs', qt, kt), 0.0)
        o = jnp.einsum('bchk,bhkv->bchv', qt, S) + jnp.einsum('bhcs,bshv->bchv', att, vc)
        b_last = b[:, -1]
        kd = kc * jnp.exp(b_last[:, None] - b)
        S = jnp.exp(b_last)[..., None] * S + jnp.einsum('bchk,bchv->bhkv', kd, vc)
        return S, o

    S, o = lax.scan(step, s0.astype(jnp.float32), (prep(q), prep(k), prep(v), prep(log_a)))
    o = from_blocks(o)[:, :T]
    return o.astype(v.dtype), S.astype(s0.dtype)


def chunk_causal_attention(q_nope, q_rope, k_nope, k_rope, v, q_pos, k_pos):
    T = q_nope.shape[1]
    qb = Q_BLOCK if T % Q_BLOCK == 0 else T
    scale = (q_nope.shape[-1] + q_rope.shape[-1]) ** -0.5
    k_chunk = k_pos // CHUNK

    def block(args):
        qn, qr, pb = args
        s = (jnp.einsum('bqhd,bkhd->bhqk', qn, k_nope)
             + jnp.einsum('bqhr,bkr->bhqk', qr, k_rope)).astype(jnp.float32) * scale
        visible = k_chunk[None, :] <= (pb // CHUNK)[:, None]
        s = jnp.where(visible, s, -1e30)
        p = jax.nn.softmax(s, axis=-1).astype(v.dtype)
        return jnp.einsum('bhqk,bkhd->bqhd', p, v)

    out = lax.map(block, (to_blocks(q_nope, qb), to_blocks(q_rope, qb), q_pos.reshape(-1, qb)))
    return from_blocks(out)


def mla_attention(cq, ckv, kr_raw, lat_past, kr_past, q_pos, k_pos, w):
    B, T = cq.shape[:2]
    q = (rmsnorm(cq, w['g_cq']) @ w['w_uq']).reshape(B, T, MLA_HEADS, MLA_NOPE + MLA_ROPE)
    q_nope = rmsnorm(q[..., :MLA_NOPE], w['g_qn'])
    q_rope = rope(rmsnorm(q[..., MLA_NOPE:], w['g_qr']), q_pos)
    lat = rmsnorm(ckv, w['g_ckv'])
    kr = rope(rmsnorm(kr_raw, w['g_kr']), q_pos)
    lat_all = jnp.concatenate([lat_past, lat], axis=1)
    kr_all = jnp.concatenate([kr_past, kr], axis=1)
    L = lat_all.shape[1]
    kv = (lat_all @ w['w_ukv']).reshape(B, L, MLA_HEADS, MLA_NOPE + MLA_V)
    k_nope = rmsnorm(kv[..., :MLA_NOPE], w['g_kn'])
    v = kv[..., MLA_NOPE:]
    out = chunk_causal_attention(q_nope, q_rope, k_nope, kr_all, v, q_pos, k_pos)
    return out, lat, kr


def memory_kv(mem, w):
    B, M, _ = mem.shape
    h = rmsnorm(mem, w['g_mem_norm'])
    k = rmsnorm((h @ w['w_ck']).reshape(B, M, MEM_HEADS, MEM_HD), w['g_ckn'])
    v = (h @ w['w_cv']).reshape(B, M, MEM_HEADS, MEM_HD)
    return k, v


def memory_cross_attention(h, mem_k, mem_v, w):
    B, T, _ = h.shape
    q = rmsnorm((h @ w['w_cq']).reshape(B, T, MEM_HEADS, MEM_HD), w['g_cqn'])
    s = jnp.einsum('bqhd,bkhd->bhqk', q, mem_k).astype(jnp.float32) * (MEM_HD ** -0.5)
    p = jax.nn.softmax(s, axis=-1).astype(mem_v.dtype)
    o = jnp.einsum('bhqk,bkhd->bqhd', p, mem_v).reshape(B, T, MEM_W)
    return o @ w['w_co']


def trunk_layer(x, sb_k_past, sb_v_past, gla_s0, lat_past, kr_past, mem_k, mem_v, w):
    B, T, _ = x.shape
    P = sb_k_past.shape[1]
    q_pos = P + jnp.arange(T, dtype=jnp.int32)
    k_pos = jnp.arange(P + T, dtype=jnp.int32)
    h = rmsnorm(x, w['g_mix_norm'])
    proj = h @ w['w_in']
    sizes = (SB_W, SB_W, SB_W, GLA_QK_W, GLA_QK_W, GLA_W, GLA_GATE_RANK, GLA_W,
             MLA_Q_LORA, MLA_KV_LORA, MLA_ROPE)
    offsets = []
    acc = 0
    for s in sizes[:-1]:
        acc += s
        offsets.append(acc)
    qa, ka, va, qg, kg, vg, ag, rg, cq, ckv, kr_raw = jnp.split(proj, offsets, axis=-1)
    qa = qa.reshape(B, T, SB_HEADS, SB_HD)
    ka = ka.reshape(B, T, SB_HEADS, SB_HD)
    va = va.reshape(B, T, SB_HEADS, SB_HD)
    o_a = stick_breaking_attention(qa, jnp.concatenate([sb_k_past, ka], axis=1),
                                   jnp.concatenate([sb_v_past, va], axis=1), q_pos, k_pos)
    o_a = rmsnorm(o_a, w['g_sb_out']).reshape(B, T, SB_W)
    log_a = jax.nn.log_sigmoid((ag @ w['w_gla_gate'] + w['b_gla_gate']).astype(jnp.float32)) / GLA_TAU
    o_b, gla_s = gla_recurrence(qg.reshape(B, T, GLA_HEADS, GLA_DK), kg.reshape(B, T, GLA_HEADS, GLA_DK),
                                vg.reshape(B, T, GLA_HEADS, GLA_DV),
                                log_a.reshape(B, T, GLA_HEADS, GLA_DK), gla_s0)
    o_b = rmsnorm(o_b, w['g_gla_out']).reshape(B, T, GLA_W) * jax.nn.silu(rg)
    o_c, lat, kr = mla_attention(cq, ckv, kr_raw, lat_past, kr_past, q_pos, k_pos, w)
    o_c = rmsnorm(o_c, w['g_mla_out']).reshape(B, T, MLA_W)
    x = x + jnp.concatenate([o_a, o_b, o_c], axis=-1) @ w['w_out']
    x = x + memory_cross_attention(rmsnorm(x, w['g_cross_norm']), mem_k, mem_v, w)
    h = rmsnorm(x, w['g_ffn_norm'])
    x = x + (jax.nn.silu(h @ w['w_gate']) * (h @ w['w_up'])) @ w['w_down']
    return x, ka, va, gla_s, lat, kr


def setup_inputs(seed: int = 0) -> dict:
    key = jax.random.key(seed)
    ks = iter(jax.random.split(key, 64))
    f32 = jnp.float32

    def nrm(shape, scale=1.0):
        return jax.random.normal(next(ks), shape, f32) * scale

    def gain(n):
        return 1.0 + 0.02 * jax.random.normal(next(ks), (DEPTH, n), f32)

    return {
        'x_prompt': nrm((BATCH, SEQ, D_MODEL)),
        'x_sample': nrm((DEC_BATCH, DEC_SEQ, D_MODEL)),
        'mem_prompt': nrm((BATCH, MEM_LEN, D_MODEL)),
        'cache_sb_k': nrm((DEPTH, DEC_BATCH, PAST_LEN, SB_HEADS, SB_HD)),
        'cache_sb_v': nrm((DEPTH, DEC_BATCH, PAST_LEN, SB_HEADS, SB_HD)),
        'state_gla': nrm((DEPTH, DEC_BATCH, GLA_HEADS, GLA_DK, GLA_DV), 0.5),
        'cache_mla_latent': nrm((DEPTH, DEC_BATCH, PAST_LEN, MLA_KV_LORA)),
        'cache_mla_krope': nrm((DEPTH, DEC_BATCH, PAST_LEN, MLA_ROPE)),
        'cache_mem_k': nrm((DEPTH, DEC_BATCH, MEM_LEN, MEM_HEADS, MEM_HD)),
        'cache_mem_v': nrm((DEPTH, DEC_BATCH, MEM_LEN, MEM_HEADS, MEM_HD)),
        'g_mix_norm': gain(D_MODEL),
        'w_in': nrm((DEPTH, D_MODEL, N_IN), D_MODEL ** -0.5),
        'w_gla_gate': nrm((DEPTH, GLA_GATE_RANK, GLA_QK_W), GLA_GATE_RANK ** -0.5),
        'b_gla_gate': nrm((DEPTH, GLA_QK_W), 0.1),
        'g_gla_out': gain(GLA_DV),
        'g_sb_out': gain(SB_HD),
        'g_cq': gain(MLA_Q_LORA),
        'w_uq': nrm((DEPTH, MLA_Q_LORA, MLA_HEADS * (MLA_NOPE + MLA_ROPE)), MLA_Q_LORA ** -0.5),
        'g_qn': gain(MLA_NOPE),
        'g_qr': gain(MLA_ROPE),
        'g_kr': gain(MLA_ROPE),
        'g_ckv': gain(MLA_KV_LORA),
        'w_ukv': nrm((DEPTH, MLA_KV_LORA, MLA_HEADS * (MLA_NOPE + MLA_V)), MLA_KV_LORA ** -0.5),
        'g_kn': gain(MLA_NOPE),
        'g_mla_out': gain(MLA_V),
        'w_out': nrm((DEPTH, MIX_W, D_MODEL), MIX_W ** -0.5),
        'g_cross_norm': gain(D_MODEL),
        'g_mem_norm': gain(D_MODEL),
        'w_cq': nrm((DEPTH, D_MODEL, MEM_W), D_MODEL ** -0.5),
        'w_ck': nrm((DEPTH, D_MODEL, MEM_W), D_MODEL ** -0.5),
        'w_cv': nrm((DEPTH, D_MODEL, MEM_W), D_MODEL ** -0.5),
        'g_cqn': gain(MEM_HD),
        'g_ckn': gain(MEM_HD),
        'w_co': nrm((DEPTH, MEM_W, D_MODEL), MEM_W ** -0.5),
        'g_ffn_norm': gain(D_MODEL),
        'w_gate': nrm((DEPTH, D_MODEL, D_FF), D_MODEL ** -0.5),
        'w_up': nrm((DEPTH, D_MODEL, D_FF), D_MODEL ** -0.5),
        'w_down': nrm((DEPTH, D_FF, D_MODEL), D_FF ** -0.5),
    }


def reference(x_prompt, x_sample, mem_prompt, cache_sb_k, cache_sb_v, state_gla, cache_mla_latent,
              cache_mla_krope, cache_mem_k, cache_mem_v, g_mix_norm, w_in, w_gla_gate, b_gla_gate,
              g_gla_out, g_sb_out, g_cq, w_uq, g_qn, g_qr, g_kr, g_ckv, w_ukv, g_kn, g_mla_out, w_out,
              g_cross_norm, g_mem_norm, w_cq, w_ck, w_cv, g_cqn, g_ckn, w_co, g_ffn_norm, w_gate,
              w_up, w_down):
    dt = x_prompt.dtype
    B = x_prompt.shape[0]
    empty_kv = jnp.zeros((B, 0, SB_HEADS, SB_HD), dt)
    empty_lat = jnp.zeros((B, 0, MLA_KV_LORA), dt)
    empty_kr = jnp.zeros((B, 0, MLA_ROPE), dt)
    zero_state = jnp.zeros((B, GLA_HEADS, GLA_DK, GLA_DV), dt)
    xp, xs = x_prompt, x_sample
    p_k, p_v, p_s, p_lat, p_kr, p_mk, p_mv = [], [], [], [], [], [], []
    s_k, s_v, s_s, s_lat, s_kr = [], [], [], [], []
    for l in range(DEPTH):
        w = {
            'g_mix_norm': g_mix_norm[l], 'w_in': w_in[l], 'w_gla_gate': w_gla_gate[l],
            'b_gla_gate': b_gla_gate[l], 'g_gla_out': g_gla_out[l], 'g_sb_out': g_sb_out[l],
            'g_cq': g_cq[l], 'w_uq': w_uq[l], 'g_qn': g_qn[l], 'g_qr': g_qr[l], 'g_kr': g_kr[l],
            'g_ckv': g_ckv[l], 'w_ukv': w_ukv[l], 'g_kn': g_kn[l], 'g_mla_out': g_mla_out[l],
            'w_out': w_out[l], 'g_cross_norm': g_cross_norm[l], 'g_mem_norm': g_mem_norm[l],
            'w_cq': w_cq[l], 'w_ck': w_ck[l], 'w_cv': w_cv[l], 'g_cqn': g_cqn[l], 'g_ckn': g_ckn[l],
            'w_co': w_co[l], 'g_ffn_norm': g_ffn_norm[l], 'w_gate': w_gate[l], 'w_up': w_up[l],
            'w_down': w_down[l],
        }
        mk, mv = memory_kv(mem_prompt, w)
        xp, ka, va, st, lat, kr = trunk_layer(xp, empty_kv, empty_kv, zero_state, empty_lat, empty_kr, mk, mv, w)
        p_k.append(ka)
        p_v.append(va)
        p_s.append(st)
        p_lat.append(lat)
        p_kr.append(kr)
        p_mk.append(mk)
        p_mv.append(mv)
        xs, ka, va, st, lat, kr = trunk_layer(xs, cache_sb_k[l], cache_sb_v[l], state_gla[l],
                                              cache_mla_latent[l], cache_mla_krope[l],
                                              cache_mem_k[l], cache_mem_v[l], w)
        s_k.append(ka)
        s_v.append(va)
        s_s.append(st)
        s_lat.append(lat)
        s_kr.append(kr)
    new_sb_k_prompt = jnp.stack(p_k)
    new_sb_v_prompt = jnp.stack(p_v)
    new_state_gla_prompt = jnp.stack(p_s)
    new_mla_latent_prompt = jnp.stack(p_lat)
    new_mla_krope_prompt = jnp.stack(p_kr)
    new_mem_k_prompt = jnp.stack(p_mk)
    new_mem_v_prompt = jnp.stack(p_mv)
    new_sb_k_sample = jnp.stack(s_k)
    new_sb_v_sample = jnp.stack(s_v)
    new_state_gla_sample = jnp.stack(s_s)
    new_mla_latent_sample = jnp.stack(s_lat)
    new_mla_krope_sample = jnp.stack(s_kr)
    return (xp, xs, new_sb_k_prompt, new_sb_v_prompt, new_state_gla_prompt, new_mla_latent_prompt,
            new_mla_krope_prompt, new_mem_k_prompt, new_mem_v_prompt, new_sb_k_sample, new_sb_v_sample,
            new_state_gla_sample, new_mla_latent_sample, new_mla_krope_sample)
```

```python
import functools
import math

import jax
import jax.numpy as jnp
import numpy as np
from jax import lax
from jax.experimental import pallas as pl
from jax.experimental.pallas import tpu as pltpu

F32 = jnp.float32
BF16 = jnp.bfloat16

D_MODEL = 1024
CHUNK = 64
SB_HEADS, SB_HD = 4, 64
SB_W = SB_HEADS * SB_HD
GLA_HEADS, GLA_DK, GLA_DV = 4, 32, 64
GLA_QK_W = GLA_HEADS * GLA_DK
GLA_W = GLA_HEADS * GLA_DV
GLA_GATE_RANK = 16
GLA_TAU = 16.0
GLA_BLOCK = 16
MLA_HEADS = 8
MLA_Q_LORA, MLA_KV_LORA = 256, 128
MLA_NOPE, MLA_ROPE, MLA_V = 64, 32, 64
MLA_W = MLA_HEADS * MLA_V
ROPE_THETA = 10000.0
MEM_LEN, MEM_HEADS, MEM_HD = 256, 4, 128
MEM_W = MEM_HEADS * MEM_HD
D_FF = 2816
NORM_EPS = 1e-6

LANES = 128
VMEM_LIMIT = 56 * 1024 * 1024
SB_SCALE = 1.0 / math.sqrt(SB_HD)
GLA_SCALE = GLA_DK ** -0.5
MLA_SCALE = (MLA_NOPE + MLA_ROPE) ** -0.5
MEM_SCALE = MEM_HD ** -0.5
SB_DEAD_LOG = -105.0
MASK_NEG = -1e30


def _dot(a, b):
    return jnp.dot(a, b, preferred_element_type=F32)


def _dot_nt(a, b):
    return lax.dot_general(a, b, (((1,), (1,)), ((), ())), preferred_element_type=F32)


def _dot_tn(a, b):
    return lax.dot_general(a, b, (((0,), (0,)), ((), ())), preferred_element_type=F32)


def _split(x):
    hi = x.astype(BF16)
    lo = (x - hi.astype(F32)).astype(BF16)
    return hi, lo


def _dot_split_lhs(x, m):
    hi, lo = _split(x)
    return _dot(hi, m) + _dot(lo, m)


def _dot_split_rhs(m, x):
    hi, lo = _split(x)
    return _dot(m, hi) + _dot(m, lo)


def _rms(x, g):
    ms = jnp.mean(x * x, axis=-1, keepdims=True)
    return x * lax.rsqrt(ms + NORM_EPS) * g


def _seg_rms(x, g, seg, width):
    ss = _dot_split_lhs(x * x, seg)
    return x * lax.rsqrt(ss * (1.0 / width) + NORM_EPS) * g


def _softplus(z):
    return jnp.maximum(z, 0.0) + jnp.log1p(jnp.exp(-jnp.abs(z)))


def _sigmoid(z):
    return 1.0 / (1.0 + jnp.exp(-z))


def _swap_halves(x, seg):
    n = x.shape[-1]
    half = seg // 2
    lane = lax.broadcasted_iota(jnp.int32, x.shape, x.ndim - 1)
    first = (lane & (seg - 1)) < half
    return jnp.where(first, pltpu.roll(x, n - half, x.ndim - 1), pltpu.roll(x, half, x.ndim - 1))


def _div_pow2(x, d):
    return lax.shift_right_logical(x, int(math.log2(d)))


def _seg_matrix(n, width):
    i = np.arange(n) // width
    return jnp.asarray(i[:, None] == i[None, :], dtype=BF16)


def _full(shape):
    nd = len(shape)
    return pl.BlockSpec(shape, lambda *_: (0,) * nd)


def _params(sem):
    return pltpu.CompilerParams(dimension_semantics=sem, vmem_limit_bytes=VMEM_LIMIT)


def _inproj_kernel(x_ref, gmix_ref, wsb_ref, wgl_ref, wag_ref, wml_ref, wgate_ref, bgate_ref,
                   gcq_ref, wuq_ref, gqn_ref, gqr_ref, gkr_ref, gckv_ref, seg64_ref, seg32_ref,
                   cos_ref, sin_ref,
                   qa_ref, ka_ref, va_ref, qg_ref, kg_ref, vg_ref, la_ref, rg_ref,
                   qn_ref, qr_ref, lat_ref, kr_ref, *, time_minor):
    tm = x_ref.shape[0]
    h = _rms(x_ref[...], gmix_ref[...]).astype(BF16)

    sb = _dot(h, wsb_ref[...])
    qa_ref[...] = (sb[:, :SB_W] * SB_SCALE).astype(BF16)
    ka = sb[:, SB_W:2 * SB_W]
    va = sb[:, 2 * SB_W:]
    if time_minor:
        ka_ref[...] = ka.T.reshape(SB_HEADS, SB_HD, tm)
        va_ref[...] = va.T.reshape(SB_HEADS, SB_HD, tm)
    else:
        ka_ref[...] = ka
        va_ref[...] = va

    gl = _dot(h, wgl_ref[...])
    qg_ref[...] = gl[:, :GLA_QK_W]
    kg_ref[...] = gl[:, GLA_QK_W:2 * GLA_QK_W]
    vg_ref[...] = gl[:, 2 * GLA_QK_W:2 * GLA_QK_W + GLA_W]
    rg_ref[...] = gl[:, 2 * GLA_QK_W + GLA_W:]
    ag = _dot(h, wag_ref[...])
    gate = _dot(ag.astype(BF16), wgate_ref[...]) + bgate_ref[...]
    la_ref[...] = -_softplus(-gate) * (1.0 / GLA_TAU)

    ml = _dot(h, wml_ref[...])
    cq = ml[:, :MLA_Q_LORA]
    ckv = ml[:, MLA_Q_LORA:MLA_Q_LORA + MLA_KV_LORA]
    krp = ml[:, MLA_Q_LORA + MLA_KV_LORA:]
    lat_ref[...] = _rms(ckv, gckv_ref[...])
    cos = cos_ref[...]
    sin = sin_ref[...]
    kr_ms = jnp.sum(krp * krp, axis=-1, keepdims=True) * (1.0 / MLA_ROPE)
    krn = krp * lax.rsqrt(kr_ms + NORM_EPS) * gkr_ref[...]
    kr_rot = krn * cos[:, :LANES] + _swap_halves(krn, MLA_ROPE) * sin[:, :LANES]
    if time_minor:
        kr_ref[...] = kr_rot.T[:MLA_ROPE, :]
    else:
        kr_ref[...] = kr_rot[:, :MLA_ROPE]

    q = _dot(_rms(cq, gcq_ref[...]).astype(BF16), wuq_ref[...])
    nope_w = MLA_HEADS * MLA_NOPE
    qn = _seg_rms(q[:, :nope_w], gqn_ref[...], seg64_ref[...], MLA_NOPE)
    qn_ref[...] = (qn * MLA_SCALE).astype(BF16)
    qr = _seg_rms(q[:, nope_w:], gqr_ref[...], seg32_ref[...], MLA_ROPE)
    qr = qr * cos + _swap_halves(qr, MLA_ROPE) * sin
    qr_ref[...] = (qr * MLA_SCALE).astype(BF16)


def _rope_tables(pos):
    half = MLA_ROPE // 2
    freqs = ROPE_THETA ** (-jnp.arange(half, dtype=F32) / half)
    ang = pos.astype(F32)[:, None] * freqs[None, :]
    cos = jnp.cos(ang)
    sin = jnp.sin(ang)
    cos_t = jnp.tile(jnp.concatenate([cos, cos], axis=-1), (1, MLA_HEADS))
    sin_t = jnp.tile(jnp.concatenate([-sin, sin], axis=-1), (1, MLA_HEADS))
    return cos_t, sin_t


def _inproj(x2d, B, T, past, wl, tm, time_minor):
    n = x2d.shape[0]
    tm = min(tm, n)
    cos_t, sin_t = _rope_tables(past + jnp.arange(T, dtype=jnp.int32))
    if T < tm:
        cos_t = jnp.tile(cos_t, (tm // T, 1))
        sin_t = jnp.tile(sin_t, (tm // T, 1))
    n_tab = cos_t.shape[0] // tm
    row = lambda w: pl.BlockSpec((tm, w), lambda i: (i, 0))
    tab = pl.BlockSpec((tm, 2 * LANES), lambda i: (i % n_tab, 0))
    weights = (wl['g_mix'], wl['w_sb'], wl['w_gl'], wl['w_ag'], wl['w_ml'], wl['w_gate'], wl['b_gate'],
               wl['g_cq'], wl['w_uq'], wl['g_qn'], wl['g_qr'], wl['g_kr'], wl['g_ckv'],
               wl['seg64_512'], wl['seg32_256'])
    rowout = lambda w, dt: (row(w), jax.ShapeDtypeStruct((n, w), dt))
    if time_minor:
        assert T % tm == 0
        tpb = T // tm
        kv_out = (pl.BlockSpec((None, SB_HEADS, SB_HD, tm), lambda i: (i // tpb, 0, 0, i % tpb)),
                  jax.ShapeDtypeStruct((B, SB_HEADS, SB_HD, T), F32))
        kr_out = (pl.BlockSpec((None, MLA_ROPE, tm), lambda i: (i // tpb, 0, i % tpb)),
                  jax.ShapeDtypeStruct((B, MLA_ROPE, T), F32))
    else:
        kv_out = rowout(SB_W, F32)
        kr_out = rowout(MLA_ROPE, F32)
    outs = [rowout(SB_W, BF16), kv_out, kv_out,
            rowout(GLA_QK_W, F32), rowout(GLA_QK_W, F32), rowout(GLA_W, F32), rowout(GLA_QK_W, F32),
            rowout(GLA_W, F32), rowout(MLA_HEADS * MLA_NOPE, BF16), rowout(MLA_HEADS * MLA_ROPE, BF16),
            rowout(MLA_KV_LORA, F32), kr_out]
    return pl.pallas_call(
        functools.partial(_inproj_kernel, time_minor=time_minor),
        grid=(n // tm,),
        in_specs=[row(D_MODEL)] + [_full(w.shape) for w in weights] + [tab, tab],
        out_specs=[o[0] for o in outs],
        out_shape=[o[1] for o in outs],
        compiler_params=_params(("parallel",)),
        name="inproj",
    )(x2d, *weights, cos_t, sin_t)


def _sb_kernel(*refs, tq, tk, has_past, n_past_blocks):
    if has_past:
        q_ref, kn_ref, vn_ref, kt_ref, vt_ref, g_ref, seg_ref, o_ref = refs
    else:
        q_ref, kt_ref, vt_ref, g_ref, seg_ref, o_ref = refs
    qi = pl.program_id(1)
    q = q_ref[...]
    lane = lax.broadcasted_iota(jnp.int32, (1, SB_W), 1)
    hmask = [(lane >= SB_HD * h) & (lane < SB_HD * (h + 1)) for h in range(SB_HEADS)]
    qh = [jnp.where(hmask[h], q, jnp.zeros_like(q)) for h in range(SB_HEADS)]

    def suffix_matrix(n):
        return (lax.broadcasted_iota(jnp.int32, (n, n), 0)
                > lax.broadcasted_iota(jnp.int32, (n, n), 1)).astype(BF16)

    def time_minor_block(ref, start, n):
        return ref[:, :, pl.ds(start, n)].reshape(SB_W, n).astype(BF16)

    def block(kblk, vblk, time_minor, vis, umat, cs, acc):
        new_cs = []
        for h in range(SB_HEADS):
            z = _dot(qh[h], kblk) if time_minor else _dot_nt(qh[h], kblk)
            lk = -_softplus(z)
            if vis is not None:
                lk = jnp.where(vis, lk, 0.0)
            later = _dot_split_lhs(lk, umat)
            w = jnp.exp(z + lk + later + cs[h])
            if vis is not None:
                w = jnp.where(vis, w, 0.0)
            w = w.astype(BF16)
            pv = _dot_nt(w, vblk) if time_minor else _dot(w, vblk)
            acc = acc + jnp.where(hmask[h], pv, 0.0)
            new_cs.append(cs[h] + later[:, :1] + lk[:, :1])
        return new_cs, acc

    def dead(cs):
        m = cs[0]
        for c in cs[1:]:
            m = jnp.maximum(m, c)
        return jnp.max(m)

    vis = (lax.broadcasted_iota(jnp.int32, (tq, tq), 1)
           < lax.broadcasted_iota(jnp.int32, (tq, tq), 0))
    zero_c = jnp.zeros((tq, 1), F32)
    init = ([zero_c] * SB_HEADS, jnp.zeros((tq, SB_W), F32))
    if has_past:
        cs, acc = block(kn_ref[...].astype(BF16), vn_ref[...].astype(BF16), False, vis,
                        suffix_matrix(tq), *init)
        kb0 = jnp.int32(n_past_blocks - 1)
    else:
        q0 = pl.multiple_of(qi * tq, tq)
        cs, acc = block(time_minor_block(kt_ref, q0, tq), time_minor_block(vt_ref, q0, tq), True, vis,
                        suffix_matrix(tq), *init)
        kb0 = qi - 1

    umat = suffix_matrix(tk)

    def cond(st):
        return jnp.logical_and(st[0] >= 0, st[1] > SB_DEAD_LOG)

    def body(st):
        kb = st[0]
        cs, acc = list(st[2:2 + SB_HEADS]), st[2 + SB_HEADS]
        start = pl.multiple_of(kb * tk, tk)
        cs, acc = block(time_minor_block(kt_ref, start, tk), time_minor_block(vt_ref, start, tk), True,
                        None, umat, cs, acc)
        return (kb - 1, dead(cs), *cs, acc)

    st = lax.while_loop(cond, body, (kb0, dead(cs), *cs, acc))
    acc = st[2 + SB_HEADS]
    o_ref[...] = _seg_rms(acc, g_ref[...], seg_ref[...], SB_HD).astype(BF16)


def _sb_attn(q, k_new, v_new, kt_cache, vt_cache, layer, wl, tq, tk):
    B, T, _ = q.shape
    tq = min(tq, T)
    has_past = kt_cache is not None
    n_past_blocks = 0
    in_specs = [pl.BlockSpec((None, tq, SB_W), lambda b, i: (b, i, 0))]
    if has_past:
        P = kt_cache.shape[-1]
        assert tq == T and P % tk == 0
        n_past_blocks = P // tk
        new = pl.BlockSpec((None, T, SB_W), lambda b, i: (b, 0, 0))
        past = pl.BlockSpec((None, None, SB_HEADS, SB_HD, P), lambda b, i: (layer, b, 0, 0, 0))
        in_specs += [new, new, past, past]
        args = [q, k_new, v_new, kt_cache, vt_cache]
    else:
        assert tq == tk
        new = pl.BlockSpec((None, SB_HEADS, SB_HD, T), lambda b, i: (b, 0, 0, 0))
        in_specs += [new, new]
        args = [q, k_new, v_new]
    in_specs += [_full((1, SB_W)), _full((SB_W, SB_W))]
    args += [wl['g_sb_out'], wl['seg64_256']]
    return pl.pallas_call(
        functools.partial(_sb_kernel, tq=tq, tk=tk, has_past=has_past, n_past_blocks=n_past_blocks),
        grid=(B, T // tq),
        in_specs=in_specs,
        out_specs=pl.BlockSpec((None, tq, SB_W), lambda b, i: (b, i, 0)),
        out_shape=jax.ShapeDtypeStruct((B, T, SB_W), BF16),
        compiler_params=_params(("parallel", "parallel")),
        name="sb_attn",
    )(*args)


def _gla_kernel(q_ref, k_ref, v_ref, la_ref, rg_ref, s0_ref, g_ref, seg_ref, o_ref, s_ref, *, tc):
    ci = pl.program_id(1)

    @pl.when(ci == 0)
    def _():
        s_ref[...] = s0_ref[...]

    nsub = tc // GLA_BLOCK
    q, k, la = q_ref[...], k_ref[...], la_ref[...]
    v = v_ref[...]
    r_i = lax.broadcasted_iota(jnp.int32, (tc, tc), 0)
    c_i = lax.broadcasted_iota(jnp.int32, (tc, tc), 1)
    same = _div_pow2(r_i, GLA_BLOCK) == _div_pow2(c_i, GLA_BLOCK)
    causal = jnp.logical_and(same, c_i <= r_i)
    b = _dot_split_rhs(causal.astype(BF16), la)
    bl = _dot_split_rhs(same.astype(BF16), la)
    qt = (q * jnp.exp(b) * GLA_SCALE).astype(BF16)
    kt = (k * jnp.exp(-b)).astype(BF16)
    kd = (k * jnp.exp(bl - b)).astype(BF16)
    vb = v.astype(BF16)
    decay = jnp.exp(bl)

    lane_k = lax.broadcasted_iota(jnp.int32, (1, GLA_QK_W), 1)
    lane_v = lax.broadcasted_iota(jnp.int32, (1, GLA_W), 1)
    intra = jnp.zeros((tc, GLA_W), F32)
    for h in range(GLA_HEADS):
        km = (lane_k >= GLA_DK * h) & (lane_k < GLA_DK * (h + 1))
        vm = (lane_v >= GLA_DV * h) & (lane_v < GLA_DV * (h + 1))
        att = _dot_nt(jnp.where(km, qt, jnp.zeros_like(qt)), kt)
        att = jnp.where(causal, att, 0.0).astype(BF16)
        intra = intra + jnp.where(vm, _dot(att, vb), 0.0)

    sr = _div_pow2(lax.broadcasted_iota(jnp.int32, (GLA_QK_W, GLA_W), 0), GLA_DK)
    sc = _div_pow2(lax.broadcasted_iota(jnp.int32, (GLA_QK_W, GLA_W), 1), GLA_DV)
    bd = sr == sc
    eye = (lax.broadcasted_iota(jnp.int32, (GLA_QK_W, GLA_QK_W), 0)
           == lax.broadcasted_iota(jnp.int32, (GLA_QK_W, GLA_QK_W), 1))
    state = s_ref[...]
    inter = []
    for j in range(nsub):
        rows = slice(j * GLA_BLOCK, (j + 1) * GLA_BLOCK)
        inter.append(_dot(qt[rows], state.astype(BF16)))
        upd = _dot_tn(kd[rows], vb[rows])
        d_row = decay[j * GLA_BLOCK:j * GLA_BLOCK + 1]
        d_col = jnp.sum(jnp.where(eye, d_row, 0.0), axis=1, keepdims=True)
        state = d_col * state + jnp.where(bd, upd, 0.0)
    s_ref[...] = state
    o = intra + jnp.concatenate(inter, axis=0)
    o = _seg_rms(o, g_ref[...], seg_ref[...], GLA_DV)
    rg = rg_ref[...]
    o_ref[...] = (o * (rg * _sigmoid(rg))).astype(BF16)


def _gla(qg, kg, vg, la, rg, s0_bd, wl, tc):
    B, T, _ = qg.shape
    tc = min(tc, T)
    blk = lambda w: pl.BlockSpec((None, tc, w), lambda b, c: (b, c, 0))
    st = pl.BlockSpec((None, GLA_QK_W, GLA_W), lambda b, c: (b, 0, 0))
    return pl.pallas_call(
        functools.partial(_gla_kernel, tc=tc),
        grid=(B, T // tc),
        in_specs=[blk(GLA_QK_W), blk(GLA_QK_W), blk(GLA_W), blk(GLA_QK_W), blk(GLA_W), st,
                  _full((1, GLA_W)), _full((GLA_W, GLA_W))],
        out_specs=[blk(GLA_W), st],
        out_shape=[jax.ShapeDtypeStruct((B, T, GLA_W), BF16),
                   jax.ShapeDtypeStruct((B, GLA_QK_W, GLA_W), F32)],
        compiler_params=_params(("parallel", "arbitrary")),
        name="gla",
    )(qg, kg, vg, la, rg, s0_bd, wl['g_gla_out'], wl['seg64_256'])


def _state_to_blockdiag(s):
    B = s.shape[0]
    eye = jnp.eye(GLA_HEADS, dtype=s.dtype)
    return jnp.einsum('bhkv,hg->bhkgv', s, eye).reshape(B, GLA_QK_W, GLA_W)


def _state_from_blockdiag(s):
    B = s.shape[0]
    s5 = s.reshape(B, GLA_HEADS, GLA_DK, GLA_HEADS, GLA_DV)
    return jnp.stack([s5[:, h, :, h, :] for h in range(GLA_HEADS)], axis=1)


def _mla_kv_kernel(lat_ref, kr_ref, wn_ref, wv_ref, gkn_ref, seg_ref, place_ref, kcat_ref, v_ref):
    lat = lat_ref[...].astype(BF16)
    kn = _seg_rms(_dot(lat, wn_ref[...]), gkn_ref[...], seg_ref[...], MLA_NOPE)
    v = _dot(lat, wv_ref[...])
    krp = _dot_tn(kr_ref[...].astype(BF16), place_ref[...])
    lane = lax.broadcasted_iota(jnp.int32, (1, LANES), 1)
    for p in range(MLA_HEADS // 2):
        pair = kn[:, LANES * p:LANES * (p + 1)]
        kcat_ref[2 * p] = jnp.where(lane < MLA_NOPE, pair, krp).astype(BF16)
        kcat_ref[2 * p + 1] = jnp.where(lane < MLA_NOPE, pltpu.roll(pair, MLA_NOPE, 1), krp).astype(BF16)
        v_ref[p] = v[:, LANES * p:LANES * (p + 1)].astype(BF16)


def _mla_kv(lat, kr_t, layer, wl, tr):
    B, L, _ = lat.shape[-3:]
    tr = min(tr, L)
    if layer is None:
        lat_spec = pl.BlockSpec((None, tr, MLA_KV_LORA), lambda b, i: (b, i, 0))
        kr_spec = pl.BlockSpec((None, MLA_ROPE, tr), lambda b, i: (b, 0, i))
    else:
        lat_spec = pl.BlockSpec((None, None, tr, MLA_KV_LORA), lambda b, i: (layer, b, i, 0))
        kr_spec = pl.BlockSpec((None, None, MLA_ROPE, tr), lambda b, i: (layer, b, 0, i))
    return pl.pallas_call(
        _mla_kv_kernel,
        grid=(B, L // tr),
        in_specs=[lat_spec, kr_spec,
                  _full(wl['w_ukn'].shape), _full(wl['w_ukv_v'].shape), _full((1, MLA_HEADS * MLA_NOPE)),
                  _full(wl['seg64_512'].shape), _full((MLA_ROPE, LANES))],
        out_specs=[pl.BlockSpec((None, MLA_HEADS, tr, LANES), lambda b, i: (b, 0, i, 0)),
                   pl.BlockSpec((None, MLA_HEADS // 2, tr, LANES), lambda b, i: (b, 0, i, 0))],
        out_shape=[jax.ShapeDtypeStruct((B, MLA_HEADS, L, LANES), BF16),
                   jax.ShapeDtypeStruct((B, MLA_HEADS // 2, L, LANES), BF16)],
        compiler_params=_params(("parallel", "parallel")),
        name="mla_kv",
    )(lat, kr_t, wl['w_ukn'], wl['w_ukv_v'], wl['g_kn'], wl['seg64_512'], wl['rope_place'])


def _mla_attn_kernel(q_ref, km_ref, vm_ref, kd_ref, vd_ref, g_ref, seg_ref, o_ref, *, tq, tk, n_main):
    qi = pl.program_id(2)
    n_blocks = qi if n_main is None else n_main
    vis = (_div_pow2(lax.broadcasted_iota(jnp.int32, (tq, tq), 1), CHUNK)
           <= _div_pow2(lax.broadcasted_iota(jnp.int32, (tq, tq), 0), CHUNK))
    vd = vd_ref[...]
    outs = []
    for hh in range(2):
        q = q_ref[hh]
        s = jnp.where(vis, _dot_nt(q, kd_ref[hh]), MASK_NEG)
        m = jnp.max(s, axis=-1, keepdims=True)
        p = jnp.exp(s - m)
        l = jnp.sum(p, axis=-1, keepdims=True)
        acc = _dot(p.astype(BF16), vd)

        def body(kb, st, q=q, hh=hh):
            m, l, acc = st
            start = pl.multiple_of(kb * tk, tk)
            s = _dot_nt(q, km_ref[hh, pl.ds(start, tk), :])
            m_new = jnp.maximum(m, jnp.max(s, axis=-1, keepdims=True))
            a = jnp.exp(m - m_new)
            p = jnp.exp(s - m_new)
            l = a * l + jnp.sum(p, axis=-1, keepdims=True)
            acc = a * acc + _dot(p.astype(BF16), vm_ref[pl.ds(start, tk), :])
            return m_new, l, acc

        m, l, acc = lax.fori_loop(0, n_blocks, body, (m, l, acc))
        outs.append(acc / l)
    lane = lax.broadcasted_iota(jnp.int32, (1, LANES), 1)
    o = jnp.where(lane < MLA_V, outs[0], outs[1])
    o_ref[...] = _seg_rms(o, g_ref[...], seg_ref[...], MLA_V).astype(BF16)


def _mla_attn(qcat, k_main, v_main, k_diag, v_diag, wl, tq, tk, causal):
    B, _, T, _ = qcat.shape
    tq = min(tq, T)
    Lm = k_main.shape[2]
    if causal:
        assert tq == tk
        n_main = None
        diag_map = lambda b, p, i: (b, p, i, 0)
    else:
        assert tq == T and Lm % tk == 0
        n_main = Lm // tk
        diag_map = lambda b, p, i: (b, p, 0, 0)
    return pl.pallas_call(
        functools.partial(_mla_attn_kernel, tq=tq, tk=tk, n_main=n_main),
        grid=(B, MLA_HEADS // 2, T // tq),
        in_specs=[pl.BlockSpec((None, 2, tq, LANES), lambda b, p, i: (b, p, i, 0)),
                  pl.BlockSpec((None, 2, Lm, LANES), lambda b, p, i: (b, p, 0, 0)),
                  pl.BlockSpec((None, None, Lm, LANES), lambda b, p, i: (b, p, 0, 0)),
                  pl.BlockSpec((None, 2, tq, LANES), diag_map),
                  pl.BlockSpec((None, None, tq, LANES), diag_map),
                  _full((1, LANES)), _full((LANES, LANES))],
        out_specs=pl.BlockSpec((None, tq, LANES), lambda b, p, i: (b, i, p)),
        out_shape=jax.ShapeDtypeStruct((B, T, MLA_W), BF16),
        compiler_params=_params(("parallel", "parallel", "parallel")),
        name="mla_attn",
    )(qcat, k_main, v_main, k_diag, v_diag, wl['g_mla_out'], wl['seg64_128'])


def _post_kernel(x_ref, oa_ref, ob_ref, oc_ref, mk_ref, mv_ref, wo_ref, gcross_ref, wcq_ref, gcqn_ref,
                 wco_ref, o_ref, att_ref, *, nb, tt):
    wo = wo_ref
    mix = (_dot(oa_ref[...], wo[:SB_W, :]) + _dot(ob_ref[...], wo[SB_W:SB_W + GLA_W, :])
           + _dot(oc_ref[...], wo[SB_W + GLA_W:, :]))
    x1 = x_ref[...] + mix
    h = _rms(x1, gcross_ref[...]).astype(BF16)
    q = _dot(h, wcq_ref[...])
    gq = gcqn_ref[...]
    qs = []
    for hd in range(MEM_HEADS):
        qh = _rms(q[:, MEM_HD * hd:MEM_HD * (hd + 1)], gq).astype(BF16)
        qs.append(qh)
    for bi in range(nb):
        rows = slice(bi * tt, (bi + 1) * tt)
        for hd in range(MEM_HEADS):
            cols = slice(MEM_HD * hd, MEM_HD * (hd + 1))
            kh = mk_ref[bi, :, cols].astype(BF16)
            vh = mv_ref[bi, :, cols].astype(BF16)
            s = _dot_nt(qs[hd][rows], kh) * MEM_SCALE
            m = jnp.max(s, axis=-1, keepdims=True)
            p = jnp.exp(s - m)
            l = jnp.sum(p, axis=-1, keepdims=True)
            att_ref[rows, cols] = _dot(p.astype(BF16), vh) / l
    o_ref[...] = x1 + _dot(att_ref[...].astype(BF16), wco_ref[...])


def _post(x2d, oa, ob, oc, mem_k, mem_v, T, wl, tm):
    n = x2d.shape[0]
    tm = min(tm, n)
    tt = min(T, tm)
    nb = tm // tt
    tiles_per_b = T // tt
    row = lambda w: pl.BlockSpec((tm, w), lambda i: (i, 0))
    mem = pl.BlockSpec((nb, MEM_LEN, MEM_W), lambda i: (i // tiles_per_b if nb == 1 else i, 0, 0))
    weights = (wl['w_out'], wl['g_cross'], wl['w_cq'], wl['g_cqn'], wl['w_co'])
    return pl.pallas_call(
        functools.partial(_post_kernel, nb=nb, tt=tt),
        grid=(n // tm,),
        in_specs=[row(D_MODEL), row(SB_W), row(GLA_W), row(MLA_W), mem, mem] + [_full(w.shape) for w in weights],
        out_specs=row(D_MODEL),
        out_shape=jax.ShapeDtypeStruct((n, D_MODEL), F32),
        scratch_shapes=[pltpu.VMEM((tm, MEM_W), F32)],
        compiler_params=_params(("parallel",)),
        name="post",
    )(x2d, oa, ob, oc, mem_k, mem_v, *weights)


def _ffn_kernel(x_ref, g_ref, wg_ref, wu_ref, wd_ref, o_ref, h_ref, acc_ref):
    j = pl.program_id(1)

    @pl.when(j == 0)
    def _():
        x = x_ref[...]
        h_ref[...] = _rms(x, g_ref[...]).astype(BF16)
        acc_ref[...] = x

    h = h_ref[...]
    gate = _dot(h, wg_ref[...])
    up = _dot(h, wu_ref[...])
    act = (gate * _sigmoid(gate) * up).astype(BF16)
    acc_ref[...] += _dot(act, wd_ref[...])

    @pl.when(j == pl.num_programs(1) - 1)
    def _():
        o_ref[...] = acc_ref[...]


def _ffn(x2d, wl, tm, tf):
    n = x2d.shape[0]
    tm = min(tm, n)
    return pl.pallas_call(
        _ffn_kernel,
        grid=(n // tm, D_FF // tf),
        in_specs=[pl.BlockSpec((tm, D_MODEL), lambda i, j: (i, 0)), _full((1, D_MODEL)),
                  pl.BlockSpec((D_MODEL, tf), lambda i, j: (0, j)),
                  pl.BlockSpec((D_MODEL, tf), lambda i, j: (0, j)),
                  pl.BlockSpec((tf, D_MODEL), lambda i, j: (j, 0))],
        out_specs=pl.BlockSpec((tm, D_MODEL), lambda i, j: (i, 0)),
        out_shape=jax.ShapeDtypeStruct((n, D_MODEL), F32),
        scratch_shapes=[pltpu.VMEM((tm, D_MODEL), BF16), pltpu.VMEM((tm, D_MODEL), F32)],
        compiler_params=_params(("parallel", "arbitrary")),
        name="ffn",
    )(x2d, wl['g_ffn'], wl['w_gate_ffn'], wl['w_up'], wl['w_down'])


def _memkv_kernel(m_ref, g_ref, wk_ref, wv_ref, gk_ref, k_ref, v_ref):
    h = _rms(m_ref[...], g_ref[...]).astype(BF16)
    k = _dot(h, wk_ref[...])
    gk = gk_ref[...]
    for hd in range(MEM_HEADS):
        cols = slice(MEM_HD * hd, MEM_HD * (hd + 1))
        k_ref[:, cols] = _rms(k[:, cols], gk)
    v_ref[...] = _dot(h, wv_ref[...])


def _memkv(mem2d, wl, tm):
    n = mem2d.shape[0]
    tm = min(tm, n)
    row = lambda w: pl.BlockSpec((tm, w), lambda i: (i, 0))
    weights = (wl['g_mem'], wl['w_ck'], wl['w_cv'], wl['g_ckn'])
    return pl.pallas_call(
        _memkv_kernel,
        grid=(n // tm,),
        in_specs=[row(D_MODEL)] + [_full(w.shape) for w in weights],
        out_specs=[row(MEM_W), row(MEM_W)],
        out_shape=[jax.ShapeDtypeStruct((n, MEM_W), F32)] * 2,
        compiler_params=_params(("parallel",)),
        name="memkv",
    )(mem2d, *weights)


def _prep_layer(p, l):
    w_in = p['w_in'][l]
    o = np.cumsum([0, SB_W, SB_W, SB_W, GLA_QK_W, GLA_QK_W, GLA_W, GLA_GATE_RANK, GLA_W,
                   MLA_Q_LORA, MLA_KV_LORA, MLA_ROPE])
    seg = lambda i, j: w_in[:, o[i]:o[j]]
    pad_cols = lambda w, n: jnp.pad(w, ((0, 0), (0, n - w.shape[1])))
    row = lambda g, reps=1: jnp.tile(g, reps)[None, :].astype(F32)
    w_uq = p['w_uq'][l].reshape(MLA_Q_LORA, MLA_HEADS, MLA_NOPE + MLA_ROPE)
    w_ukv = p['w_ukv'][l].reshape(MLA_KV_LORA, MLA_HEADS, MLA_NOPE + MLA_V)
    place = np.zeros((MLA_ROPE, LANES), np.float32)
    place[np.arange(MLA_ROPE), MLA_NOPE + np.arange(MLA_ROPE)] = 1.0
    return {
        'g_mix': row(p['g_mix_norm'][l]),
        'w_sb': seg(0, 3).astype(BF16),
        'w_gl': jnp.concatenate([seg(3, 6), seg(7, 8)], axis=1).astype(BF16),
        'w_ag': pad_cols(seg(6, 7), LANES).astype(BF16),
        'w_ml': pad_cols(seg(8, 11), 4 * LANES).astype(BF16),
        'w_gate': jnp.pad(p['w_gla_gate'][l], ((0, LANES - GLA_GATE_RANK), (0, 0))).astype(BF16),
        'b_gate': row(p['b_gla_gate'][l]),
        'g_cq': row(p['g_cq'][l]),
        'w_uq': jnp.concatenate([w_uq[:, :, :MLA_NOPE].reshape(MLA_Q_LORA, -1),
                                 w_uq[:, :, MLA_NOPE:].reshape(MLA_Q_LORA, -1)], axis=1).astype(BF16),
        'g_qn': row(p['g_qn'][l], MLA_HEADS),
        'g_qr': row(p['g_qr'][l], MLA_HEADS),
        'g_kr': jnp.pad(p['g_kr'][l], (0, LANES - MLA_ROPE))[None, :],
        'g_ckv': row(p['g_ckv'][l]),
        'w_ukn': w_ukv[:, :, :MLA_NOPE].reshape(MLA_KV_LORA, -1).astype(BF16),
        'w_ukv_v': w_ukv[:, :, MLA_NOPE:].reshape(MLA_KV_LORA, -1).astype(BF16),
        'g_kn': row(p['g_kn'][l], MLA_HEADS),
        'g_mla_out': row(p['g_mla_out'][l], 2),
        'g_sb_out': row(p['g_sb_out'][l], SB_HEADS),
        'g_gla_out': row(p['g_gla_out'][l], GLA_HEADS),
        'w_out': p['w_out'][l].astype(BF16),
        'g_cross': row(p['g_cross_norm'][l]),
        'g_mem': row(p['g_mem_norm'][l]),
        'w_cq': p['w_cq'][l].astype(BF16),
        'w_ck': p['w_ck'][l].astype(BF16),
        'w_cv': p['w_cv'][l].astype(BF16),
        'g_cqn': row(p['g_cqn'][l]),
        'g_ckn': row(p['g_ckn'][l]),
        'w_co': p['w_co'][l].astype(BF16),
        'g_ffn': row(p['g_ffn_norm'][l]),
        'w_gate_ffn': p['w_gate'][l].astype(BF16),
        'w_up': p['w_up'][l].astype(BF16),
        'w_down': p['w_down'][l].astype(BF16),
        'seg64_512': _seg_matrix(512, 64),
        'seg32_256': _seg_matrix(256, 32),
        'seg64_256': _seg_matrix(256, 64),
        'seg64_128': _seg_matrix(128, 64),
        'rope_place': jnp.asarray(place, dtype=BF16),
    }


def _head_major_q(qn, qr, B, T):
    qn = qn.reshape(B, T, MLA_HEADS, MLA_NOPE)
    qr = qr.reshape(B, T, MLA_HEADS, MLA_ROPE)
    pad = jnp.zeros((B, T, MLA_HEADS, LANES - MLA_NOPE - MLA_ROPE), qn.dtype)
    return jnp.concatenate([qn, qr, pad], axis=-1).transpose(0, 2, 1, 3)


def _time_minor_caches(cache_sb_k, cache_sb_v, cache_mla_latent, cache_mla_krope):
    return (jnp.transpose(cache_sb_k, (0, 1, 3, 4, 2)), jnp.transpose(cache_sb_v, (0, 1, 3, 4, 2)),
            cache_mla_latent, jnp.transpose(cache_mla_krope, (0, 1, 3, 2)))


def _trunk_layer(x, caches, layer, gla_s0, mem_k, mem_v, wl):
    B, T, _ = x.shape
    has_past = caches is not None
    past = caches[0].shape[-1] if has_past else 0
    x2d = x.reshape(B * T, D_MODEL)
    (qa, ka, va, qg, kg, vg, la, rg, qn, qr, lat, kr) = _inproj(x2d, B, T, past, wl, tm=512,
                                                                time_minor=not has_past)
    r3 = lambda a: a.reshape(B, T, a.shape[-1])

    if has_past:
        sb_kt, sb_vt, lat_cache, kr_t_cache = caches
        o_a = _sb_attn(r3(qa), r3(ka), r3(va), sb_kt, sb_vt, layer, wl, tq=128, tk=128)
        ka_out = ka.reshape(B, T, SB_HEADS, SB_HD)
        va_out = va.reshape(B, T, SB_HEADS, SB_HD)
        kr_out = r3(kr)
        kr_t = jnp.transpose(kr_out, (0, 2, 1))
    else:
        o_a = _sb_attn(r3(qa), ka, va, None, None, None, wl, tq=128, tk=128)
        ka_out = jnp.transpose(ka, (0, 3, 1, 2))
        va_out = jnp.transpose(va, (0, 3, 1, 2))
        kr_t = kr
        kr_out = jnp.transpose(kr, (0, 2, 1))

    if gla_s0 is None:
        s0 = jnp.zeros((B, GLA_QK_W, GLA_W), F32)
    else:
        s0 = _state_to_blockdiag(gla_s0)
    o_b, s_bd = _gla(r3(qg), r3(kg), r3(vg), r3(la), r3(rg), s0, wl, tc=128)
    gla_s = _state_from_blockdiag(s_bd)

    qcat = _head_major_q(qn, qr, B, T)
    kcat_new, v_new = _mla_kv(r3(lat), kr_t, None, wl, tr=512)
    if has_past:
        kcat_past, v_past = _mla_kv(lat_cache, kr_t_cache, layer, wl, tr=512)
        o_c = _mla_attn(qcat, kcat_past, v_past, kcat_new, v_new, wl, tq=T, tk=256, causal=False)
    else:
        o_c = _mla_attn(qcat, kcat_new, v_new, kcat_new, v_new, wl, tq=256, tk=256, causal=True)

    x2d = _post(x2d, o_a.reshape(B * T, SB_W), o_b.reshape(B * T, GLA_W), o_c.reshape(B * T, MLA_W),
                mem_k.reshape(B, MEM_LEN, MEM_W), mem_v.reshape(B, MEM_LEN, MEM_W), T, wl, tm=512)
    x2d = _ffn(x2d, wl, tm=512, tf=D_FF // 2)
    return x2d.reshape(B, T, D_MODEL), ka_out, va_out, gla_s, r3(lat), kr_out


def kernel(x_prompt, x_sample, mem_prompt, cache_sb_k, cache_sb_v, state_gla, cache_mla_latent,
           cache_mla_krope, cache_mem_k, cache_mem_v, g_mix_norm, w_in, w_gla_gate, b_gla_gate,
           g_gla_out, g_sb_out, g_cq, w_uq, g_qn, g_qr, g_kr, g_ckv, w_ukv, g_kn, g_mla_out, w_out,
           g_cross_norm, g_mem_norm, w_cq, w_ck, w_cv, g_cqn, g_ckn, w_co, g_ffn_norm, w_gate,
           w_up, w_down):
    p = dict(g_mix_norm=g_mix_norm, w_in=w_in, w_gla_gate=w_gla_gate, b_gla_gate=b_gla_gate,
             g_gla_out=g_gla_out, g_sb_out=g_sb_out, g_cq=g_cq, w_uq=w_uq, g_qn=g_qn, g_qr=g_qr,
             g_kr=g_kr, g_ckv=g_ckv, w_ukv=w_ukv, g_kn=g_kn, g_mla_out=g_mla_out, w_out=w_out,
             g_cross_norm=g_cross_norm, g_mem_norm=g_mem_norm, w_cq=w_cq, w_ck=w_ck, w_cv=w_cv,
             g_cqn=g_cqn, g_ckn=g_ckn, w_co=w_co, g_ffn_norm=g_ffn_norm, w_gate=w_gate, w_up=w_up,
             w_down=w_down)
    depth = w_in.shape[0]
    B = x_prompt.shape[0]
    xp, xs = x_prompt, x_sample
    outs_p = [[] for _ in range(7)]
    outs_s = [[] for _ in range(5)]
    caches = _time_minor_caches(cache_sb_k, cache_sb_v, cache_mla_latent, cache_mla_krope)
    for l in range(depth):
        wl = _prep_layer(p, l)
        mk, mv = _memkv(mem_prompt.reshape(B * MEM_LEN, D_MODEL), wl, tm=512)
        mk = mk.reshape(B, MEM_LEN, MEM_HEADS, MEM_HD)
        mv = mv.reshape(B, MEM_LEN, MEM_HEADS, MEM_HD)
        xp, ka, va, st, lat, kr = _trunk_layer(xp, None, l, None, mk, mv, wl)
        for lst, val in zip(outs_p, (ka, va, st, lat, kr, mk, mv)):
            lst.append(val)
        xs, ka, va, st, lat, kr = _trunk_layer(xs, caches, l, state_gla[l], cache_mem_k[l], cache_mem_v[l], wl)
        for lst, val in zip(outs_s, (ka, va, st, lat, kr)):
            lst.append(val)
    return (xp, xs, *[jnp.stack(v) for v in outs_p], *[jnp.stack(v) for v in outs_s])
```

```python
import functools
import math

import jax
import jax.numpy as jnp
import numpy as np
from jax import lax
from jax.experimental import pallas as pl
from jax.experimental.pallas import tpu as pltpu

F32 = jnp.float32
BF16 = jnp.bfloat16

D_MODEL = 1024
CHUNK = 64
SB_HEADS, SB_HD = 4, 64
SB_W = SB_HEADS * SB_HD
GLA_HEADS, GLA_DK, GLA_DV = 4, 32, 64
GLA_QK_W = GLA_HEADS * GLA_DK
GLA_W = GLA_HEADS * GLA_DV
GLA_GATE_RANK = 16
GLA_TAU = 16.0
GLA_BLOCK = 16
MLA_HEADS = 8
MLA_Q_LORA, MLA_KV_LORA = 256, 128
MLA_NOPE, MLA_ROPE, MLA_V = 64, 32, 64
MLA_W = MLA_HEADS * MLA_V
ROPE_THETA = 10000.0
MEM_LEN, MEM_HEADS, MEM_HD = 256, 4, 128
MEM_W = MEM_HEADS * MEM_HD
D_FF = 2816
NORM_EPS = 1e-6

LANES = 128
VMEM_LIMIT = 56 * 1024 * 1024
SB_SCALE = 1.0 / math.sqrt(SB_HD)
GLA_SCALE = GLA_DK ** -0.5
MLA_SCALE = (MLA_NOPE + MLA_ROPE) ** -0.5
MEM_SCALE = MEM_HD ** -0.5
SB_DEAD_LOG = -105.0
MASK_NEG = -1e30


def _dot(a, b):
    return jnp.dot(a, b, preferred_element_type=F32)


def _dot_nt(a, b):
    return lax.dot_general(a, b, (((1,), (1,)), ((), ())), preferred_element_type=F32)


def _dot_tn(a, b):
    return lax.dot_general(a, b, (((0,), (0,)), ((), ())), preferred_element_type=F32)


def _split(x):
    hi = x.astype(BF16)
    lo = (x - hi.astype(F32)).astype(BF16)
    return hi, lo


def _dot_split_lhs(x, m):
    hi, lo = _split(x)
    return _dot(hi, m) + _dot(lo, m)


def _dot_split_rhs(m, x):
    hi, lo = _split(x)
    return _dot(m, hi) + _dot(m, lo)


def _rms(x, g):
    ms = jnp.mean(x * x, axis=-1, keepdims=True)
    return x * lax.rsqrt(ms + NORM_EPS) * g


def _seg_rms(x, g, seg, width):
    ss = _dot_split_lhs(x * x, seg)
    return x * lax.rsqrt(ss * (1.0 / width) + NORM_EPS) * g


def _softplus(z):
    return jnp.maximum(z, 0.0) + jnp.log1p(jnp.exp(-jnp.abs(z)))


def _sigmoid(z):
    return 1.0 / (1.0 + jnp.exp(-z))


def _swap_halves(x, seg):
    n = x.shape[-1]
    half = seg // 2
    lane = lax.broadcasted_iota(jnp.int32, x.shape, x.ndim - 1)
    first = (lane & (seg - 1)) < half
    return jnp.where(first, pltpu.roll(x, n - half, x.ndim - 1), pltpu.roll(x, half, x.ndim - 1))


def _div_pow2(x, d):
    return lax.shift_right_logical(x, int(math.log2(d)))


def _seg_matrix(n, width):
    i = np.arange(n) // width
    return jnp.asarray(i[:, None] == i[None, :], dtype=BF16)


def _full(shape):
    nd = len(shape)
    return pl.BlockSpec(shape, lambda *_: (0,) * nd)


def _params(sem):
    return pltpu.CompilerParams(dimension_semantics=sem, vmem_limit_bytes=VMEM_LIMIT)


def _inproj_kernel(x_ref, gmix_ref, wsb_ref, wgl_ref, wag_ref, wml_ref, wgate_ref, bgate_ref,
                   gcq_ref, wuq_ref, gqn_ref, gqr_ref, gkr_ref, gckv_ref, seg64_ref, seg32_ref,
                   cos_ref, sin_ref,
                   qa_ref, ka_ref, va_ref, qg_ref, kg_ref, vg_ref, la_ref, rg_ref,
                   qn_ref, qr_ref, lat_ref, kr_ref, *, time_minor):
    tm = x_ref.shape[0]
    h = _rms(x_ref[...], gmix_ref[...]).astype(BF16)

    sb = _dot(h, wsb_ref[...])
    qa_ref[...] = (sb[:, :SB_W] * SB_SCALE).astype(BF16)
    ka = sb[:, SB_W:2 * SB_W]
    va = sb[:, 2 * SB_W:]
    if time_minor:
        ka_ref[...] = ka.T.reshape(SB_HEADS, SB_HD, tm)
        va_ref[...] = va.T.reshape(SB_HEADS, SB_HD, tm)
    else:
        ka_ref[...] = ka
        va_ref[...] = va

    gl = _dot(h, wgl_ref[...])
    qg_ref[...] = gl[:, :GLA_QK_W]
    kg_ref[...] = gl[:, GLA_QK_W:2 * GLA_QK_W]
    vg_ref[...] = gl[:, 2 * GLA_QK_W:2 * GLA_QK_W + GLA_W]
    rg_ref[...] = gl[:, 2 * GLA_QK_W + GLA_W:]
    ag = _dot(h, wag_ref[...])
    gate = _dot(ag.astype(BF16), wgate_ref[...]) + bgate_ref[...]
    la_ref[...] = -_softplus(-gate) * (1.0 / GLA_TAU)

    ml = _dot(h, wml_ref[...])
    cq = ml[:, :MLA_Q_LORA]
    ckv = ml[:, MLA_Q_LORA:MLA_Q_LORA + MLA_KV_LORA]
    krp = ml[:, MLA_Q_LORA + MLA_KV_LORA:]
    lat_ref[...] = _rms(ckv, gckv_ref[...])
    cos = cos_ref[...]
    sin = sin_ref[...]
    kr_ms = jnp.sum(krp * krp, axis=-1, keepdims=True) * (1.0 / MLA_ROPE)
    krn = krp * lax.rsqrt(kr_ms + NORM_EPS) * gkr_ref[...]
    kr_rot = krn * cos[:, :LANES] + _swap_halves(krn, MLA_ROPE) * sin[:, :LANES]
    if time_minor:
        kr_ref[...] = kr_rot.T[:MLA_ROPE, :]
    else:
        kr_ref[...] = kr_rot[:, :MLA_ROPE]

    q = _dot(_rms(cq, gcq_ref[...]).astype(BF16), wuq_ref[...])
    nope_w = MLA_HEADS * MLA_NOPE
    qn = _seg_rms(q[:, :nope_w], gqn_ref[...], seg64_ref[...], MLA_NOPE)
    qn_ref[...] = (qn * MLA_SCALE).astype(BF16)
    qr = _seg_rms(q[:, nope_w:], gqr_ref[...], seg32_ref[...], MLA_ROPE)
    qr = qr * cos + _swap_halves(qr, MLA_ROPE) * sin
    qr_ref[...] = (qr * MLA_SCALE).astype(BF16)


def _rope_tables(pos):
    half = MLA_ROPE // 2
    freqs = ROPE_THETA ** (-jnp.arange(half, dtype=F32) / half)
    ang = pos.astype(F32)[:, None] * freqs[None, :]
    cos = jnp.cos(ang)
    sin = jnp.sin(ang)
    cos_t = jnp.tile(jnp.concatenate([cos, cos], axis=-1), (1, MLA_HEADS))
    sin_t = jnp.tile(jnp.concatenate([-sin, sin], axis=-1), (1, MLA_HEADS))
    return cos_t, sin_t


def _inproj(x2d, B, T, past, wl, tm, time_minor):
    n = x2d.shape[0]
    tm = min(tm, n)
    cos_t, sin_t = _rope_tables(past + jnp.arange(T, dtype=jnp.int32))
    if T < tm:
        cos_t = jnp.tile(cos_t, (tm // T, 1))
        sin_t = jnp.tile(sin_t, (tm // T, 1))
    n_tab = cos_t.shape[0] // tm
    row = lambda w: pl.BlockSpec((tm, w), lambda i: (i, 0))
    tab = pl.BlockSpec((tm, 2 * LANES), lambda i: (i % n_tab, 0))
    weights = (wl['g_mix'], wl['w_sb'], wl['w_gl'], wl['w_ag'], wl['w_ml'], wl['w_gate'], wl['b_gate'],
               wl['g_cq'], wl['w_uq'], wl['g_qn'], wl['g_qr'], wl['g_kr'], wl['g_ckv'],
               wl['seg64_512'], wl['seg32_256'])
    rowout = lambda w, dt: (row(w), jax.ShapeDtypeStruct((n, w), dt))
    if time_minor:
        assert T % tm == 0
        tpb = T // tm
        kv_out = (pl.BlockSpec((None, SB_HEADS, SB_HD, tm), lambda i: (i // tpb, 0, 0, i % tpb)),
                  jax.ShapeDtypeStruct((B, SB_HEADS, SB_HD, T), F32))
        kr_out = (pl.BlockSpec((None, MLA_ROPE, tm), lambda i: (i // tpb, 0, i % tpb)),
                  jax.ShapeDtypeStruct((B, MLA_ROPE, T), F32))
    else:
        kv_out = rowout(SB_W, F32)
        kr_out = rowout(MLA_ROPE, F32)
    outs = [rowout(SB_W, BF16), kv_out, kv_out,
            rowout(GLA_QK_W, F32), rowout(GLA_QK_W, F32), rowout(GLA_W, F32), rowout(GLA_QK_W, F32),
            rowout(GLA_W, F32), rowout(MLA_HEADS * MLA_NOPE, BF16), rowout(MLA_HEADS * MLA_ROPE, BF16),
            rowout(MLA_KV_LORA, F32), kr_out]
    return pl.pallas_call(
        functools.partial(_inproj_kernel, time_minor=time_minor),
        grid=(n // tm,),
        in_specs=[row(D_MODEL)] + [_full(w.shape) for w in weights] + [tab, tab],
        out_specs=[o[0] for o in outs],
        out_shape=[o[1] for o in outs],
        compiler_params=_params(("parallel",)),
        name="inproj",
    )(x2d, *weights, cos_t, sin_t)


def _sb_kernel(*refs, tq, tk, has_past, n_past_blocks):
    if has_past:
        q_ref, kn_ref, vn_ref, kt_ref, vt_ref, g_ref, seg_ref, o_ref = refs
    else:
        q_ref, kt_ref, vt_ref, g_ref, seg_ref, o_ref = refs
    qi = pl.program_id(1)
    q = q_ref[...]
    lane = lax.broadcasted_iota(jnp.int32, (1, SB_W), 1)
    hmask = [(lane >= SB_HD * h) & (lane < SB_HD * (h + 1)) for h in range(SB_HEADS)]
    qh = [jnp.where(hmask[h], q, jnp.zeros_like(q)) for h in range(SB_HEADS)]

    def suffix_matrix(n):
        return (lax.broadcasted_iota(jnp.int32, (n, n), 0)
                > lax.broadcasted_iota(jnp.int32, (n, n), 1)).astype(BF16)

    def time_minor_block(ref, start, n):
        return ref[:, :, pl.ds(start, n)].reshape(SB_W, n).astype(BF16)

    def block(kblk, vblk, time_minor, vis, umat, cs, acc):
        new_cs = []
        for h in range(SB_HEADS):
            z = _dot(qh[h], kblk) if time_minor else _dot_nt(qh[h], kblk)
            lk = -_softplus(z)
            if vis is not None:
                lk = jnp.where(vis, lk, 0.0)
            later = _dot_split_lhs(lk, umat)
            w = jnp.exp(z + lk + later + cs[h])
            if vis is not None:
                w = jnp.where(vis, w, 0.0)
            w = w.astype(BF16)
            pv = _dot_nt(w, vblk) if time_minor else _dot(w, vblk)
            acc = acc + jnp.where(hmask[h], pv, 0.0)
            new_cs.append(cs[h] + later[:, :1] + lk[:, :1])
        return new_cs, acc

    def dead(cs):
        m = cs[0]
        for c in cs[1:]:
            m = jnp.maximum(m, c)
        return jnp.max(m)

    vis = (lax.broadcasted_iota(jnp.int32, (tq, tq), 1)
           < lax.broadcasted_iota(jnp.int32, (tq, tq), 0))
    zero_c = jnp.zeros((tq, 1), F32)
    init = ([zero_c] * SB_HEADS, jnp.zeros((tq, SB_W), F32))
    if has_past:
        cs, acc = block(kn_ref[...].astype(BF16), vn_ref[...].astype(BF16), False, vis,
                        suffix_matrix(tq), *init)
        kb0 = jnp.int32(n_past_blocks - 1)
    else:
        q0 = pl.multiple_of(qi * tq, tq)
        cs, acc = block(time_minor_block(kt_ref, q0, tq), time_minor_block(vt_ref, q0, tq), True, vis,
                        suffix_matrix(tq), *init)
        kb0 = qi - 1

    umat = suffix_matrix(tk)

    def cond(st):
        return jnp.logical_and(st[0] >= 0, st[1] > SB_DEAD_LOG)

    def body(st):
        kb = st[0]
        cs, acc = list(st[2:2 + SB_HEADS]), st[2 + SB_HEADS]
        start = pl.multiple_of(kb * tk, tk)
        cs, acc = block(time_minor_block(kt_ref, start, tk), time_minor_block(vt_ref, start, tk), True,
                        None, umat, cs, acc)
        return (kb - 1, dead(cs), *cs, acc)

    st = lax.while_loop(cond, body, (kb0, dead(cs), *cs, acc))
    acc = st[2 + SB_HEADS]
    o_ref[...] = _seg_rms(acc, g_ref[...], seg_ref[...], SB_HD).astype(BF16)


def _sb_attn(q, k_new, v_new, kt_cache, vt_cache, layer, wl, tq, tk):
    B, T, _ = q.shape
    tq = min(tq, T)
    has_past = kt_cache is not None
    n_past_blocks = 0
    in_specs = [pl.BlockSpec((None, tq, SB_W), lambda b, i: (b, i, 0))]
    if has_past:
        P = kt_cache.shape[-1]
        assert tq == T and P % tk == 0
        n_past_blocks = P // tk
        new = pl.BlockSpec((None, T, SB_W), lambda b, i: (b, 0, 0))
        past = pl.BlockSpec((None, None, SB_HEADS, SB_HD, P), lambda b, i: (layer, b, 0, 0, 0))
        in_specs += [new, new, past, past]
        args = [q, k_new, v_new, kt_cache, vt_cache]
    else:
        assert tq == tk
        new = pl.BlockSpec((None, SB_HEADS, SB_HD, T), lambda b, i: (b, 0, 0, 0))
        in_specs += [new, new]
        args = [q, k_new, v_new]
    in_specs += [_full((1, SB_W)), _full((SB_W, SB_W))]
    args += [wl['g_sb_out'], wl['seg64_256']]
    return pl.pallas_call(
        functools.partial(_sb_kernel, tq=tq, tk=tk, has_past=has_past, n_past_blocks=n_past_blocks),
        grid=(B, T // tq),
        in_specs=in_specs,
        out_specs=pl.BlockSpec((None, tq, SB_W), lambda b, i: (b, i, 0)),
        out_shape=jax.ShapeDtypeStruct((B, T, SB_W), BF16),
        compiler_params=_params(("parallel", "parallel")),
        name="sb_attn",
    )(*args)


def _gla_kernel(q_ref, k_ref, v_ref, la_ref, rg_ref, s0_ref, g_ref, seg_ref, o_ref, s_ref, *, tc):
    ci = pl.program_id(1)

    @pl.when(ci == 0)
    def _():
        s_ref[...] = s0_ref[...]

    nsub = tc // GLA_BLOCK
    q, k, la = q_ref[...], k_ref[...], la_ref[...]
    v = v_ref[...]
    r_i = lax.broadcasted_iota(jnp.int32, (tc, tc), 0)
    c_i = lax.broadcasted_iota(jnp.int32, (tc, tc), 1)
    same = _div_pow2(r_i, GLA_BLOCK) == _div_pow2(c_i, GLA_BLOCK)
    causal = jnp.logical_and(same, c_i <= r_i)
    b = _dot_split_rhs(causal.astype(BF16), la)
    bl = _dot_split_rhs(same.astype(BF16), la)
    qt = (q * jnp.exp(b) * GLA_SCALE).astype(BF16)
    kt = (k * jnp.exp(-b)).astype(BF16)
    kd = (k * jnp.exp(bl - b)).astype(BF16)
    vb = v.astype(BF16)
    decay = jnp.exp(bl)

    lane_k = lax.broadcasted_iota(jnp.int32, (1, GLA_QK_W), 1)
    lane_v = lax.broadcasted_iota(jnp.int32, (1, GLA_W), 1)
    intra = jnp.zeros((tc, GLA_W), F32)
    for h in range(GLA_HEADS):
        km = (lane_k >= GLA_DK * h) & (lane_k < GLA_DK * (h + 1))
        vm = (lane_v >= GLA_DV * h) & (lane_v < GLA_DV * (h + 1))
        att = _dot_nt(jnp.where(km, qt, jnp.zeros_like(qt)), kt)
        att = jnp.where(causal, att, 0.0).astype(BF16)
        intra = intra + jnp.where(vm, _dot(att, vb), 0.0)

    sr = _div_pow2(lax.broadcasted_iota(jnp.int32, (GLA_QK_W, GLA_W), 0), GLA_DK)
    sc = _div_pow2(lax.broadcasted_iota(jnp.int32, (GLA_QK_W, GLA_W), 1), GLA_DV)
    bd = sr == sc
    eye = (lax.broadcasted_iota(jnp.int32, (GLA_QK_W, GLA_QK_W), 0)
           == lax.broadcasted_iota(jnp.int32, (GLA_QK_W, GLA_QK_W), 1))
    state = s_ref[...]
    inter = []
    for j in range(nsub):
        rows = slice(j * GLA_BLOCK, (j + 1) * GLA_BLOCK)
        inter.append(_dot(qt[rows], state.astype(BF16)))
        upd = _dot_tn(kd[rows], vb[rows])
        d_row = decay[j * GLA_BLOCK:j * GLA_BLOCK + 1]
        d_col = jnp.sum(jnp.where(eye, d_row, 0.0), axis=1, keepdims=True)
        state = d_col * state + jnp.where(bd, upd, 0.0)
    s_ref[...] = state
    o = intra + jnp.concatenate(inter, axis=0)
    o = _seg_rms(o, g_ref[...], seg_ref[...], GLA_DV)
    rg = rg_ref[...]
    o_ref[...] = (o * (rg * _sigmoid(rg))).astype(BF16)


def _gla(qg, kg, vg, la, rg, s0_bd, wl, tc):
    B, T, _ = qg.shape
    tc = min(tc, T)
    blk = lambda w: pl.BlockSpec((None, tc, w), lambda b, c: (b, c, 0))
    st = pl.BlockSpec((None, GLA_QK_W, GLA_W), lambda b, c: (b, 0, 0))
    return pl.pallas_call(
        functools.partial(_gla_kernel, tc=tc),
        grid=(B, T // tc),
        in_specs=[blk(GLA_QK_W), blk(GLA_QK_W), blk(GLA_W), blk(GLA_QK_W), blk(GLA_W), st,
                  _full((1, GLA_W)), _full((GLA_W, GLA_W))],
        out_specs=[blk(GLA_W), st],
        out_shape=[jax.ShapeDtypeStruct((B, T, GLA_W), BF16),
                   jax.ShapeDtypeStruct((B, GLA_QK_W, GLA_W), F32)],
        compiler_params=_params(("parallel", "arbitrary")),
        name="gla",
    )(qg, kg, vg, la, rg, s0_bd, wl['g_gla_out'], wl['seg64_256'])


def _state_to_blockdiag(s):
    B = s.shape[0]
    eye = jnp.eye(GLA_HEADS, dtype=s.dtype)
    return jnp.einsum('bhkv,hg->bhkgv', s, eye).reshape(B, GLA_QK_W, GLA_W)


def _state_from_blockdiag(s):
    B = s.shape[0]
    s5 = s.reshape(B, GLA_HEADS, GLA_DK, GLA_HEADS, GLA_DV)
    return jnp.stack([s5[:, h, :, h, :] for h in range(GLA_HEADS)], axis=1)


def _mla_kv_kernel(lat_ref, kr_ref, wn_ref, wv_ref, gkn_ref, seg_ref, place_ref, kcat_ref, v_ref):
    lat = lat_ref[...].astype(BF16)
    kn = _seg_rms(_dot(lat, wn_ref[...]), gkn_ref[...], seg_ref[...], MLA_NOPE)
    v = _dot(lat, wv_ref[...])
    krp = _dot_tn(kr_ref[...].astype(BF16), place_ref[...])
    lane = lax.broadcasted_iota(jnp.int32, (1, LANES), 1)
    for p in range(MLA_HEADS // 2):
        pair = kn[:, LANES * p:LANES * (p + 1)]
        kcat_ref[2 * p] = jnp.where(lane < MLA_NOPE, pair, krp).astype(BF16)
        kcat_ref[2 * p + 1] = jnp.where(lane < MLA_NOPE, pltpu.roll(pair, MLA_NOPE, 1), krp).astype(BF16)
        v_ref[p] = v[:, LANES * p:LANES * (p + 1)].astype(BF16)


def _mla_kv(lat, kr_t, wl, tr):
    B, L, _ = lat.shape
    tr = min(tr, L)
    return pl.pallas_call(
        _mla_kv_kernel,
        grid=(B, L // tr),
        in_specs=[pl.BlockSpec((None, tr, MLA_KV_LORA), lambda b, i: (b, i, 0)),
                  pl.BlockSpec((None, MLA_ROPE, tr), lambda b, i: (b, 0, i)),
                  _full(wl['w_ukn'].shape), _full(wl['w_ukv_v'].shape), _full((1, MLA_HEADS * MLA_NOPE)),
                  _full(wl['seg64_512'].shape), _full((MLA_ROPE, LANES))],
        out_specs=[pl.BlockSpec((None, MLA_HEADS, tr, LANES), lambda b, i: (b, 0, i, 0)),
                   pl.BlockSpec((None, MLA_HEADS // 2, tr, LANES), lambda b, i: (b, 0, i, 0))],
        out_shape=[jax.ShapeDtypeStruct((B, MLA_HEADS, L, LANES), BF16),
                   jax.ShapeDtypeStruct((B, MLA_HEADS // 2, L, LANES), BF16)],
        compiler_params=_params(("parallel", "parallel")),
        name="mla_kv",
    )(lat, kr_t, wl['w_ukn'], wl['w_ukv_v'], wl['g_kn'], wl['seg64_512'], wl['rope_place'])


def _mla_attn_kernel(q_ref, k_ref, v_ref, g_ref, seg_ref, o_ref, *, tq):
    qi = pl.program_id(1)
    q0 = pl.multiple_of(qi * tq, tq)
    vis = (_div_pow2(lax.broadcasted_iota(jnp.int32, (tq, tq), 1), CHUNK)
           <= _div_pow2(lax.broadcasted_iota(jnp.int32, (tq, tq), 0), CHUNK))
    qs = [q_ref[h] for h in range(MLA_HEADS)]

    state = []
    for h in range(MLA_HEADS):
        s = jnp.where(vis, _dot_nt(qs[h], k_ref[h, pl.ds(q0, tq), :]), MASK_NEG)
        m = jnp.max(s, axis=-1, keepdims=True)
        p = jnp.exp(s - m)
        state += [m, jnp.sum(p, axis=-1, keepdims=True),
                  _dot(p.astype(BF16), v_ref[h // 2, pl.ds(q0, tq), :])]

    def body(kb, st):
        start = pl.multiple_of(kb * tq, tq)
        out = []
        for h in range(MLA_HEADS):
            m, l, acc = st[3 * h:3 * h + 3]
            s = _dot_nt(qs[h], k_ref[h, pl.ds(start, tq), :])
            m_new = jnp.maximum(m, jnp.max(s, axis=-1, keepdims=True))
            a = jnp.exp(m - m_new)
            p = jnp.exp(s - m_new)
            out += [m_new, a * l + jnp.sum(p, axis=-1, keepdims=True),
                    a * acc + _dot(p.astype(BF16), v_ref[h // 2, pl.ds(start, tq), :])]
        return tuple(out)

    st = lax.fori_loop(0, qi, body, tuple(state))
    lane = lax.broadcasted_iota(jnp.int32, (1, LANES), 1)
    pairs = []
    for p in range(MLA_HEADS // 2):
        even = st[6 * p + 2] / st[6 * p + 1]
        odd = st[6 * p + 5] / st[6 * p + 4]
        pairs.append(jnp.where(lane < MLA_V, even, odd))
    o = jnp.concatenate(pairs, axis=1)
    o_ref[...] = _seg_rms(o, g_ref[...], seg_ref[...], MLA_V).astype(BF16)


def _mla_attn(qcat, kcat, v, wl, tq):
    B, _, T, _ = qcat.shape
    tq = min(tq, T)
    return pl.pallas_call(
        functools.partial(_mla_attn_kernel, tq=tq),
        grid=(B, T // tq),
        in_specs=[pl.BlockSpec((None, MLA_HEADS, tq, LANES), lambda b, i: (b, 0, i, 0)),
                  pl.BlockSpec((None, MLA_HEADS, T, LANES), lambda b, i: (b, 0, 0, 0)),
                  pl.BlockSpec((None, MLA_HEADS // 2, T, LANES), lambda b, i: (b, 0, 0, 0)),
                  _full((1, MLA_W)), _full((MLA_W, MLA_W))],
        out_specs=pl.BlockSpec((None, tq, MLA_W), lambda b, i: (b, i, 0)),
        out_shape=jax.ShapeDtypeStruct((B, T, MLA_W), BF16),
        compiler_params=_params(("parallel", "parallel")),
        name="mla_attn",
    )(qcat, kcat, v, wl['g_mla_out_512'], wl['seg64_512'])


def _mla_dec_kernel(qn_ref, qr_ref, lat_ref, krt_ref, latn_ref, krn_ref, wknt_ref, wv_ref, gkn_ref,
                    g_ref, seg_ref, o_ref, s_ref, *, kc):
    T = qn_ref.shape[0]
    P = lat_ref.shape[0]
    H = MLA_HEADS
    wknt = wknt_ref[...]
    qn = (qn_ref[...].astype(F32) * gkn_ref[...]).astype(BF16)
    qabs = jnp.concatenate(
        [_dot(qn[:, MLA_NOPE * h:MLA_NOPE * (h + 1)], wknt[MLA_NOPE * h:MLA_NOPE * (h + 1), :])
         for h in range(H)], axis=0).astype(BF16)
    qr = qr_ref[...]
    qrs = jnp.concatenate([qr[:, MLA_ROPE * h:MLA_ROPE * (h + 1)] for h in range(H)], axis=0)

    def scores(lat_b, sr, n):
        kn_t = _dot_nt(wknt, lat_b)
        ss = jnp.sum((kn_t * kn_t).reshape(H, MLA_NOPE, n), axis=1)
        r = lax.rsqrt(ss * (1.0 / MLA_NOPE) + NORM_EPS)
        sn = _dot_nt(qabs, lat_b).reshape(H, T, n)
        return (sn * r[:, None, :]).reshape(H * T, n) + sr

    m = None
    for c in range(P // kc):
        cols = slice(c * kc, (c + 1) * kc)
        s = scores(lat_ref[cols, :].astype(BF16), _dot(qrs, krt_ref[:, cols].astype(BF16)), kc)
        s_ref[:, cols] = s
        mc = jnp.max(s, axis=-1, keepdims=True)
        m = mc if m is None else jnp.maximum(m, mc)
    latn = latn_ref[...].astype(BF16)
    s_new = scores(latn, _dot_nt(qrs, krn_ref[...].astype(BF16)), T)
    m = jnp.maximum(m, jnp.max(s_new, axis=-1, keepdims=True))

    p = jnp.exp(s_new - m)
    l = jnp.sum(p, axis=-1, keepdims=True)
    ctx = _dot(p.astype(BF16), latn)
    for c in range(P // kc):
        cols = slice(c * kc, (c + 1) * kc)
        p = jnp.exp(s_ref[:, cols] - m)
        l = l + jnp.sum(p, axis=-1, keepdims=True)
        ctx = ctx + _dot(p.astype(BF16), lat_ref[cols, :].astype(BF16))
    ctx = (ctx / l).astype(BF16)

    wv = wv_ref[...]
    lane = lax.broadcasted_iota(jnp.int32, (1, MLA_W), 1)
    o = jnp.zeros((T, MLA_W), F32)
    for h in range(H):
        oh = _dot(ctx[T * h:T * (h + 1)], wv)
        o = jnp.where((lane >= MLA_V * h) & (lane < MLA_V * (h + 1)), oh, o)
    o_ref[...] = _seg_rms(o, g_ref[...], seg_ref[...], MLA_V).astype(BF16)


def _mla_dec(qn, qr, lat_cache, kr_t_cache, layer, lat_new, kr_new, T, wl, kc):
    n = qn.shape[0]
    B = n // T
    P = lat_cache.shape[2]
    row = lambda w: pl.BlockSpec((T, w), lambda b: (b, 0))
    return pl.pallas_call(
        functools.partial(_mla_dec_kernel, kc=kc),
        grid=(B,),
        in_specs=[row(MLA_HEADS * MLA_NOPE), row(MLA_HEADS * MLA_ROPE),
                  pl.BlockSpec((None, None, P, MLA_KV_LORA), lambda b: (layer, b, 0, 0)),
                  pl.BlockSpec((None, None, MLA_ROPE, P), lambda b: (layer, b, 0, 0)),
                  row(MLA_KV_LORA), row(MLA_ROPE),
                  _full(wl['w_ukn_t'].shape), _full(wl['w_ukv_v'].shape), _full((1, MLA_HEADS * MLA_NOPE)),
                  _full((1, MLA_W)), _full((MLA_W, MLA_W))],
        out_specs=row(MLA_W),
        out_shape=jax.ShapeDtypeStruct((n, MLA_W), BF16),
        scratch_shapes=[pltpu.VMEM((MLA_HEADS * T, P), F32)],
        compiler_params=_params(("parallel",)),
        name="mla_dec",
    )(qn, qr, lat_cache, kr_t_cache, lat_new, kr_new, wl['w_ukn_t'], wl['w_ukv_v'], wl['g_kn'],
      wl['g_mla_out_512'], wl['seg64_512'])


def _post_kernel(x_ref, oa_ref, ob_ref, oc_ref, mk_ref, mv_ref, wo_ref, gcross_ref, wcq_ref, gcqn_ref,
                 wco_ref, o_ref, att_ref, *, nb, tt):
    wo = wo_ref
    mix = (_dot(oa_ref[...], wo[:SB_W, :]) + _dot(ob_ref[...], wo[SB_W:SB_W + GLA_W, :])
           + _dot(oc_ref[...], wo[SB_W + GLA_W:, :]))
    x1 = x_ref[...] + mix
    h = _rms(x1, gcross_ref[...]).astype(BF16)
    q = _dot(h, wcq_ref[...])
    gq = gcqn_ref[...]
    qs = []
    for hd in range(MEM_HEADS):
        qh = _rms(q[:, MEM_HD * hd:MEM_HD * (hd + 1)], gq).astype(BF16)
        qs.append(qh)
    for bi in range(nb):
        rows = slice(bi * tt, (bi + 1) * tt)
        for hd in range(MEM_HEADS):
            cols = slice(MEM_HD * hd, MEM_HD * (hd + 1))
            kh = mk_ref[bi, :, cols].astype(BF16)
            vh = mv_ref[bi, :, cols].astype(BF16)
            s = _dot_nt(qs[hd][rows], kh) * MEM_SCALE
            m = jnp.max(s, axis=-1, keepdims=True)
            p = jnp.exp(s - m)
            l = jnp.sum(p, axis=-1, keepdims=True)
            att_ref[rows, cols] = _dot(p.astype(BF16), vh) / l
    o_ref[...] = x1 + _dot(att_ref[...].astype(BF16), wco_ref[...])


def _post(x2d, oa, ob, oc, mem_k, mem_v, T, wl, tm):
    n = x2d.shape[0]
    tm = min(tm, n)
    tt = min(T, tm)
    nb = tm // tt
    tiles_per_b = T // tt
    row = lambda w: pl.BlockSpec((tm, w), lambda i: (i, 0))
    mem = pl.BlockSpec((nb, MEM_LEN, MEM_W), lambda i: (i // tiles_per_b if nb == 1 else i, 0, 0))
    weights = (wl['w_out'], wl['g_cross'], wl['w_cq'], wl['g_cqn'], wl['w_co'])
    return pl.pallas_call(
        functools.partial(_post_kernel, nb=nb, tt=tt),
        grid=(n // tm,),
        in_specs=[row(D_MODEL), row(SB_W), row(GLA_W), row(MLA_W), mem, mem] + [_full(w.shape) for w in weights],
        out_specs=row(D_MODEL),
        out_shape=jax.ShapeDtypeStruct((n, D_MODEL), F32),
        scratch_shapes=[pltpu.VMEM((tm, MEM_W), F32)],
        compiler_params=_params(("parallel",)),
        name="post",
    )(x2d, oa, ob, oc, mem_k, mem_v, *weights)


def _ffn_kernel(x_ref, g_ref, wg_ref, wu_ref, wd_ref, o_ref, h_ref, acc_ref):
    j = pl.program_id(1)

    @pl.when(j == 0)
    def _():
        x = x_ref[...]
        h_ref[...] = _rms(x, g_ref[...]).astype(BF16)
        acc_ref[...] = x

    h = h_ref[...]
    gate = _dot(h, wg_ref[...])
    up = _dot(h, wu_ref[...])
    act = (gate * _sigmoid(gate) * up).astype(BF16)
    acc_ref[...] += _dot(act, wd_ref[...])

    @pl.when(j == pl.num_programs(1) - 1)
    def _():
        o_ref[...] = acc_ref[...]


def _ffn(x2d, wl, tm, tf):
    n = x2d.shape[0]
    tm = min(tm, n)
    return pl.pallas_call(
        _ffn_kernel,
        grid=(n // tm, D_FF // tf),
        in_specs=[pl.BlockSpec((tm, D_MODEL), lambda i, j: (i, 0)), _full((1, D_MODEL)),
                  pl.BlockSpec((D_MODEL, tf), lambda i, j: (0, j)),
                  pl.BlockSpec((D_MODEL, tf), lambda i, j: (0, j)),
                  pl.BlockSpec((tf, D_MODEL), lambda i, j: (j, 0))],
        out_specs=pl.BlockSpec((tm, D_MODEL), lambda i, j: (i, 0)),
        out_shape=jax.ShapeDtypeStruct((n, D_MODEL), F32),
        scratch_shapes=[pltpu.VMEM((tm, D_MODEL), BF16), pltpu.VMEM((tm, D_MODEL), F32)],
        compiler_params=_params(("parallel", "arbitrary")),
        name="ffn",
    )(x2d, wl['g_ffn'], wl['w_gate_ffn'], wl['w_up'], wl['w_down'])


def _memkv_kernel(m_ref, g_ref, wk_ref, wv_ref, gk_ref, k_ref, v_ref):
    h = _rms(m_ref[...], g_ref[...]).astype(BF16)
    k = _dot(h, wk_ref[...])
    gk = gk_ref[...]
    for hd in range(MEM_HEADS):
        cols = slice(MEM_HD * hd, MEM_HD * (hd + 1))
        k_ref[:, cols] = _rms(k[:, cols], gk)
    v_ref[...] = _dot(h, wv_ref[...])


def _memkv(mem2d, wl, tm):
    n = mem2d.shape[0]
    tm = min(tm, n)
    row = lambda w: pl.BlockSpec((tm, w), lambda i: (i, 0))
    weights = (wl['g_mem'], wl['w_ck'], wl['w_cv'], wl['g_ckn'])
    return pl.pallas_call(
        _memkv_kernel,
        grid=(n // tm,),
        in_specs=[row(D_MODEL)] + [_full(w.shape) for w in weights],
        out_specs=[row(MEM_W), row(MEM_W)],
        out_shape=[jax.ShapeDtypeStruct((n, MEM_W), F32)] * 2,
        compiler_params=_params(("parallel",)),
        name="memkv",
    )(mem2d, *weights)


def _prep_layer(p, l):
    w_in = p['w_in'][l]
    o = np.cumsum([0, SB_W, SB_W, SB_W, GLA_QK_W, GLA_QK_W, GLA_W, GLA_GATE_RANK, GLA_W,
                   MLA_Q_LORA, MLA_KV_LORA, MLA_ROPE])
    seg = lambda i, j: w_in[:, o[i]:o[j]]
    pad_cols = lambda w, n: jnp.pad(w, ((0, 0), (0, n - w.shape[1])))
    row = lambda g, reps=1: jnp.tile(g, reps)[None, :].astype(F32)
    w_uq = p['w_uq'][l].reshape(MLA_Q_LORA, MLA_HEADS, MLA_NOPE + MLA_ROPE)
    w_ukv = p['w_ukv'][l].reshape(MLA_KV_LORA, MLA_HEADS, MLA_NOPE + MLA_V)
    place = np.zeros((MLA_ROPE, LANES), np.float32)
    place[np.arange(MLA_ROPE), MLA_NOPE + np.arange(MLA_ROPE)] = 1.0
    return {
        'g_mix': row(p['g_mix_norm'][l]),
        'w_sb': seg(0, 3).astype(BF16),
        'w_gl': jnp.concatenate([seg(3, 6), seg(7, 8)], axis=1).astype(BF16),
        'w_ag': pad_cols(seg(6, 7), LANES).astype(BF16),
        'w_ml': pad_cols(seg(8, 11), 4 * LANES).astype(BF16),
        'w_gate': jnp.pad(p['w_gla_gate'][l], ((0, LANES - GLA_GATE_RANK), (0, 0))).astype(BF16),
        'b_gate': row(p['b_gla_gate'][l]),
        'g_cq': row(p['g_cq'][l]),
        'w_uq': jnp.concatenate([w_uq[:, :, :MLA_NOPE].reshape(MLA_Q_LORA, -1),
                                 w_uq[:, :, MLA_NOPE:].reshape(MLA_Q_LORA, -1)], axis=1).astype(BF16),
        'g_qn': row(p['g_qn'][l], MLA_HEADS),
        'g_qr': row(p['g_qr'][l], MLA_HEADS),
        'g_kr': jnp.pad(p['g_kr'][l], (0, LANES - MLA_ROPE))[None, :],
        'g_ckv': row(p['g_ckv'][l]),
        'w_ukn': w_ukv[:, :, :MLA_NOPE].reshape(MLA_KV_LORA, -1).astype(BF16),
        'w_ukv_v': w_ukv[:, :, MLA_NOPE:].reshape(MLA_KV_LORA, -1).astype(BF16),
        'g_kn': row(p['g_kn'][l], MLA_HEADS),
        'w_ukn_t': w_ukv[:, :, :MLA_NOPE].reshape(MLA_KV_LORA, -1).T.astype(BF16),
        'g_mla_out_512': row(p['g_mla_out'][l], MLA_HEADS),
        'g_sb_out': row(p['g_sb_out'][l], SB_HEADS),
        'g_gla_out': row(p['g_gla_out'][l], GLA_HEADS),
        'w_out': p['w_out'][l].astype(BF16),
        'g_cross': row(p['g_cross_norm'][l]),
        'g_mem': row(p['g_mem_norm'][l]),
        'w_cq': p['w_cq'][l].astype(BF16),
        'w_ck': p['w_ck'][l].astype(BF16),
        'w_cv': p['w_cv'][l].astype(BF16),
        'g_cqn': row(p['g_cqn'][l]),
        'g_ckn': row(p['g_ckn'][l]),
        'w_co': p['w_co'][l].astype(BF16),
        'g_ffn': row(p['g_ffn_norm'][l]),
        'w_gate_ffn': p['w_gate'][l].astype(BF16),
        'w_up': p['w_up'][l].astype(BF16),
        'w_down': p['w_down'][l].astype(BF16),
        'seg64_512': _seg_matrix(512, 64),
        'seg32_256': _seg_matrix(256, 32),
        'seg64_256': _seg_matrix(256, 64),
        'rope_place': jnp.asarray(place, dtype=BF16),
    }


def _head_major_q(qn, qr, B, T):
    qn = qn.reshape(B, T, MLA_HEADS, MLA_NOPE)
    qr = qr.reshape(B, T, MLA_HEADS, MLA_ROPE)
    pad = jnp.zeros((B, T, MLA_HEADS, LANES - MLA_NOPE - MLA_ROPE), qn.dtype)
    return jnp.concatenate([qn, qr, pad], axis=-1).transpose(0, 2, 1, 3)


def _time_minor_caches(cache_sb_k, cache_sb_v, cache_mla_latent, cache_mla_krope):
    return (jnp.transpose(cache_sb_k, (0, 1, 3, 4, 2)), jnp.transpose(cache_sb_v, (0, 1, 3, 4, 2)),
            cache_mla_latent, jnp.transpose(cache_mla_krope, (0, 1, 3, 2)))


def _trunk_layer(x, caches, layer, gla_s0, mem_k, mem_v, wl):
    B, T, _ = x.shape
    has_past = caches is not None
    past = caches[0].shape[-1] if has_past else 0
    x2d = x.reshape(B * T, D_MODEL)
    (qa, ka, va, qg, kg, vg, la, rg, qn, qr, lat, kr) = _inproj(x2d, B, T, past, wl, tm=512,
                                                                time_minor=not has_past)
    r3 = lambda a: a.reshape(B, T, a.shape[-1])

    if has_past:
        sb_kt, sb_vt, lat_cache, kr_t_cache = caches
        o_a = _sb_attn(r3(qa), r3(ka), r3(va), sb_kt, sb_vt, layer, wl, tq=128, tk=128)
        ka_out = ka.reshape(B, T, SB_HEADS, SB_HD)
        va_out = va.reshape(B, T, SB_HEADS, SB_HD)
        kr_out = r3(kr)
    else:
        o_a = _sb_attn(r3(qa), ka, va, None, None, None, wl, tq=128, tk=128)
        ka_out = jnp.transpose(ka, (0, 3, 1, 2))
        va_out = jnp.transpose(va, (0, 3, 1, 2))
        kr_out = jnp.transpose(kr, (0, 2, 1))

    if gla_s0 is None:
        s0 = jnp.zeros((B, GLA_QK_W, GLA_W), F32)
    else:
        s0 = _state_to_blockdiag(gla_s0)
    o_b, s_bd = _gla(r3(qg), r3(kg), r3(vg), r3(la), r3(rg), s0, wl, tc=128)
    gla_s = _state_from_blockdiag(s_bd)

    if has_past:
        o_c = _mla_dec(qn, qr, lat_cache, kr_t_cache, layer, lat, kr, T, wl, kc=1024)
    else:
        kcat, v_pairs = _mla_kv(r3(lat), kr, wl, tr=512)
        o_c = _mla_attn(_head_major_q(qn, qr, B, T), kcat, v_pairs, wl, tq=256)

    x2d = _post(x2d, o_a.reshape(B * T, SB_W), o_b.reshape(B * T, GLA_W), o_c.reshape(B * T, MLA_W),
                mem_k.reshape(B, MEM_LEN, MEM_W), mem_v.reshape(B, MEM_LEN, MEM_W), T, wl, tm=512)
    x2d = _ffn(x2d, wl, tm=512, tf=D_FF // 2)
    return x2d.reshape(B, T, D_MODEL), ka_out, va_out, gla_s, r3(lat), kr_out


def kernel(x_prompt, x_sample, mem_prompt, cache_sb_k, cache_sb_v, state_gla, cache_mla_latent,
           cache_mla_krope, cache_mem_k, cache_mem_v, g_mix_norm, w_in, w_gla_gate, b_gla_gate,
           g_gla_out, g_sb_out, g_cq, w_uq, g_qn, g_qr, g_kr, g_ckv, w_ukv, g_kn, g_mla_out, w_out,
           g_cross_norm, g_mem_norm, w_cq, w_ck, w_cv, g_cqn, g_ckn, w_co, g_ffn_norm, w_gate,
           w_up, w_down):
    p = dict(g_mix_norm=g_mix_norm, w_in=w_in, w_gla_gate=w_gla_gate, b_gla_gate=b_gla_gate,
             g_gla_out=g_gla_out, g_sb_out=g_sb_out, g_cq=g_cq, w_uq=w_uq, g_qn=g_qn, g_qr=g_qr,
             g_kr=g_kr, g_ckv=g_ckv, w_ukv=w_ukv, g_kn=g_kn, g_mla_out=g_mla_out, w_out=w_out,
             g_cross_norm=g_cross_norm, g_mem_norm=g_mem_norm, w_cq=w_cq, w_ck=w_ck, w_cv=w_cv,
             g_cqn=g_cqn, g_ckn=g_ckn, w_co=w_co, g_ffn_norm=g_ffn_norm, w_gate=w_gate, w_up=w_up,
             w_down=w_down)
    depth = w_in.shape[0]
    B = x_prompt.shape[0]
    xp, xs = x_prompt, x_sample
    outs_p = [[] for _ in range(7)]
    outs_s = [[] for _ in range(5)]
    caches = _time_minor_caches(cache_sb_k, cache_sb_v, cache_mla_latent, cache_mla_krope)
    for l in range(depth):
        wl = _prep_layer(p, l)
        mk, mv = _memkv(mem_prompt.reshape(B * MEM_LEN, D_MODEL), wl, tm=512)
        mk = mk.reshape(B, MEM_LEN, MEM_HEADS, MEM_HD)
        mv = mv.reshape(B, MEM_LEN, MEM_HEADS, MEM_HD)
        xp, ka, va, st, lat, kr = _trunk_layer(xp, None, l, None, mk, mv, wl)
        for lst, val in zip(outs_p, (ka, va, st, lat, kr, mk, mv)):
            lst.append(val)
        xs, ka, va, st, lat, kr = _trunk_layer(xs, caches, l, state_gla[l], cache_mem_k[l], cache_mem_v[l], wl)
        for lst, val in zip(outs_s, (ka, va, st, lat, kr)):
            lst.append(val)
    return (xp, xs, *[jnp.stack(v) for v in outs_p], *[jnp.stack(v) for v in outs_s])
```

```python
import functools
import math

import jax
import jax.numpy as jnp
import numpy as np
from jax import lax
from jax.experimental import pallas as pl
from jax.experimental.pallas import tpu as pltpu

F32 = jnp.float32
BF16 = jnp.bfloat16

D_MODEL = 1024
CHUNK = 64
SB_HEADS, SB_HD = 4, 64
SB_W = SB_HEADS * SB_HD
GLA_HEADS, GLA_DK, GLA_DV = 4, 32, 64
GLA_QK_W = GLA_HEADS * GLA_DK
GLA_W = GLA_HEADS * GLA_DV
GLA_GATE_RANK = 16
GLA_TAU = 16.0
GLA_BLOCK = 16
MLA_HEADS = 8
MLA_Q_LORA, MLA_KV_LORA = 256, 128
MLA_NOPE, MLA_ROPE, MLA_V = 64, 32, 64
MLA_W = MLA_HEADS * MLA_V
ROPE_THETA = 10000.0
MEM_LEN, MEM_HEADS, MEM_HD = 256, 4, 128
MEM_W = MEM_HEADS * MEM_HD
D_FF = 2816
NORM_EPS = 1e-6

LANES = 128
VMEM_LIMIT = 56 * 1024 * 1024
SB_SCALE = 1.0 / math.sqrt(SB_HD)
GLA_SCALE = GLA_DK ** -0.5
MLA_SCALE = (MLA_NOPE + MLA_ROPE) ** -0.5
MEM_SCALE = MEM_HD ** -0.5
SB_DEAD_LOG = -105.0
MASK_NEG = -1e30


def _dot(a, b):
    return jnp.dot(a, b, preferred_element_type=F32)


def _dot_nt(a, b):
    return lax.dot_general(a, b, (((1,), (1,)), ((), ())), preferred_element_type=F32)


def _dot_tn(a, b):
    return lax.dot_general(a, b, (((0,), (0,)), ((), ())), preferred_element_type=F32)


def _split(x):
    hi = x.astype(BF16)
    lo = (x - hi.astype(F32)).astype(BF16)
    return hi, lo


def _dot_split_lhs(x, m):
    hi, lo = _split(x)
    return _dot(hi, m) + _dot(lo, m)


def _dot_split_rhs(m, x):
    hi, lo = _split(x)
    return _dot(m, hi) + _dot(m, lo)


def _rms(x, g):
    ms = jnp.mean(x * x, axis=-1, keepdims=True)
    return x * lax.rsqrt(ms + NORM_EPS) * g


def _seg_rms(x, g, seg, width):
    ss = _dot_split_lhs(x * x, seg)
    return x * lax.rsqrt(ss * (1.0 / width) + NORM_EPS) * g


def _softplus(z):
    return jnp.maximum(z, 0.0) + jnp.log1p(jnp.exp(-jnp.abs(z)))


def _sigmoid(z):
    return 1.0 / (1.0 + jnp.exp(-z))


def _swap_halves(x, seg):
    n = x.shape[-1]
    half = seg // 2
    lane = lax.broadcasted_iota(jnp.int32, x.shape, x.ndim - 1)
    first = (lane & (seg - 1)) < half
    return jnp.where(first, pltpu.roll(x, n - half, x.ndim - 1), pltpu.roll(x, half, x.ndim - 1))


def _div_pow2(x, d):
    return lax.shift_right_logical(x, int(math.log2(d)))


def _seg_matrix(n, width):
    i = np.arange(n) // width
    return jnp.asarray(i[:, None] == i[None, :], dtype=BF16)


def _full(shape):
    nd = len(shape)
    return pl.BlockSpec(shape, lambda *_: (0,) * nd)


def _params(sem):
    return pltpu.CompilerParams(dimension_semantics=sem, vmem_limit_bytes=VMEM_LIMIT)


def _inproj_kernel(x_ref, gmix_ref, wsb_ref, wgl_ref, wag_ref, wml_ref, wgate_ref, bgate_ref,
                   gcq_ref, wuq_ref, gqn_ref, gqr_ref, gkr_ref, gckv_ref, seg64_ref, seg32_ref,
                   cos_ref, sin_ref,
                   qa_ref, ka_ref, va_ref, qg_ref, kg_ref, vg_ref, la_ref, rg_ref, *mla_refs, time_minor):
    if time_minor:
        qcat_ref, lat_ref, kr_ref = mla_refs
    else:
        qn_ref, qr_ref, lat_ref, kr_ref = mla_refs
    tm = x_ref.shape[0]
    h = _rms(x_ref[...], gmix_ref[...]).astype(BF16)

    sb = _dot(h, wsb_ref[...])
    qa_ref[...] = (sb[:, :SB_W] * SB_SCALE).astype(BF16)
    ka = sb[:, SB_W:2 * SB_W]
    va = sb[:, 2 * SB_W:]
    if time_minor:
        ka_ref[...] = ka.T.reshape(SB_HEADS, SB_HD, tm)
        va_ref[...] = va.T.reshape(SB_HEADS, SB_HD, tm)
    else:
        ka_ref[...] = ka
        va_ref[...] = va

    gl = _dot(h, wgl_ref[...])
    qg_ref[...] = gl[:, :GLA_QK_W]
    kg_ref[...] = gl[:, GLA_QK_W:2 * GLA_QK_W]
    vg_ref[...] = gl[:, 2 * GLA_QK_W:2 * GLA_QK_W + GLA_W]
    rg_ref[...] = gl[:, 2 * GLA_QK_W + GLA_W:]
    ag = _dot(h, wag_ref[...])
    gate = _dot(ag.astype(BF16), wgate_ref[...]) + bgate_ref[...]
    la_ref[...] = -_softplus(-gate) * (1.0 / GLA_TAU)

    ml = _dot(h, wml_ref[...])
    cq = ml[:, :MLA_Q_LORA]
    ckv = ml[:, MLA_Q_LORA:MLA_Q_LORA + MLA_KV_LORA]
    krp = ml[:, MLA_Q_LORA + MLA_KV_LORA:]
    lat_ref[...] = _rms(ckv, gckv_ref[...])
    cos = cos_ref[...]
    sin = sin_ref[...]
    kr_ms = jnp.sum(krp * krp, axis=-1, keepdims=True) * (1.0 / MLA_ROPE)
    krn = krp * lax.rsqrt(kr_ms + NORM_EPS) * gkr_ref[...]
    kr_rot = krn * cos[:, :LANES] + _swap_halves(krn, MLA_ROPE) * sin[:, :LANES]
    if time_minor:
        kr_ref[...] = kr_rot.T[:MLA_ROPE, :]
    else:
        kr_ref[...] = kr_rot[:, :MLA_ROPE]

    q = _dot(_rms(cq, gcq_ref[...]).astype(BF16), wuq_ref[...])
    nope_w = MLA_HEADS * MLA_NOPE
    qn = _seg_rms(q[:, :nope_w], gqn_ref[...], seg64_ref[...], MLA_NOPE) * MLA_SCALE
    qr = _seg_rms(q[:, nope_w:], gqr_ref[...], seg32_ref[...], MLA_ROPE)
    qr = (qr * cos + _swap_halves(qr, MLA_ROPE) * sin) * MLA_SCALE
    if not time_minor:
        qn_ref[...] = qn.astype(BF16)
        qr_ref[...] = qr.astype(BF16)
        return
    lane = lax.broadcasted_iota(jnp.int32, (1, LANES), 1)
    rope_heads_per_tile = LANES // MLA_ROPE
    for hd in range(MLA_HEADS):
        pair = qn[:, LANES * (hd // 2):LANES * (hd // 2 + 1)]
        nope = pair if hd % 2 == 0 else pltpu.roll(pair, MLA_NOPE, 1)
        tile = qr[:, LANES * (hd // rope_heads_per_tile):LANES * (hd // rope_heads_per_tile + 1)]
        shift = (MLA_NOPE - MLA_ROPE * (hd % rope_heads_per_tile)) % LANES
        rope = tile if shift == 0 else pltpu.roll(tile, shift, 1)
        qcat_ref[hd] = jnp.where(lane < MLA_NOPE, nope,
                                 jnp.where(lane < MLA_NOPE + MLA_ROPE, rope, 0.0)).astype(BF16)


def _rope_tables(pos):
    half = MLA_ROPE // 2
    freqs = ROPE_THETA ** (-jnp.arange(half, dtype=F32) / half)
    ang = pos.astype(F32)[:, None] * freqs[None, :]
    cos = jnp.cos(ang)
    sin = jnp.sin(ang)
    cos_t = jnp.tile(jnp.concatenate([cos, cos], axis=-1), (1, MLA_HEADS))
    sin_t = jnp.tile(jnp.concatenate([-sin, sin], axis=-1), (1, MLA_HEADS))
    return cos_t, sin_t


def _inproj(x2d, B, T, past, wl, tm, time_minor):
    n = x2d.shape[0]
    tm = min(tm, n)
    cos_t, sin_t = _rope_tables(past + jnp.arange(T, dtype=jnp.int32))
    if T < tm:
        cos_t = jnp.tile(cos_t, (tm // T, 1))
        sin_t = jnp.tile(sin_t, (tm // T, 1))
    n_tab = cos_t.shape[0] // tm
    row = lambda w: pl.BlockSpec((tm, w), lambda i: (i, 0))
    tab = pl.BlockSpec((tm, 2 * LANES), lambda i: (i % n_tab, 0))
    weights = (wl['g_mix'], wl['w_sb'], wl['w_gl'], wl['w_ag'], wl['w_ml'], wl['w_gate'], wl['b_gate'],
               wl['g_cq'], wl['w_uq'], wl['g_qn'], wl['g_qr'], wl['g_kr'], wl['g_ckv'],
               wl['seg64_512'], wl['seg32_256'])
    rowout = lambda w, dt: (row(w), jax.ShapeDtypeStruct((n, w), dt))
    if time_minor:
        assert T % tm == 0
        tpb = T // tm
        kv_out = (pl.BlockSpec((None, SB_HEADS, SB_HD, tm), lambda i: (i // tpb, 0, 0, i % tpb)),
                  jax.ShapeDtypeStruct((B, SB_HEADS, SB_HD, T), F32))
        kr_out = (pl.BlockSpec((None, MLA_ROPE, tm), lambda i: (i // tpb, 0, i % tpb)),
                  jax.ShapeDtypeStruct((B, MLA_ROPE, T), F32))
        q_out = [(pl.BlockSpec((None, MLA_HEADS, tm, LANES), lambda i: (i // tpb, 0, i % tpb, 0)),
                  jax.ShapeDtypeStruct((B, MLA_HEADS, T, LANES), BF16))]
    else:
        kv_out = rowout(SB_W, F32)
        kr_out = rowout(MLA_ROPE, F32)
        q_out = [rowout(MLA_HEADS * MLA_NOPE, BF16), rowout(MLA_HEADS * MLA_ROPE, BF16)]
    outs = [rowout(SB_W, BF16), kv_out, kv_out,
            rowout(GLA_QK_W, F32), rowout(GLA_QK_W, F32), rowout(GLA_W, F32), rowout(GLA_QK_W, F32),
            rowout(GLA_W, F32), *q_out, rowout(MLA_KV_LORA, F32), kr_out]
    return pl.pallas_call(
        functools.partial(_inproj_kernel, time_minor=time_minor),
        grid=(n // tm,),
        in_specs=[row(D_MODEL)] + [_full(w.shape) for w in weights] + [tab, tab],
        out_specs=[o[0] for o in outs],
        out_shape=[o[1] for o in outs],
        compiler_params=_params(("parallel",)),
        name="inproj",
    )(x2d, *weights, cos_t, sin_t)


def _sb_kernel(*refs, tq, tk, has_past, n_far_blocks, layer):
    if has_past:
        (q_ref, kn_ref, vn_ref, kt_ref, vt_ref, kt_all_ref, vt_all_ref, g_ref, seg_ref, o_ref,
         kbuf_ref, vbuf_ref) = refs
    else:
        q_ref, kt_ref, vt_ref, g_ref, seg_ref, o_ref = refs
    qi = pl.program_id(1)
    q = q_ref[...]
    lane = lax.broadcasted_iota(jnp.int32, (1, SB_W), 1)
    hmask = [(lane >= SB_HD * h) & (lane < SB_HD * (h + 1)) for h in range(SB_HEADS)]
    qh = [jnp.where(hmask[h], q, jnp.zeros_like(q)) for h in range(SB_HEADS)]

    def suffix_matrix(n):
        return (lax.broadcasted_iota(jnp.int32, (n, n), 0)
                > lax.broadcasted_iota(jnp.int32, (n, n), 1)).astype(BF16)

    def time_minor_block(ref, start, n):
        return ref[:, :, pl.ds(start, n)].reshape(SB_W, n).astype(BF16)

    def block(kblk, vblk, time_minor, vis, umat, cs, acc):
        heads = range(SB_HEADS)
        zs = [_dot(qh[h], kblk) if time_minor else _dot_nt(qh[h], kblk) for h in heads]
        lks = [-_softplus(z) for z in zs]
        if vis is not None:
            lks = [jnp.where(vis, lk, 0.0) for lk in lks]
        splits = [_split(lk) for lk in lks]
        laters = [_dot(hi, umat) + _dot(lo, umat) for hi, lo in splits]
        ws = [jnp.exp(zs[h] + lks[h] + laters[h] + cs[h]) for h in heads]
        if vis is not None:
            ws = [jnp.where(vis, w, 0.0) for w in ws]
        ws = [w.astype(BF16) for w in ws]
        pvs = [_dot_nt(w, vblk) if time_minor else _dot(w, vblk) for w in ws]
        for h in heads:
            acc = acc + jnp.where(hmask[h], pvs[h], 0.0)
        new_cs = [cs[h] + laters[h][:, :1] + lks[h][:, :1] for h in heads]
        return new_cs, acc

    def dead(cs):
        m = cs[0]
        for c in cs[1:]:
            m = jnp.maximum(m, c)
        return jnp.max(m)

    vis = (lax.broadcasted_iota(jnp.int32, (tq, tq), 1)
           < lax.broadcasted_iota(jnp.int32, (tq, tq), 0))
    zero_c = jnp.zeros((tq, 1), F32)
    init = ([zero_c] * SB_HEADS, jnp.zeros((tq, SB_W), F32))
    if has_past:
        cs, acc = block(kn_ref[...].astype(BF16), vn_ref[...].astype(BF16), False, vis,
                        suffix_matrix(tq), *init)
        kb0 = jnp.int32(kt_ref.shape[-1] // tk - 1)
    else:
        q0 = pl.multiple_of(qi * tq, tq)
        cs, acc = block(time_minor_block(kt_ref, q0, tq), time_minor_block(vt_ref, q0, tq), True, vis,
                        suffix_matrix(tq), *init)
        kb0 = qi - 1

    umat = suffix_matrix(tk)

    def walk(fetch, kb0, cs, acc):
        def cond(st):
            return jnp.logical_and(st[0] >= 0, st[1] > SB_DEAD_LOG)

        def body(st):
            kb = st[0]
            cs, acc = list(st[2:2 + SB_HEADS]), st[2 + SB_HEADS]
            kblk, vblk = fetch(pl.multiple_of(kb * tk, tk))
            cs, acc = block(kblk, vblk, True, None, umat, cs, acc)
            return (kb - 1, dead(cs), *cs, acc)

        st = lax.while_loop(cond, body, (kb0, dead(cs), *cs, acc))
        return list(st[2:2 + SB_HEADS]), st[2 + SB_HEADS]

    cs, acc = walk(lambda start: (time_minor_block(kt_ref, start, tk), time_minor_block(vt_ref, start, tk)),
                   kb0, cs, acc)
    if has_past and n_far_blocks:
        b = pl.program_id(0)

        def fetch_far(start):
            pltpu.sync_copy(kt_all_ref.at[layer, b, :, :, pl.ds(start, tk)], kbuf_ref)
            pltpu.sync_copy(vt_all_ref.at[layer, b, :, :, pl.ds(start, tk)], vbuf_ref)
            return (kbuf_ref[...].reshape(SB_W, tk).astype(BF16), vbuf_ref[...].reshape(SB_W, tk).astype(BF16))

        cs, acc = walk(fetch_far, jnp.int32(n_far_blocks - 1), cs, acc)
    o_ref[...] = _seg_rms(acc, g_ref[...], seg_ref[...], SB_HD).astype(BF16)


def _sb_attn(q, k_new, v_new, kt_cache, vt_cache, layer, wl, tq, tk, window):
    B, T, _ = q.shape
    tq = min(tq, T)
    has_past = kt_cache is not None
    n_far_blocks = 0
    scratch = []
    in_specs = [pl.BlockSpec((None, tq, SB_W), lambda b, i: (b, i, 0))]
    if has_past:
        P = kt_cache.shape[-1]
        window = min(window, P)
        assert tq == T and P % window == 0 and window % tk == 0
        n_far_blocks = (P - window) // tk
        new = pl.BlockSpec((None, T, SB_W), lambda b, i: (b, 0, 0))
        near = pl.BlockSpec((None, None, SB_HEADS, SB_HD, window), lambda b, i: (layer, b, 0, 0, P // window - 1))
        hbm = pl.BlockSpec(memory_space=pl.ANY)
        in_specs += [new, new, near, near, hbm, hbm]
        args = [q, k_new, v_new, kt_cache, vt_cache, kt_cache, vt_cache]
        scratch = [pltpu.VMEM((SB_HEADS, SB_HD, tk), F32)] * 2
    else:
        assert tq == tk
        new = pl.BlockSpec((None, SB_HEADS, SB_HD, T), lambda b, i: (b, 0, 0, 0))
        in_specs += [new, new]
        args = [q, k_new, v_new]
    in_specs += [_full((1, SB_W)), _full((SB_W, SB_W))]
    args += [wl['g_sb_out'], wl['seg64_256']]
    return pl.pallas_call(
        functools.partial(_sb_kernel, tq=tq, tk=tk, has_past=has_past, n_far_blocks=n_far_blocks, layer=layer),
        grid=(B, T // tq),
        in_specs=in_specs,
        out_specs=pl.BlockSpec((None, tq, SB_W), lambda b, i: (b, i, 0)),
        out_shape=jax.ShapeDtypeStruct((B, T, SB_W), BF16),
        scratch_shapes=scratch,
        compiler_params=_params(("parallel", "parallel")),
        name="sb_attn",
    )(*args)


def _gla_kernel(q_ref, k_ref, v_ref, la_ref, rg_ref, s0_ref, g_ref, seg_ref, o_ref, s_ref, *, tc, nb):
    ci = pl.program_id(1)

    @pl.when(ci == 0)
    def _():
        s_ref[...] = s0_ref[...]

    nsub = tc // GLA_BLOCK
    r_i = lax.broadcasted_iota(jnp.int32, (tc, tc), 0)
    c_i = lax.broadcasted_iota(jnp.int32, (tc, tc), 1)
    same = _div_pow2(r_i, GLA_BLOCK) == _div_pow2(c_i, GLA_BLOCK)
    causal = jnp.logical_and(same, c_i <= r_i)
    causal_b = causal.astype(BF16)
    same_b = same.astype(BF16)
    lane_k = lax.broadcasted_iota(jnp.int32, (1, GLA_QK_W), 1)
    lane_v = lax.broadcasted_iota(jnp.int32, (1, GLA_W), 1)
    kms = [(lane_k >= GLA_DK * h) & (lane_k < GLA_DK * (h + 1)) for h in range(GLA_HEADS)]
    vms = [(lane_v >= GLA_DV * h) & (lane_v < GLA_DV * (h + 1)) for h in range(GLA_HEADS)]
    sr = _div_pow2(lax.broadcasted_iota(jnp.int32, (GLA_QK_W, GLA_W), 0), GLA_DK)
    sc = _div_pow2(lax.broadcasted_iota(jnp.int32, (GLA_QK_W, GLA_W), 1), GLA_DV)
    bd = sr == sc
    eye = (lax.broadcasted_iota(jnp.int32, (GLA_QK_W, GLA_QK_W), 0)
           == lax.broadcasted_iota(jnp.int32, (GLA_QK_W, GLA_QK_W), 1))
    g = g_ref[...]
    seg = seg_ref[...]

    rows_b = range(nb)
    heads = range(GLA_HEADS)
    las = [_split(la_ref[bi]) for bi in rows_b]
    bs = [_dot(causal_b, hi) + _dot(causal_b, lo) for hi, lo in las]
    bls = [_dot(same_b, hi) + _dot(same_b, lo) for hi, lo in las]
    qts = [(q_ref[bi] * jnp.exp(bs[bi]) * GLA_SCALE).astype(BF16) for bi in rows_b]
    kts = [(k_ref[bi] * jnp.exp(-bs[bi])).astype(BF16) for bi in rows_b]
    kds = [(k_ref[bi] * jnp.exp(bls[bi] - bs[bi])).astype(BF16) for bi in rows_b]
    vbs = [v_ref[bi].astype(BF16) for bi in rows_b]
    decays = [jnp.exp(bl) for bl in bls]

    atts = [[_dot_nt(jnp.where(kms[h], qts[bi], jnp.zeros_like(qts[bi])), kts[bi]) for h in heads]
            for bi in rows_b]
    atts = [[jnp.where(causal, a, 0.0).astype(BF16) for a in row] for row in atts]
    pvs = [[_dot(atts[bi][h], vbs[bi]) for h in heads] for bi in rows_b]
    intras = []
    for bi in rows_b:
        intra = jnp.zeros((tc, GLA_W), F32)
        for h in heads:
            intra = intra + jnp.where(vms[h], pvs[bi][h], 0.0)
        intras.append(intra)

    states = [s_ref[bi] for bi in rows_b]
    inters = [[] for _ in rows_b]
    for j in range(nsub):
        rows = slice(j * GLA_BLOCK, (j + 1) * GLA_BLOCK)
        for bi in rows_b:
            inters[bi].append(_dot(qts[bi][rows], states[bi].astype(BF16)))
        upds = [_dot_tn(kds[bi][rows], vbs[bi][rows]) for bi in rows_b]
        for bi in rows_b:
            d_row = decays[bi][j * GLA_BLOCK:j * GLA_BLOCK + 1]
            d_col = jnp.sum(jnp.where(eye, d_row, 0.0), axis=1, keepdims=True)
            states[bi] = d_col * states[bi] + jnp.where(bd, upds[bi], 0.0)
    os_ = [intras[bi] + jnp.concatenate(inters[bi], axis=0) for bi in rows_b]
    splits = [_split(o * o) for o in os_]
    sss = [_dot(hi, seg) + _dot(lo, seg) for hi, lo in splits]
    for bi in rows_b:
        s_ref[bi] = states[bi]
        o = os_[bi] * lax.rsqrt(sss[bi] * (1.0 / GLA_DV) + NORM_EPS) * g
        rg = rg_ref[bi]
        o_ref[bi] = (o * (rg * _sigmoid(rg))).astype(BF16)


def _gla(qg, kg, vg, la, rg, s0_bd, wl, tc, nb):
    B, T, _ = qg.shape
    tc = min(tc, T)
    nb = min(nb, B)
    blk = lambda w: pl.BlockSpec((nb, tc, w), lambda b, c: (b, c, 0))
    st = pl.BlockSpec((nb, GLA_QK_W, GLA_W), lambda b, c: (b, 0, 0))
    return pl.pallas_call(
        functools.partial(_gla_kernel, tc=tc, nb=nb),
        grid=(B // nb, T // tc),
        in_specs=[blk(GLA_QK_W), blk(GLA_QK_W), blk(GLA_W), blk(GLA_QK_W), blk(GLA_W), st,
                  _full((1, GLA_W)), _full((GLA_W, GLA_W))],
        out_specs=[blk(GLA_W), st],
        out_shape=[jax.ShapeDtypeStruct((B, T, GLA_W), BF16),
                   jax.ShapeDtypeStruct((B, GLA_QK_W, GLA_W), F32)],
        compiler_params=_params(("parallel", "arbitrary")),
        name="gla",
    )(qg, kg, vg, la, rg, s0_bd, wl['g_gla_out'], wl['seg64_256'])


def _state_to_blockdiag(s):
    B = s.shape[0]
    eye = jnp.eye(GLA_HEADS, dtype=s.dtype)
    return jnp.einsum('bhkv,hg->bhkgv', s, eye).reshape(B, GLA_QK_W, GLA_W)


def _state_from_blockdiag(s):
    B = s.shape[0]
    s5 = s.reshape(B, GLA_HEADS, GLA_DK, GLA_HEADS, GLA_DV)
    return jnp.stack([s5[:, h, :, h, :] for h in range(GLA_HEADS)], axis=1)


def _mla_kv_kernel(lat_ref, kr_ref, wn_ref, wv_ref, gkn_ref, seg_ref, place_ref, kcat_ref, v_ref):
    lat = lat_ref[...].astype(BF16)
    kn = _seg_rms(_dot(lat, wn_ref[...]), gkn_ref[...], seg_ref[...], MLA_NOPE)
    v = _dot(lat, wv_ref[...])
    krp = _dot_tn(kr_ref[...].astype(BF16), place_ref[...])
    lane = lax.broadcasted_iota(jnp.int32, (1, LANES), 1)
    for p in range(MLA_HEADS // 2):
        pair = kn[:, LANES * p:LANES * (p + 1)]
        kcat_ref[2 * p] = jnp.where(lane < MLA_NOPE, pair, krp).astype(BF16)
        kcat_ref[2 * p + 1] = jnp.where(lane < MLA_NOPE, pltpu.roll(pair, MLA_NOPE, 1), krp).astype(BF16)
        v_ref[p] = v[:, LANES * p:LANES * (p + 1)].astype(BF16)


def _mla_kv(lat, kr_t, wl, tr):
    B, L, _ = lat.shape
    tr = min(tr, L)
    return pl.pallas_call(
        _mla_kv_kernel,
        grid=(B, L // tr),
        in_specs=[pl.BlockSpec((None, tr, MLA_KV_LORA), lambda b, i: (b, i, 0)),
                  pl.BlockSpec((None, MLA_ROPE, tr), lambda b, i: (b, 0, i)),
                  _full(wl['w_ukn'].shape), _full(wl['w_ukv_v'].shape), _full((1, MLA_HEADS * MLA_NOPE)),
                  _full(wl['seg64_512'].shape), _full((MLA_ROPE, LANES))],
        out_specs=[pl.BlockSpec((None, MLA_HEADS, tr, LANES), lambda b, i: (b, 0, i, 0)),
                   pl.BlockSpec((None, MLA_HEADS // 2, tr, LANES), lambda b, i: (b, 0, i, 0))],
        out_shape=[jax.ShapeDtypeStruct((B, MLA_HEADS, L, LANES), BF16),
                   jax.ShapeDtypeStruct((B, MLA_HEADS // 2, L, LANES), BF16)],
        compiler_params=_params(("parallel", "parallel")),
        name="mla_kv",
    )(lat, kr_t, wl['w_ukn'], wl['w_ukv_v'], wl['g_kn'], wl['seg64_512'], wl['rope_place'])


def _mla_attn_kernel(q_ref, k_ref, v_ref, g_ref, seg_ref, o_ref, *, tq):
    qi = pl.program_id(1)
    q0 = pl.multiple_of(qi * tq, tq)
    vis = (_div_pow2(lax.broadcasted_iota(jnp.int32, (tq, tq), 1), CHUNK)
           <= _div_pow2(lax.broadcasted_iota(jnp.int32, (tq, tq), 0), CHUNK))
    qs = [q_ref[h] for h in range(MLA_HEADS)]

    heads = range(MLA_HEADS)

    def step(start, st, mask):
        ss = [_dot_nt(qs[h], k_ref[h, pl.ds(start, tq), :]) for h in heads]
        if mask:
            ss = [jnp.where(vis, s, MASK_NEG) for s in ss]
        ms, ls, ps = [], [], []
        for h in heads:
            bm = jnp.max(ss[h], axis=-1, keepdims=True)
            m_new = bm if st is None else jnp.maximum(st[3 * h], bm)
            p = jnp.exp(ss[h] - m_new)
            ms.append(m_new)
            ls.append(jnp.sum(p, axis=-1, keepdims=True))
            ps.append(p.astype(BF16))
        pvs = [_dot(ps[h], v_ref[h // 2, pl.ds(start, tq), :]) for h in heads]
        out = []
        for h in heads:
            if st is None:
                out += [ms[h], ls[h], pvs[h]]
            else:
                a = jnp.exp(st[3 * h] - ms[h])
                out += [ms[h], a * st[3 * h + 1] + ls[h], a * st[3 * h + 2] + pvs[h]]
        return tuple(out)

    st = step(q0, None, True)
    st = lax.fori_loop(0, qi, lambda kb, st: step(pl.multiple_of(kb * tq, tq), st, False), st)
    lane = lax.broadcasted_iota(jnp.int32, (1, LANES), 1)
    pairs = []
    for p in range(MLA_HEADS // 2):
        even = st[6 * p + 2] / st[6 * p + 1]
        odd = st[6 * p + 5] / st[6 * p + 4]
        pairs.append(jnp.where(lane < MLA_V, even, odd))
    o = jnp.concatenate(pairs, axis=1)
    o_ref[...] = _seg_rms(o, g_ref[...], seg_ref[...], MLA_V).astype(BF16)


def _mla_attn(qcat, kcat, v, wl, tq):
    B, _, T, _ = qcat.shape
    tq = min(tq, T)
    return pl.pallas_call(
        functools.partial(_mla_attn_kernel, tq=tq),
        grid=(B, T // tq),
        in_specs=[pl.BlockSpec((None, MLA_HEADS, tq, LANES), lambda b, i: (b, 0, i, 0)),
                  pl.BlockSpec((None, MLA_HEADS, T, LANES), lambda b, i: (b, 0, 0, 0)),
                  pl.BlockSpec((None, MLA_HEADS // 2, T, LANES), lambda b, i: (b, 0, 0, 0)),
                  _full((1, MLA_W)), _full((MLA_W, MLA_W))],
        out_specs=pl.BlockSpec((None, tq, MLA_W), lambda b, i: (b, i, 0)),
        out_shape=jax.ShapeDtypeStruct((B, T, MLA_W), BF16),
        compiler_params=_params(("parallel", "parallel")),
        name="mla_attn",
    )(qcat, kcat, v, wl['g_mla_out_512'], wl['seg64_512'])


def _mla_dec_kernel(qn_ref, qr_ref, lat_ref, krt_ref, latn_ref, krn_ref, wknt_ref, wv_ref, gkn_ref,
                    g_ref, seg_ref, o_ref, s_ref, *, kc):
    T = qn_ref.shape[0]
    P = lat_ref.shape[0]
    H = MLA_HEADS
    wknt = wknt_ref[...]
    qn = (qn_ref[...].astype(F32) * gkn_ref[...]).astype(BF16)
    qabs = jnp.concatenate(
        [_dot(qn[:, MLA_NOPE * h:MLA_NOPE * (h + 1)], wknt[MLA_NOPE * h:MLA_NOPE * (h + 1), :])
         for h in range(H)], axis=0).astype(BF16)
    qr = qr_ref[...]
    qrs = jnp.concatenate([qr[:, MLA_ROPE * h:MLA_ROPE * (h + 1)] for h in range(H)], axis=0)

    def scores(lat_b, sr, n):
        kn_t = _dot_nt(wknt, lat_b)
        ss = jnp.sum((kn_t * kn_t).reshape(H, MLA_NOPE, n), axis=1)
        r = lax.rsqrt(ss * (1.0 / MLA_NOPE) + NORM_EPS)
        sn = _dot_nt(qabs, lat_b).reshape(H, T, n)
        return (sn * r[:, None, :]).reshape(H * T, n) + sr

    m = None
    for c in range(P // kc):
        cols = slice(c * kc, (c + 1) * kc)
        s = scores(lat_ref[cols, :].astype(BF16), _dot(qrs, krt_ref[:, cols].astype(BF16)), kc)
        s_ref[:, cols] = s
        mc = jnp.max(s, axis=-1, keepdims=True)
        m = mc if m is None else jnp.maximum(m, mc)
    latn = latn_ref[...].astype(BF16)
    s_new = scores(latn, _dot_nt(qrs, krn_ref[...].astype(BF16)), T)
    m = jnp.maximum(m, jnp.max(s_new, axis=-1, keepdims=True))

    p = jnp.exp(s_new - m)
    l = jnp.sum(p, axis=-1, keepdims=True)
    ctx = _dot(p.astype(BF16), latn)
    for c in range(P // kc):
        cols = slice(c * kc, (c + 1) * kc)
        p = jnp.exp(s_ref[:, cols] - m)
        l = l + jnp.sum(p, axis=-1, keepdims=True)
        ctx = ctx + _dot(p.astype(BF16), lat_ref[cols, :].astype(BF16))
    ctx = (ctx / l).astype(BF16)

    wv = wv_ref[...]
    lane = lax.broadcasted_iota(jnp.int32, (1, MLA_W), 1)
    o = jnp.zeros((T, MLA_W), F32)
    for h in range(H):
        oh = _dot(ctx[T * h:T * (h + 1)], wv)
        o = jnp.where((lane >= MLA_V * h) & (lane < MLA_V * (h + 1)), oh, o)
    o_ref[...] = _seg_rms(o, g_ref[...], seg_ref[...], MLA_V).astype(BF16)


def _mla_dec(qn, qr, lat_cache, kr_t_cache, layer, lat_new, kr_new, T, wl, kc):
    n = qn.shape[0]
    B = n // T
    P = lat_cache.shape[2]
    row = lambda w: pl.BlockSpec((T, w), lambda b: (b, 0))
    return pl.pallas_call(
        functools.partial(_mla_dec_kernel, kc=kc),
        grid=(B,),
        in_specs=[row(MLA_HEADS * MLA_NOPE), row(MLA_HEADS * MLA_ROPE),
                  pl.BlockSpec((None, None, P, MLA_KV_LORA), lambda b: (layer, b, 0, 0)),
                  pl.BlockSpec((None, None, MLA_ROPE, P), lambda b: (layer, b, 0, 0)),
                  row(MLA_KV_LORA), row(MLA_ROPE),
                  _full(wl['w_ukn_t'].shape), _full(wl['w_ukv_v'].shape), _full((1, MLA_HEADS * MLA_NOPE)),
                  _full((1, MLA_W)), _full((MLA_W, MLA_W))],
        out_specs=row(MLA_W),
        out_shape=jax.ShapeDtypeStruct((n, MLA_W), BF16),
        scratch_shapes=[pltpu.VMEM((MLA_HEADS * T, P), F32)],
        compiler_params=_params(("parallel",)),
        name="mla_dec",
    )(qn, qr, lat_cache, kr_t_cache, lat_new, kr_new, wl['w_ukn_t'], wl['w_ukv_v'], wl['g_kn'],
      wl['g_mla_out_512'], wl['seg64_512'])


def _post_kernel(x_ref, oa_ref, ob_ref, oc_ref, mk_ref, mv_ref, wo_ref, gcross_ref, wcq_ref, gcqn_ref,
                 wco_ref, o_ref, att_ref, *, nb, tt):
    wo = wo_ref
    mix = (_dot(oa_ref[...], wo[:SB_W, :]) + _dot(ob_ref[...], wo[SB_W:SB_W + GLA_W, :])
           + _dot(oc_ref[...], wo[SB_W + GLA_W:, :]))
    x1 = x_ref[...] + mix
    h = _rms(x1, gcross_ref[...]).astype(BF16)
    q = _dot(h, wcq_ref[...])
    gq = gcqn_ref[...]
    qs = []
    for hd in range(MEM_HEADS):
        qh = _rms(q[:, MEM_HD * hd:MEM_HD * (hd + 1)], gq).astype(BF16)
        qs.append(qh)
    pairs = [(bi, hd) for bi in range(nb) for hd in range(MEM_HEADS)]
    rows = lambda bi: slice(bi * tt, (bi + 1) * tt)
    cols = lambda hd: slice(MEM_HD * hd, MEM_HD * (hd + 1))
    ss = [_dot_nt(qs[hd][rows(bi)], mk_ref[bi, :, cols(hd)].astype(BF16)) * MEM_SCALE for bi, hd in pairs]
    ps, ls = [], []
    for s in ss:
        p = jnp.exp(s - jnp.max(s, axis=-1, keepdims=True))
        ls.append(jnp.sum(p, axis=-1, keepdims=True))
        ps.append(p.astype(BF16))
    pvs = [_dot(ps[i], mv_ref[bi, :, cols(hd)].astype(BF16)) for i, (bi, hd) in enumerate(pairs)]
    for i, (bi, hd) in enumerate(pairs):
        att_ref[rows(bi), cols(hd)] = pvs[i] / ls[i]
    o_ref[...] = x1 + _dot(att_ref[...].astype(BF16), wco_ref[...])


def _post(x2d, oa, ob, oc, mem_k, mem_v, T, wl, tm):
    n = x2d.shape[0]
    tm = min(tm, n)
    tt = min(T, tm)
    nb = tm // tt
    tiles_per_b = T // tt
    row = lambda w: pl.BlockSpec((tm, w), lambda i: (i, 0))
    mem = pl.BlockSpec((nb, MEM_LEN, MEM_W), lambda i: (i // tiles_per_b if nb == 1 else i, 0, 0))
    weights = (wl['w_out'], wl['g_cross'], wl['w_cq'], wl['g_cqn'], wl['w_co'])
    return pl.pallas_call(
        functools.partial(_post_kernel, nb=nb, tt=tt),
        grid=(n // tm,),
        in_specs=[row(D_MODEL), row(SB_W), row(GLA_W), row(MLA_W), mem, mem] + [_full(w.shape) for w in weights],
        out_specs=row(D_MODEL),
        out_shape=jax.ShapeDtypeStruct((n, D_MODEL), F32),
        scratch_shapes=[pltpu.VMEM((tm, MEM_W), F32)],
        compiler_params=_params(("parallel",)),
        name="post",
    )(x2d, oa, ob, oc, mem_k, mem_v, *weights)


def _ffn_kernel(x_ref, g_ref, wg_ref, wu_ref, wd_ref, o_ref, h_ref, acc_ref):
    j = pl.program_id(1)

    @pl.when(j == 0)
    def _():
        x = x_ref[...]
        h_ref[...] = _rms(x, g_ref[...]).astype(BF16)
        acc_ref[...] = x

    h = h_ref[...]
    gate = _dot(h, wg_ref[...])
    up = _dot(h, wu_ref[...])
    act = (gate * _sigmoid(gate) * up).astype(BF16)
    acc_ref[...] += _dot(act, wd_ref[...])

    @pl.when(j == pl.num_programs(1) - 1)
    def _():
        o_ref[...] = acc_ref[...]


def _ffn(x2d, wl, tm, tf):
    n = x2d.shape[0]
    tm = min(tm, n)
    return pl.pallas_call(
        _ffn_kernel,
        grid=(n // tm, D_FF // tf),
        in_specs=[pl.BlockSpec((tm, D_MODEL), lambda i, j: (i, 0)), _full((1, D_MODEL)),
                  pl.BlockSpec((D_MODEL, tf), lambda i, j: (0, j)),
                  pl.BlockSpec((D_MODEL, tf), lambda i, j: (0, j)),
                  pl.BlockSpec((tf, D_MODEL), lambda i, j: (j, 0))],
        out_specs=pl.BlockSpec((tm, D_MODEL), lambda i, j: (i, 0)),
        out_shape=jax.ShapeDtypeStruct((n, D_MODEL), F32),
        scratch_shapes=[pltpu.VMEM((tm, D_MODEL), BF16), pltpu.VMEM((tm, D_MODEL), F32)],
        compiler_params=_params(("parallel", "arbitrary")),
        name="ffn",
    )(x2d, wl['g_ffn'], wl['w_gate_ffn'], wl['w_up'], wl['w_down'])


def _memkv_kernel(m_ref, g_ref, wk_ref, wv_ref, gk_ref, k_ref, v_ref):
    h = _rms(m_ref[...], g_ref[...]).astype(BF16)
    k = _dot(h, wk_ref[...])
    gk = gk_ref[...]
    for hd in range(MEM_HEADS):
        cols = slice(MEM_HD * hd, MEM_HD * (hd + 1))
        k_ref[:, cols] = _rms(k[:, cols], gk)
    v_ref[...] = _dot(h, wv_ref[...])


def _memkv(mem2d, wl, tm):
    n = mem2d.shape[0]
    tm = min(tm, n)
    row = lambda w: pl.BlockSpec((tm, w), lambda i: (i, 0))
    weights = (wl['g_mem'], wl['w_ck'], wl['w_cv'], wl['g_ckn'])
    return pl.pallas_call(
        _memkv_kernel,
        grid=(n // tm,),
        in_specs=[row(D_MODEL)] + [_full(w.shape) for w in weights],
        out_specs=[row(MEM_W), row(MEM_W)],
        out_shape=[jax.ShapeDtypeStruct((n, MEM_W), F32)] * 2,
        compiler_params=_params(("parallel",)),
        name="memkv",
    )(mem2d, *weights)


def _prep_layer(p, l):
    w_in = p['w_in'][l]
    o = np.cumsum([0, SB_W, SB_W, SB_W, GLA_QK_W, GLA_QK_W, GLA_W, GLA_GATE_RANK, GLA_W,
                   MLA_Q_LORA, MLA_KV_LORA, MLA_ROPE])
    seg = lambda i, j: w_in[:, o[i]:o[j]]
    pad_cols = lambda w, n: jnp.pad(w, ((0, 0), (0, n - w.shape[1])))
    row = lambda g, reps=1: jnp.tile(g, reps)[None, :].astype(F32)
    w_uq = p['w_uq'][l].reshape(MLA_Q_LORA, MLA_HEADS, MLA_NOPE + MLA_ROPE)
    w_ukv = p['w_ukv'][l].reshape(MLA_KV_LORA, MLA_HEADS, MLA_NOPE + MLA_V)
    place = np.zeros((MLA_ROPE, LANES), np.float32)
    place[np.arange(MLA_ROPE), MLA_NOPE + np.arange(MLA_ROPE)] = 1.0
    return {
        'g_mix': row(p['g_mix_norm'][l]),
        'w_sb': seg(0, 3).astype(BF16),
        'w_gl': jnp.concatenate([seg(3, 6), seg(7, 8)], axis=1).astype(BF16),
        'w_ag': pad_cols(seg(6, 7), LANES).astype(BF16),
        'w_ml': pad_cols(seg(8, 11), 4 * LANES).astype(BF16),
        'w_gate': jnp.pad(p['w_gla_gate'][l], ((0, LANES - GLA_GATE_RANK), (0, 0))).astype(BF16),
        'b_gate': row(p['b_gla_gate'][l]),
        'g_cq': row(p['g_cq'][l]),
        'w_uq': jnp.concatenate([w_uq[:, :, :MLA_NOPE].reshape(MLA_Q_LORA, -1),
                                 w_uq[:, :, MLA_NOPE:].reshape(MLA_Q_LORA, -1)], axis=1).astype(BF16),
        'g_qn': row(p['g_qn'][l], MLA_HEADS),
        'g_qr': row(p['g_qr'][l], MLA_HEADS),
        'g_kr': jnp.pad(p['g_kr'][l], (0, LANES - MLA_ROPE))[None, :],
        'g_ckv': row(p['g_ckv'][l]),
        'w_ukn': w_ukv[:, :, :MLA_NOPE].reshape(MLA_KV_LORA, -1).astype(BF16),
        'w_ukv_v': w_ukv[:, :, MLA_NOPE:].reshape(MLA_KV_LORA, -1).astype(BF16),
        'g_kn': row(p['g_kn'][l], MLA_HEADS),
        'w_ukn_t': w_ukv[:, :, :MLA_NOPE].reshape(MLA_KV_LORA, -1).T.astype(BF16),
        'g_mla_out_512': row(p['g_mla_out'][l], MLA_HEADS),
        'g_sb_out': row(p['g_sb_out'][l], SB_HEADS),
        'g_gla_out': row(p['g_gla_out'][l], GLA_HEADS),
        'w_out': p['w_out'][l].astype(BF16),
        'g_cross': row(p['g_cross_norm'][l]),
        'g_mem': row(p['g_mem_norm'][l]),
        'w_cq': p['w_cq'][l].astype(BF16),
        'w_ck': p['w_ck'][l].astype(BF16),
        'w_cv': p['w_cv'][l].astype(BF16),
        'g_cqn': row(p['g_cqn'][l]),
        'g_ckn': row(p['g_ckn'][l]),
        'w_co': p['w_co'][l].astype(BF16),
        'g_ffn': row(p['g_ffn_norm'][l]),
        'w_gate_ffn': p['w_gate'][l].astype(BF16),
        'w_up': p['w_up'][l].astype(BF16),
        'w_down': p['w_down'][l].astype(BF16),
        'seg64_512': _seg_matrix(512, 64),
        'seg32_256': _seg_matrix(256, 32),
        'seg64_256': _seg_matrix(256, 64),
        'rope_place': jnp.asarray(place, dtype=BF16),
    }


def _time_minor_caches(cache_sb_k, cache_sb_v, cache_mla_latent, cache_mla_krope):
    return (jnp.transpose(cache_sb_k, (0, 1, 3, 4, 2)), jnp.transpose(cache_sb_v, (0, 1, 3, 4, 2)),
            cache_mla_latent, jnp.transpose(cache_mla_krope, (0, 1, 3, 2)))


def _trunk_layer(x, caches, layer, gla_s0, mem_k, mem_v, wl):
    B, T, _ = x.shape
    has_past = caches is not None
    past = caches[0].shape[-1] if has_past else 0
    x2d = x.reshape(B * T, D_MODEL)
    qa, ka, va, qg, kg, vg, la, rg, *q_mla, lat, kr = _inproj(x2d, B, T, past, wl, tm=512,
                                                              time_minor=not has_past)
    r3 = lambda a: a.reshape(B, T, a.shape[-1])

    if has_past:
        sb_kt, sb_vt, lat_cache, kr_t_cache = caches
        o_a = _sb_attn(r3(qa), r3(ka), r3(va), sb_kt, sb_vt, layer, wl, tq=256, tk=256, window=512)
        ka_out = ka.reshape(B, T, SB_HEADS, SB_HD)
        va_out = va.reshape(B, T, SB_HEADS, SB_HD)
        kr_out = r3(kr)
    else:
        o_a = _sb_attn(r3(qa), ka, va, None, None, None, wl, tq=256, tk=256, window=None)
        ka_out = jnp.transpose(ka, (0, 3, 1, 2))
        va_out = jnp.transpose(va, (0, 3, 1, 2))
        kr_out = jnp.transpose(kr, (0, 2, 1))

    if gla_s0 is None:
        s0 = jnp.zeros((B, GLA_QK_W, GLA_W), F32)
    else:
        s0 = _state_to_blockdiag(gla_s0)
    o_b, s_bd = _gla(r3(qg), r3(kg), r3(vg), r3(la), r3(rg), s0, wl, tc=128, nb=4)
    gla_s = _state_from_blockdiag(s_bd)

    if has_past:
        o_c = _mla_dec(*q_mla, lat_cache, kr_t_cache, layer, lat, kr, T, wl, kc=1024)
    else:
        kcat, v_pairs = _mla_kv(r3(lat), kr, wl, tr=512)
        o_c = _mla_attn(*q_mla, kcat, v_pairs, wl, tq=256)

    x2d = _post(x2d, o_a.reshape(B * T, SB_W), o_b.reshape(B * T, GLA_W), o_c.reshape(B * T, MLA_W),
                mem_k.reshape(B, MEM_LEN, MEM_W), mem_v.reshape(B, MEM_LEN, MEM_W), T, wl, tm=512)
    x2d = _ffn(x2d, wl, tm=512, tf=D_FF // 2)
    return x2d.reshape(B, T, D_MODEL), ka_out, va_out, gla_s, r3(lat), kr_out


def kernel(x_prompt, x_sample, mem_prompt, cache_sb_k, cache_sb_v, state_gla, cache_mla_latent,
           cache_mla_krope, cache_mem_k, cache_mem_v, g_mix_norm, w_in, w_gla_gate, b_gla_gate,
           g_gla_out, g_sb_out, g_cq, w_uq, g_qn, g_qr, g_kr, g_ckv, w_ukv, g_kn, g_mla_out, w_out,
           g_cross_norm, g_mem_norm, w_cq, w_ck, w_cv, g_cqn, g_ckn, w_co, g_ffn_norm, w_gate,
           w_up, w_down):
    p = dict(g_mix_norm=g_mix_norm, w_in=w_in, w_gla_gate=w_gla_gate, b_gla_gate=b_gla_gate,
             g_gla_out=g_gla_out, g_sb_out=g_sb_out, g_cq=g_cq, w_uq=w_uq, g_qn=g_qn, g_qr=g_qr,
             g_kr=g_kr, g_ckv=g_ckv, w_ukv=w_ukv, g_kn=g_kn, g_mla_out=g_mla_out, w_out=w_out,
             g_cross_norm=g_cross_norm, g_mem_norm=g_mem_norm, w_cq=w_cq, w_ck=w_ck, w_cv=w_cv,
             g_cqn=g_cqn, g_ckn=g_ckn, w_co=w_co, g_ffn_norm=g_ffn_norm, w_gate=w_gate, w_up=w_up,
             w_down=w_down)
    depth = w_in.shape[0]
    B = x_prompt.shape[0]
    xp, xs = x_prompt, x_sample
    outs_p = [[] for _ in range(7)]
    outs_s = [[] for _ in range(5)]
    caches = _time_minor_caches(cache_sb_k, cache_sb_v, cache_mla_latent, cache_mla_krope)
    for l in range(depth):
        wl = _prep_layer(p, l)
        mk, mv = _memkv(mem_prompt.reshape(B * MEM_LEN, D_MODEL), wl, tm=512)
        mk = mk.reshape(B, MEM_LEN, MEM_HEADS, MEM_HD)
        mv = mv.reshape(B, MEM_LEN, MEM_HEADS, MEM_HD)
        xp, ka, va, st, lat, kr = _trunk_layer(xp, None, l, None, mk, mv, wl)
        for lst, val in zip(outs_p, (ka, va, st, lat, kr, mk, mv)):
            lst.append(val)
        xs, ka, va, st, lat, kr = _trunk_layer(xs, caches, l, state_gla[l], cache_mem_k[l], cache_mem_v[l], wl)
        for lst, val in zip(outs_s, (ka, va, st, lat, kr)):
            lst.append(val)
    return (xp, xs, *[jnp.stack(v) for v in outs_p], *[jnp.stack(v) for v in outs_s])
```

```python
import functools
import math

import jax
import jax.numpy as jnp
import numpy as np
from jax import lax
from jax.experimental import pallas as pl
from jax.experimental.pallas import tpu as pltpu

F32 = jnp.float32
BF16 = jnp.bfloat16

D_MODEL = 1024
CHUNK = 64
SB_HEADS, SB_HD = 4, 64
SB_W = SB_HEADS * SB_HD
GLA_HEADS, GLA_DK, GLA_DV = 4, 32, 64
GLA_QK_W = GLA_HEADS * GLA_DK
GLA_W = GLA_HEADS * GLA_DV
GLA_GATE_RANK = 16
GLA_TAU = 16.0
GLA_BLOCK = 16
MLA_HEADS = 8
MLA_Q_LORA, MLA_KV_LORA = 256, 128
MLA_NOPE, MLA_ROPE, MLA_V = 64, 32, 64
MLA_W = MLA_HEADS * MLA_V
ROPE_THETA = 10000.0
MEM_LEN, MEM_HEADS, MEM_HD = 256, 4, 128
MEM_W = MEM_HEADS * MEM_HD
D_FF = 2816
NORM_EPS = 1e-6

LANES = 128
VMEM_LIMIT = 56 * 1024 * 1024
SB_SCALE = 1.0 / math.sqrt(SB_HD)
GLA_SCALE = GLA_DK ** -0.5
MLA_SCALE = (MLA_NOPE + MLA_ROPE) ** -0.5
MEM_SCALE = MEM_HD ** -0.5
LOG2E = math.log2(math.e)
SB_Q_SCALE = SB_SCALE * LOG2E
MLA_Q_SCALE2 = MLA_SCALE * LOG2E
SB_DEAD_LOG2 = -105.0 * LOG2E
MASK_NEG = -1e30


def _dot(a, b):
    return jnp.dot(a, b, preferred_element_type=F32)


def _dot_nt(a, b):
    return lax.dot_general(a, b, (((1,), (1,)), ((), ())), preferred_element_type=F32)


def _dot_tn(a, b):
    return lax.dot_general(a, b, (((0,), (0,)), ((), ())), preferred_element_type=F32)


def _split(x):
    hi = x.astype(BF16)
    lo = (x - hi.astype(F32)).astype(BF16)
    return hi, lo


def _dot_split_lhs(x, m):
    hi, lo = _split(x)
    return _dot(hi, m) + _dot(lo, m)


def _dot_split_rhs(m, x):
    hi, lo = _split(x)
    return _dot(m, hi) + _dot(m, lo)


def _rms(x, g):
    ms = jnp.mean(x * x, axis=-1, keepdims=True)
    return x * lax.rsqrt(ms + NORM_EPS) * g


def _seg_rms(x, g, seg, width):
    ss = _dot_split_lhs(x * x, seg)
    return x * lax.rsqrt(ss * (1.0 / width) + NORM_EPS) * g


def _softplus(z):
    return jnp.maximum(z, 0.0) + jnp.log1p(jnp.exp(-jnp.abs(z)))


def _softplus2(z2):
    return jnp.maximum(z2, 0.0) + jnp.log2(1.0 + jnp.exp2(-jnp.abs(z2)))


def _sigmoid(z):
    return 1.0 / (1.0 + jnp.exp(-z))


def _swap_halves(x, seg):
    n = x.shape[-1]
    half = seg // 2
    lane = lax.broadcasted_iota(jnp.int32, x.shape, x.ndim - 1)
    first = (lane & (seg - 1)) < half
    return jnp.where(first, pltpu.roll(x, n - half, x.ndim - 1), pltpu.roll(x, half, x.ndim - 1))


def _div_pow2(x, d):
    return lax.shift_right_logical(x, int(math.log2(d)))


def _seg_matrix(n, width):
    i = np.arange(n) // width
    return jnp.asarray(i[:, None] == i[None, :], dtype=BF16)


def _full(shape):
    nd = len(shape)
    return pl.BlockSpec(shape, lambda *_: (0,) * nd)


def _params(sem):
    return pltpu.CompilerParams(dimension_semantics=sem, vmem_limit_bytes=VMEM_LIMIT)


def _inproj_kernel(x_ref, gmix_ref, wsb_ref, wgl_ref, wag_ref, wml_ref, wgate_ref, bgate_ref,
                   gcq_ref, wuq_ref, gqn_ref, gqr_ref, gkr_ref, gckv_ref, seg64_ref, seg32_ref,
                   cos_ref, sin_ref,
                   qa_ref, ka_ref, va_ref, qg_ref, kg_ref, vg_ref, la_ref, rg_ref, *mla_refs, time_minor):
    if time_minor:
        qcat_ref, lat_ref, kr_ref = mla_refs
    else:
        qn_ref, qr_ref, lat_ref, kr_ref = mla_refs
    tm = x_ref.shape[0]
    h = _rms(x_ref[...], gmix_ref[...]).astype(BF16)

    sb = _dot(h, wsb_ref[...])
    qa_ref[...] = (sb[:, :SB_W] * SB_Q_SCALE).astype(BF16)
    ka = sb[:, SB_W:2 * SB_W]
    va = sb[:, 2 * SB_W:]
    if time_minor:
        ka_ref[...] = ka.T.reshape(SB_HEADS, SB_HD, tm)
        va_ref[...] = va.T.reshape(SB_HEADS, SB_HD, tm)
    else:
        ka_ref[...] = ka
        va_ref[...] = va

    gl = _dot(h, wgl_ref[...])
    qg_ref[...] = gl[:, :GLA_QK_W]
    kg_ref[...] = gl[:, GLA_QK_W:2 * GLA_QK_W]
    vg_ref[...] = gl[:, 2 * GLA_QK_W:2 * GLA_QK_W + GLA_W]
    rg_ref[...] = gl[:, 2 * GLA_QK_W + GLA_W:]
    ag = _dot(h, wag_ref[...])
    gate = _dot(ag.astype(BF16), wgate_ref[...]) + bgate_ref[...]
    la_ref[...] = -_softplus(-gate) * (1.0 / GLA_TAU)

    ml = _dot(h, wml_ref[...])
    cq = ml[:, :MLA_Q_LORA]
    ckv = ml[:, MLA_Q_LORA:MLA_Q_LORA + MLA_KV_LORA]
    krp = ml[:, MLA_Q_LORA + MLA_KV_LORA:]
    lat_ref[...] = _rms(ckv, gckv_ref[...])
    cos = cos_ref[...]
    sin = sin_ref[...]
    kr_ms = jnp.sum(krp * krp, axis=-1, keepdims=True) * (1.0 / MLA_ROPE)
    krn = krp * lax.rsqrt(kr_ms + NORM_EPS) * gkr_ref[...]
    kr_rot = krn * cos[:, :LANES] + _swap_halves(krn, MLA_ROPE) * sin[:, :LANES]
    if time_minor:
        kr_ref[...] = kr_rot.T[:MLA_ROPE, :]
    else:
        kr_ref[...] = kr_rot[:, :MLA_ROPE]

    q = _dot(_rms(cq, gcq_ref[...]).astype(BF16), wuq_ref[...])
    nope_w = MLA_HEADS * MLA_NOPE
    q_scale = MLA_Q_SCALE2 if time_minor else MLA_SCALE
    qn = _seg_rms(q[:, :nope_w], gqn_ref[...], seg64_ref[...], MLA_NOPE) * q_scale
    qr = _seg_rms(q[:, nope_w:], gqr_ref[...], seg32_ref[...], MLA_ROPE)
    qr = (qr * cos + _swap_halves(qr, MLA_ROPE) * sin) * q_scale
    if not time_minor:
        qn_ref[...] = qn.astype(BF16)
        qr_ref[...] = qr.astype(BF16)
        return
    lane = lax.broadcasted_iota(jnp.int32, (1, LANES), 1)
    rope_heads_per_tile = LANES // MLA_ROPE
    for hd in range(MLA_HEADS):
        pair = qn[:, LANES * (hd // 2):LANES * (hd // 2 + 1)]
        nope = pair if hd % 2 == 0 else pltpu.roll(pair, MLA_NOPE, 1)
        tile = qr[:, LANES * (hd // rope_heads_per_tile):LANES * (hd // rope_heads_per_tile + 1)]
        shift = (MLA_NOPE - MLA_ROPE * (hd % rope_heads_per_tile)) % LANES
        rope = tile if shift == 0 else pltpu.roll(tile, shift, 1)
        qcat_ref[hd] = jnp.where(lane < MLA_NOPE, nope,
                                 jnp.where(lane < MLA_NOPE + MLA_ROPE, rope, 0.0)).astype(BF16)


def _rope_tables(pos):
    half = MLA_ROPE // 2
    freqs = ROPE_THETA ** (-jnp.arange(half, dtype=F32) / half)
    ang = pos.astype(F32)[:, None] * freqs[None, :]
    cos = jnp.cos(ang)
    sin = jnp.sin(ang)
    cos_t = jnp.tile(jnp.concatenate([cos, cos], axis=-1), (1, MLA_HEADS))
    sin_t = jnp.tile(jnp.concatenate([-sin, sin], axis=-1), (1, MLA_HEADS))
    return cos_t, sin_t


def _inproj(x2d, B, T, past, wl, tm, time_minor):
    n = x2d.shape[0]
    tm = min(tm, n)
    cos_t, sin_t = _rope_tables(past + jnp.arange(T, dtype=jnp.int32))
    if T < tm:
        cos_t = jnp.tile(cos_t, (tm // T, 1))
        sin_t = jnp.tile(sin_t, (tm // T, 1))
    n_tab = cos_t.shape[0] // tm
    row = lambda w: pl.BlockSpec((tm, w), lambda i: (i, 0))
    tab = pl.BlockSpec((tm, 2 * LANES), lambda i: (i % n_tab, 0))
    weights = (wl['g_mix'], wl['w_sb'], wl['w_gl'], wl['w_ag'], wl['w_ml'], wl['w_gate'], wl['b_gate'],
               wl['g_cq'], wl['w_uq'], wl['g_qn'], wl['g_qr'], wl['g_kr'], wl['g_ckv'],
               wl['seg64_512'], wl['seg32_256'])
    rowout = lambda w, dt: (row(w), jax.ShapeDtypeStruct((n, w), dt))
    if time_minor:
        assert T % tm == 0
        tpb = T // tm
        kv_out = (pl.BlockSpec((None, SB_HEADS, SB_HD, tm), lambda i: (i // tpb, 0, 0, i % tpb)),
                  jax.ShapeDtypeStruct((B, SB_HEADS, SB_HD, T), F32))
        kr_out = (pl.BlockSpec((None, MLA_ROPE, tm), lambda i: (i // tpb, 0, i % tpb)),
                  jax.ShapeDtypeStruct((B, MLA_ROPE, T), F32))
        q_out = [(pl.BlockSpec((None, MLA_HEADS, tm, LANES), lambda i: (i // tpb, 0, i % tpb, 0)),
                  jax.ShapeDtypeStruct((B, MLA_HEADS, T, LANES), BF16))]
    else:
        kv_out = rowout(SB_W, F32)
        kr_out = rowout(MLA_ROPE, F32)
        q_out = [rowout(MLA_HEADS * MLA_NOPE, BF16), rowout(MLA_HEADS * MLA_ROPE, BF16)]
    outs = [rowout(SB_W, BF16), kv_out, kv_out,
            rowout(GLA_QK_W, F32), rowout(GLA_QK_W, F32), rowout(GLA_W, F32), rowout(GLA_QK_W, F32),
            rowout(GLA_W, F32), *q_out, rowout(MLA_KV_LORA, F32), kr_out]
    return pl.pallas_call(
        functools.partial(_inproj_kernel, time_minor=time_minor),
        grid=(n // tm,),
        in_specs=[row(D_MODEL)] + [_full(w.shape) for w in weights] + [tab, tab],
        out_specs=[o[0] for o in outs],
        out_shape=[o[1] for o in outs],
        compiler_params=_params(("parallel",)),
        name="inproj",
    )(x2d, *weights, cos_t, sin_t)


def _sb_kernel(*refs, tq, tk, has_past, n_far_blocks, layer):
    if has_past:
        (q_ref, kn_ref, vn_ref, kt_ref, vt_ref, kt_all_ref, vt_all_ref, g_ref, seg_ref, o_ref,
         kbuf_ref, vbuf_ref) = refs
    else:
        q_ref, kt_ref, vt_ref, g_ref, seg_ref, o_ref = refs
    qi = pl.program_id(1)
    q = q_ref[...]
    lane = lax.broadcasted_iota(jnp.int32, (1, SB_W), 1)
    hmask = [(lane >= SB_HD * h) & (lane < SB_HD * (h + 1)) for h in range(SB_HEADS)]
    qh = [jnp.where(hmask[h], q, jnp.zeros_like(q)) for h in range(SB_HEADS)]

    def suffix_matrix(n):
        return (lax.broadcasted_iota(jnp.int32, (n, n), 0)
                > lax.broadcasted_iota(jnp.int32, (n, n), 1)).astype(BF16)

    def time_minor_block(ref, start, n):
        return ref[:, :, pl.ds(start, n)].reshape(SB_W, n).astype(BF16)

    def block(kblk, vblk, time_minor, vis, umat, cs, acc):
        heads = range(SB_HEADS)
        zs = [_dot(qh[h], kblk) if time_minor else _dot_nt(qh[h], kblk) for h in heads]
        lks = [-_softplus2(z) for z in zs]
        if vis is not None:
            lks = [jnp.where(vis, lk, 0.0) for lk in lks]
        splits = [_split(lk) for lk in lks]
        laters = [_dot(hi, umat) + _dot(lo, umat) for hi, lo in splits]
        ws = [jnp.exp2(zs[h] + lks[h] + laters[h] + cs[h]) for h in heads]
        if vis is not None:
            ws = [jnp.where(vis, w, 0.0) for w in ws]
        ws = [w.astype(BF16) for w in ws]
        pvs = [_dot_nt(w, vblk) if time_minor else _dot(w, vblk) for w in ws]
        for h in heads:
            acc = acc + jnp.where(hmask[h], pvs[h], 0.0)
        new_cs = [cs[h] + laters[h][:, :1] + lks[h][:, :1] for h in heads]
        return new_cs, acc

    def dead(cs):
        m = cs[0]
        for c in cs[1:]:
            m = jnp.maximum(m, c)
        return jnp.max(m)

    vis = (lax.broadcasted_iota(jnp.int32, (tq, tq), 1)
           < lax.broadcasted_iota(jnp.int32, (tq, tq), 0))
    zero_c = jnp.zeros((tq, 1), F32)
    init = ([zero_c] * SB_HEADS, jnp.zeros((tq, SB_W), F32))
    if has_past:
        cs, acc = block(kn_ref[...].astype(BF16), vn_ref[...].astype(BF16), False, vis,
                        suffix_matrix(tq), *init)
        kb0 = jnp.int32(kt_ref.shape[-1] // tk - 1)
    else:
        q0 = pl.multiple_of(qi * tq, tq)
        cs, acc = block(time_minor_block(kt_ref, q0, tq), time_minor_block(vt_ref, q0, tq), True, vis,
                        suffix_matrix(tq), *init)
        kb0 = qi - 1

    umat = suffix_matrix(tk)

    def walk(fetch, kb0, cs, acc):
        def cond(st):
            return jnp.logical_and(st[0] >= 0, st[1] > SB_DEAD_LOG2)

        def body(st):
            kb = st[0]
            cs, acc = list(st[2:2 + SB_HEADS]), st[2 + SB_HEADS]
            kblk, vblk = fetch(pl.multiple_of(kb * tk, tk))
            cs, acc = block(kblk, vblk, True, None, umat, cs, acc)
            return (kb - 1, dead(cs), *cs, acc)

        st = lax.while_loop(cond, body, (kb0, dead(cs), *cs, acc))
        return list(st[2:2 + SB_HEADS]), st[2 + SB_HEADS]

    cs, acc = walk(lambda start: (time_minor_block(kt_ref, start, tk), time_minor_block(vt_ref, start, tk)),
                   kb0, cs, acc)
    if has_past and n_far_blocks:
        b = pl.program_id(0)

        def fetch_far(start):
            pltpu.sync_copy(kt_all_ref.at[layer, b, :, :, pl.ds(start, tk)], kbuf_ref)
            pltpu.sync_copy(vt_all_ref.at[layer, b, :, :, pl.ds(start, tk)], vbuf_ref)
            return (kbuf_ref[...].reshape(SB_W, tk).astype(BF16), vbuf_ref[...].reshape(SB_W, tk).astype(BF16))

        cs, acc = walk(fetch_far, jnp.int32(n_far_blocks - 1), cs, acc)
    o_ref[...] = _seg_rms(acc, g_ref[...], seg_ref[...], SB_HD).astype(BF16)


def _sb_attn(q, k_new, v_new, kt_cache, vt_cache, layer, wl, tq, tk, window):
    B, T, _ = q.shape
    tq = min(tq, T)
    has_past = kt_cache is not None
    n_far_blocks = 0
    scratch = []
    in_specs = [pl.BlockSpec((None, tq, SB_W), lambda b, i: (b, i, 0))]
    if has_past:
        P = kt_cache.shape[-1]
        window = min(window, P)
        assert tq == T and P % window == 0 and window % tk == 0
        n_far_blocks = (P - window) // tk
        new = pl.BlockSpec((None, T, SB_W), lambda b, i: (b, 0, 0))
        near = pl.BlockSpec((None, None, SB_HEADS, SB_HD, window), lambda b, i: (layer, b, 0, 0, P // window - 1))
        hbm = pl.BlockSpec(memory_space=pl.ANY)
        in_specs += [new, new, near, near, hbm, hbm]
        args = [q, k_new, v_new, kt_cache, vt_cache, kt_cache, vt_cache]
        scratch = [pltpu.VMEM((SB_HEADS, SB_HD, tk), F32)] * 2
    else:
        assert tq == tk
        new = pl.BlockSpec((None, SB_HEADS, SB_HD, T), lambda b, i: (b, 0, 0, 0))
        in_specs += [new, new]
        args = [q, k_new, v_new]
    in_specs += [_full((1, SB_W)), _full((SB_W, SB_W))]
    args += [wl['g_sb_out'], wl['seg64_256']]
    return pl.pallas_call(
        functools.partial(_sb_kernel, tq=tq, tk=tk, has_past=has_past, n_far_blocks=n_far_blocks, layer=layer),
        grid=(B, T // tq),
        in_specs=in_specs,
        out_specs=pl.BlockSpec((None, tq, SB_W), lambda b, i: (b, i, 0)),
        out_shape=jax.ShapeDtypeStruct((B, T, SB_W), BF16),
        scratch_shapes=scratch,
        compiler_params=_params(("parallel", "parallel")),
        name="sb_attn",
    )(*args)


def _gla_kernel(q_ref, k_ref, v_ref, la_ref, rg_ref, s0_ref, g_ref, seg_ref, o_ref, s_ref, *, tc, nb):
    ci = pl.program_id(1)

    @pl.when(ci == 0)
    def _():
        s_ref[...] = s0_ref[...]

    nsub = tc // GLA_BLOCK
    r_i = lax.broadcasted_iota(jnp.int32, (tc, tc), 0)
    c_i = lax.broadcasted_iota(jnp.int32, (tc, tc), 1)
    same = _div_pow2(r_i, GLA_BLOCK) == _div_pow2(c_i, GLA_BLOCK)
    causal = jnp.logical_and(same, c_i <= r_i)
    causal_b = causal.astype(BF16)
    same_b = same.astype(BF16)
    lane_k = lax.broadcasted_iota(jnp.int32, (1, GLA_QK_W), 1)
    lane_v = lax.broadcasted_iota(jnp.int32, (1, GLA_W), 1)
    kms = [(lane_k >= GLA_DK * h) & (lane_k < GLA_DK * (h + 1)) for h in range(GLA_HEADS)]
    vms = [(lane_v >= GLA_DV * h) & (lane_v < GLA_DV * (h + 1)) for h in range(GLA_HEADS)]
    sr = _div_pow2(lax.broadcasted_iota(jnp.int32, (GLA_QK_W, GLA_W), 0), GLA_DK)
    sc = _div_pow2(lax.broadcasted_iota(jnp.int32, (GLA_QK_W, GLA_W), 1), GLA_DV)
    bd = sr == sc
    eye = (lax.broadcasted_iota(jnp.int32, (GLA_QK_W, GLA_QK_W), 0)
           == lax.broadcasted_iota(jnp.int32, (GLA_QK_W, GLA_QK_W), 1))
    g = g_ref[...]
    seg = seg_ref[...]

    rows_b = range(nb)
    heads = range(GLA_HEADS)
    las = [_split(la_ref[bi]) for bi in rows_b]
    bs = [_dot(causal_b, hi) + _dot(causal_b, lo) for hi, lo in las]
    bls = [_dot(same_b, hi) + _dot(same_b, lo) for hi, lo in las]
    qts = [(q_ref[bi] * jnp.exp(bs[bi]) * GLA_SCALE).astype(BF16) for bi in rows_b]
    kts = [(k_ref[bi] * jnp.exp(-bs[bi])).astype(BF16) for bi in rows_b]
    kds = [(k_ref[bi] * jnp.exp(bls[bi] - bs[bi])).astype(BF16) for bi in rows_b]
    vbs = [v_ref[bi].astype(BF16) for bi in rows_b]
    decays = [jnp.exp(bl) for bl in bls]

    atts = [[_dot_nt(jnp.where(kms[h], qts[bi], jnp.zeros_like(qts[bi])), kts[bi]) for h in heads]
            for bi in rows_b]
    atts = [[jnp.where(causal, a, 0.0).astype(BF16) for a in row] for row in atts]
    pvs = [[_dot(atts[bi][h], vbs[bi]) for h in heads] for bi in rows_b]
    intras = []
    for bi in rows_b:
        intra = jnp.zeros((tc, GLA_W), F32)
        for h in heads:
            intra = intra + jnp.where(vms[h], pvs[bi][h], 0.0)
        intras.append(intra)

    states = [s_ref[bi] for bi in rows_b]
    inters = [[] for _ in rows_b]
    for j in range(nsub):
        rows = slice(j * GLA_BLOCK, (j + 1) * GLA_BLOCK)
        for bi in rows_b:
            inters[bi].append(_dot(qts[bi][rows], states[bi].astype(BF16)))
        upds = [_dot_tn(kds[bi][rows], vbs[bi][rows]) for bi in rows_b]
        for bi in rows_b:
            d_row = decays[bi][j * GLA_BLOCK:j * GLA_BLOCK + 1]
            d_col = jnp.sum(jnp.where(eye, d_row, 0.0), axis=1, keepdims=True)
            states[bi] = d_col * states[bi] + jnp.where(bd, upds[bi], 0.0)
    os_ = [intras[bi] + jnp.concatenate(inters[bi], axis=0) for bi in rows_b]
    splits = [_split(o * o) for o in os_]
    sss = [_dot(hi, seg) + _dot(lo, seg) for hi, lo in splits]
    for bi in rows_b:
        s_ref[bi] = states[bi]
        o = os_[bi] * lax.rsqrt(sss[bi] * (1.0 / GLA_DV) + NORM_EPS) * g
        rg = rg_ref[bi]
        o_ref[bi] = (o * (rg * _sigmoid(rg))).astype(BF16)


def _gla(qg, kg, vg, la, rg, s0_bd, wl, tc, nb):
    B, T, _ = qg.shape
    tc = min(tc, T)
    nb = min(nb, B)
    blk = lambda w: pl.BlockSpec((nb, tc, w), lambda b, c: (b, c, 0))
    st = pl.BlockSpec((nb, GLA_QK_W, GLA_W), lambda b, c: (b, 0, 0))
    return pl.pallas_call(
        functools.partial(_gla_kernel, tc=tc, nb=nb),
        grid=(B // nb, T // tc),
        in_specs=[blk(GLA_QK_W), blk(GLA_QK_W), blk(GLA_W), blk(GLA_QK_W), blk(GLA_W), st,
                  _full((1, GLA_W)), _full((GLA_W, GLA_W))],
        out_specs=[blk(GLA_W), st],
        out_shape=[jax.ShapeDtypeStruct((B, T, GLA_W), BF16),
                   jax.ShapeDtypeStruct((B, GLA_QK_W, GLA_W), F32)],
        compiler_params=_params(("parallel", "arbitrary")),
        name="gla",
    )(qg, kg, vg, la, rg, s0_bd, wl['g_gla_out'], wl['seg64_256'])


def _state_to_blockdiag(s):
    B = s.shape[0]
    eye = jnp.eye(GLA_HEADS, dtype=s.dtype)
    return jnp.einsum('bhkv,hg->bhkgv', s, eye).reshape(B, GLA_QK_W, GLA_W)


def _state_from_blockdiag(s):
    B = s.shape[0]
    s5 = s.reshape(B, GLA_HEADS, GLA_DK, GLA_HEADS, GLA_DV)
    return jnp.stack([s5[:, h, :, h, :] for h in range(GLA_HEADS)], axis=1)


def _mla_kv_kernel(lat_ref, kr_ref, wn_ref, wvt_ref, gkn_ref, seg_ref, place_ref, kcat_ref, vt_ref):
    tr = lat_ref.shape[0]
    lat = lat_ref[...].astype(BF16)
    kn = _seg_rms(_dot(lat, wn_ref[...]), gkn_ref[...], seg_ref[...], MLA_NOPE)
    vt_ref[...] = _dot_nt(wvt_ref[...], lat).reshape(MLA_HEADS, MLA_V, tr).astype(BF16)
    krp = _dot_tn(kr_ref[...].astype(BF16), place_ref[...])
    lane = lax.broadcasted_iota(jnp.int32, (1, LANES), 1)
    for p in range(MLA_HEADS // 2):
        pair = kn[:, LANES * p:LANES * (p + 1)]
        kcat_ref[2 * p] = jnp.where(lane < MLA_NOPE, pair, krp).astype(BF16)
        kcat_ref[2 * p + 1] = jnp.where(lane < MLA_NOPE, pltpu.roll(pair, MLA_NOPE, 1), krp).astype(BF16)


def _mla_kv(lat, kr_t, wl, tr):
    B, L, _ = lat.shape
    tr = min(tr, L)
    return pl.pallas_call(
        _mla_kv_kernel,
        grid=(B, L // tr),
        in_specs=[pl.BlockSpec((None, tr, MLA_KV_LORA), lambda b, i: (b, i, 0)),
                  pl.BlockSpec((None, MLA_ROPE, tr), lambda b, i: (b, 0, i)),
                  _full(wl['w_ukn'].shape), _full(wl['w_ukv_v_t'].shape), _full((1, MLA_HEADS * MLA_NOPE)),
                  _full(wl['seg64_512'].shape), _full((MLA_ROPE, LANES))],
        out_specs=[pl.BlockSpec((None, MLA_HEADS, tr, LANES), lambda b, i: (b, 0, i, 0)),
                   pl.BlockSpec((None, MLA_HEADS, MLA_V, tr), lambda b, i: (b, 0, 0, i))],
        out_shape=[jax.ShapeDtypeStruct((B, MLA_HEADS, L, LANES), BF16),
                   jax.ShapeDtypeStruct((B, MLA_HEADS, MLA_V, L), BF16)],
        compiler_params=_params(("parallel", "parallel")),
        name="mla_kv",
    )(lat, kr_t, wl['w_ukn'], wl['w_ukv_v_t'], wl['g_kn'], wl['seg64_512'], wl['rope_place'])


def _mla_attn_kernel(q_ref, k_ref, vt_ref, g_ref, o_ref, *, tq):
    qi = pl.program_id(1)
    q0 = pl.multiple_of(qi * tq, tq)
    vis = (_div_pow2(lax.broadcasted_iota(jnp.int32, (tq, tq), 0), CHUNK)
           <= _div_pow2(lax.broadcasted_iota(jnp.int32, (tq, tq), 1), CHUNK))
    qs = [q_ref[h] for h in range(MLA_HEADS)]

    heads = range(MLA_HEADS)

    def step(start, st, mask):
        ss = [_dot_nt(k_ref[h, pl.ds(start, tq), :], qs[h]) for h in heads]
        if mask:
            ss = [jnp.where(vis, s, MASK_NEG) for s in ss]
        out = []
        for h in heads:
            bm = jnp.max(ss[h], axis=0, keepdims=True)
            m_new = bm if st is None else jnp.maximum(st[3 * h], bm)
            p = jnp.exp2(ss[h] - m_new)
            l_blk = jnp.sum(p, axis=0, keepdims=True)
            pv = _dot(vt_ref[h, :, pl.ds(start, tq)], p.astype(BF16))
            if st is None:
                out += [m_new, l_blk, pv]
            else:
                a = jnp.exp2(st[3 * h] - m_new)
                out += [m_new, a * st[3 * h + 1] + l_blk, a * st[3 * h + 2] + pv]
        return tuple(out)

    st = step(q0, None, True)
    st = lax.fori_loop(0, qi, lambda kb, st: step(pl.multiple_of(kb * tq, tq), st, False), st)
    g = g_ref[...]
    outs = []
    for h in heads:
        o = st[3 * h + 2] / st[3 * h + 1]
        ms = jnp.mean(o * o, axis=0, keepdims=True)
        outs.append(o * lax.rsqrt(ms + NORM_EPS) * g)
    o_ref[...] = jnp.concatenate(outs, axis=0).T.astype(BF16)


def _mla_attn(qcat, kcat, v_t, wl, tq):
    B, _, T, _ = qcat.shape
    tq = min(tq, T)
    return pl.pallas_call(
        functools.partial(_mla_attn_kernel, tq=tq),
        grid=(B, T // tq),
        in_specs=[pl.BlockSpec((None, MLA_HEADS, tq, LANES), lambda b, i: (b, 0, i, 0)),
                  pl.BlockSpec((None, MLA_HEADS, T, LANES), lambda b, i: (b, 0, 0, 0)),
                  pl.BlockSpec((None, MLA_HEADS, MLA_V, T), lambda b, i: (b, 0, 0, 0)),
                  _full((MLA_V, 1))],
        out_specs=pl.BlockSpec((None, tq, MLA_W), lambda b, i: (b, i, 0)),
        out_shape=jax.ShapeDtypeStruct((B, T, MLA_W), BF16),
        compiler_params=_params(("parallel", "parallel")),
        name="mla_attn",
    )(qcat, kcat, v_t, wl['g_mla_out_col'])


def _mla_dec_kernel(qn_ref, qr_ref, lat_ref, krt_ref, latn_ref, krn_ref, wknt_ref, wv_ref, gkn_ref,
                    g_ref, seg_ref, o_ref, s_ref, *, kc):
    T = qn_ref.shape[0]
    P = lat_ref.shape[0]
    H = MLA_HEADS
    wknt = wknt_ref[...]
    qn = (qn_ref[...].astype(F32) * gkn_ref[...]).astype(BF16)
    qabs = jnp.concatenate(
        [_dot(qn[:, MLA_NOPE * h:MLA_NOPE * (h + 1)], wknt[MLA_NOPE * h:MLA_NOPE * (h + 1), :])
         for h in range(H)], axis=0).astype(BF16)
    qr = qr_ref[...]
    qrs = jnp.concatenate([qr[:, MLA_ROPE * h:MLA_ROPE * (h + 1)] for h in range(H)], axis=0)

    def scores(lat_b, sr, n):
        kn_t = _dot_nt(wknt, lat_b)
        ss = jnp.sum((kn_t * kn_t).reshape(H, MLA_NOPE, n), axis=1)
        r = lax.rsqrt(ss * (1.0 / MLA_NOPE) + NORM_EPS)
        sn = _dot_nt(qabs, lat_b).reshape(H, T, n)
        return (sn * r[:, None, :]).reshape(H * T, n) + sr

    m = None
    for c in range(P // kc):
        cols = slice(c * kc, (c + 1) * kc)
        s = scores(lat_ref[cols, :].astype(BF16), _dot(qrs, krt_ref[:, cols].astype(BF16)), kc)
        s_ref[:, cols] = s
        mc = jnp.max(s, axis=-1, keepdims=True)
        m = mc if m is None else jnp.maximum(m, mc)
    latn = latn_ref[...].astype(BF16)
    s_new = scores(latn, _dot_nt(qrs, krn_ref[...].astype(BF16)), T)
    m = jnp.maximum(m, jnp.max(s_new, axis=-1, keepdims=True))

    p = jnp.exp(s_new - m)
    l = jnp.sum(p, axis=-1, keepdims=True)
    ctx = _dot(p.astype(BF16), latn)
    for c in range(P // kc):
        cols = slice(c * kc, (c + 1) * kc)
        p = jnp.exp(s_ref[:, cols] - m)
        l = l + jnp.sum(p, axis=-1, keepdims=True)
        ctx = ctx + _dot(p.astype(BF16), lat_ref[cols, :].astype(BF16))
    ctx = (ctx / l).astype(BF16)

    wv = wv_ref[...]
    lane = lax.broadcasted_iota(jnp.int32, (1, MLA_W), 1)
    o = jnp.zeros((T, MLA_W), F32)
    for h in range(H):
        oh = _dot(ctx[T * h:T * (h + 1)], wv)
        o = jnp.where((lane >= MLA_V * h) & (lane < MLA_V * (h + 1)), oh, o)
    o_ref[...] = _seg_rms(o, g_ref[...], seg_ref[...], MLA_V).astype(BF16)


def _mla_dec(qn, qr, lat_cache, kr_t_cache, layer, lat_new, kr_new, T, wl, kc):
    n = qn.shape[0]
    B = n // T
    P = lat_cache.shape[2]
    row = lambda w: pl.BlockSpec((T, w), lambda b: (b, 0))
    return pl.pallas_call(
        functools.partial(_mla_dec_kernel, kc=kc),
        grid=(B,),
        in_specs=[row(MLA_HEADS * MLA_NOPE), row(MLA_HEADS * MLA_ROPE),
                  pl.BlockSpec((None, None, P, MLA_KV_LORA), lambda b: (layer, b, 0, 0)),
                  pl.BlockSpec((None, None, MLA_ROPE, P), lambda b: (layer, b, 0, 0)),
                  row(MLA_KV_LORA), row(MLA_ROPE),
                  _full(wl['w_ukn_t'].shape), _full(wl['w_ukv_v'].shape), _full((1, MLA_HEADS * MLA_NOPE)),
                  _full((1, MLA_W)), _full((MLA_W, MLA_W))],
        out_specs=row(MLA_W),
        out_shape=jax.ShapeDtypeStruct((n, MLA_W), BF16),
        scratch_shapes=[pltpu.VMEM((MLA_HEADS * T, P), F32)],
        compiler_params=_params(("parallel",)),
        name="mla_dec",
    )(qn, qr, lat_cache, kr_t_cache, lat_new, kr_new, wl['w_ukn_t'], wl['w_ukv_v'], wl['g_kn'],
      wl['g_mla_out_512'], wl['seg64_512'])


def _post_kernel(x_ref, oa_ref, ob_ref, oc_ref, mk_ref, mv_ref, wo_ref, gcross_ref, wcq_ref, gcqn_ref,
                 wco_ref, o_ref, att_ref, *, nb, tt):
    wo = wo_ref
    mix = (_dot(oa_ref[...], wo[:SB_W, :]) + _dot(ob_ref[...], wo[SB_W:SB_W + GLA_W, :])
           + _dot(oc_ref[...], wo[SB_W + GLA_W:, :]))
    x1 = x_ref[...] + mix
    h = _rms(x1, gcross_ref[...]).astype(BF16)
    q = _dot(h, wcq_ref[...])
    gq = gcqn_ref[...]
    qs = []
    for hd in range(MEM_HEADS):
        qh = _rms(q[:, MEM_HD * hd:MEM_HD * (hd + 1)], gq).astype(BF16)
        qs.append(qh)
    pairs = [(bi, hd) for bi in range(nb) for hd in range(MEM_HEADS)]
    rows = lambda bi: slice(bi * tt, (bi + 1) * tt)
    cols = lambda hd: slice(MEM_HD * hd, MEM_HD * (hd + 1))
    ss = [_dot_nt(qs[hd][rows(bi)], mk_ref[bi, :, cols(hd)].astype(BF16)) * MEM_SCALE for bi, hd in pairs]
    ps, ls = [], []
    for s in ss:
        p = jnp.exp(s - jnp.max(s, axis=-1, keepdims=True))
        ls.append(jnp.sum(p, axis=-1, keepdims=True))
        ps.append(p.astype(BF16))
    pvs = [_dot(ps[i], mv_ref[bi, :, cols(hd)].astype(BF16)) for i, (bi, hd) in enumerate(pairs)]
    for i, (bi, hd) in enumerate(pairs):
        att_ref[rows(bi), cols(hd)] = pvs[i] / ls[i]
    o_ref[...] = x1 + _dot(att_ref[...].astype(BF16), wco_ref[...])


def _post(x2d, oa, ob, oc, mem_k, mem_v, T, wl, tm):
    n = x2d.shape[0]
    tm = min(tm, n)
    tt = min(T, tm)
    nb = tm // tt
    tiles_per_b = T // tt
    row = lambda w: pl.BlockSpec((tm, w), lambda i: (i, 0))
    mem = pl.BlockSpec((nb, MEM_LEN, MEM_W), lambda i: (i // tiles_per_b if nb == 1 else i, 0, 0))
    weights = (wl['w_out'], wl['g_cross'], wl['w_cq'], wl['g_cqn'], wl['w_co'])
    return pl.pallas_call(
        functools.partial(_post_kernel, nb=nb, tt=tt),
        grid=(n // tm,),
        in_specs=[row(D_MODEL), row(SB_W), row(GLA_W), row(MLA_W), mem, mem] + [_full(w.shape) for w in weights],
        out_specs=row(D_MODEL),
        out_shape=jax.ShapeDtypeStruct((n, D_MODEL), F32),
        scratch_shapes=[pltpu.VMEM((tm, MEM_W), F32)],
        compiler_params=_params(("parallel",)),
        name="post",
    )(x2d, oa, ob, oc, mem_k, mem_v, *weights)


def _ffn_kernel(x_ref, g_ref, wg_ref, wu_ref, wd_ref, o_ref, h_ref, acc_ref):
    j = pl.program_id(1)

    @pl.when(j == 0)
    def _():
        x = x_ref[...]
        h_ref[...] = _rms(x, g_ref[...]).astype(BF16)
        acc_ref[...] = x

    h = h_ref[...]
    gate = _dot(h, wg_ref[...])
    up = _dot(h, wu_ref[...])
    act = (gate * _sigmoid(gate) * up).astype(BF16)
    acc_ref[...] += _dot(act, wd_ref[...])

    @pl.when(j == pl.num_programs(1) - 1)
    def _():
        o_ref[...] = acc_ref[...]


def _ffn(x2d, wl, tm, tf):
    n = x2d.shape[0]
    tm = min(tm, n)
    return pl.pallas_call(
        _ffn_kernel,
        grid=(n // tm, D_FF // tf),
        in_specs=[pl.BlockSpec((tm, D_MODEL), lambda i, j: (i, 0)), _full((1, D_MODEL)),
                  pl.BlockSpec((D_MODEL, tf), lambda i, j: (0, j)),
                  pl.BlockSpec((D_MODEL, tf), lambda i, j: (0, j)),
                  pl.BlockSpec((tf, D_MODEL), lambda i, j: (j, 0))],
        out_specs=pl.BlockSpec((tm, D_MODEL), lambda i, j: (i, 0)),
        out_shape=jax.ShapeDtypeStruct((n, D_MODEL), F32),
        scratch_shapes=[pltpu.VMEM((tm, D_MODEL), BF16), pltpu.VMEM((tm, D_MODEL), F32)],
        compiler_params=_params(("parallel", "arbitrary")),
        name="ffn",
    )(x2d, wl['g_ffn'], wl['w_gate_ffn'], wl['w_up'], wl['w_down'])


def _memkv_kernel(m_ref, g_ref, wk_ref, wv_ref, gk_ref, k_ref, v_ref):
    h = _rms(m_ref[...], g_ref[...]).astype(BF16)
    k = _dot(h, wk_ref[...])
    gk = gk_ref[...]
    for hd in range(MEM_HEADS):
        cols = slice(MEM_HD * hd, MEM_HD * (hd + 1))
        k_ref[:, cols] = _rms(k[:, cols], gk)
    v_ref[...] = _dot(h, wv_ref[...])


def _memkv(mem2d, wl, tm):
    n = mem2d.shape[0]
    tm = min(tm, n)
    row = lambda w: pl.BlockSpec((tm, w), lambda i: (i, 0))
    weights = (wl['g_mem'], wl['w_ck'], wl['w_cv'], wl['g_ckn'])
    return pl.pallas_call(
        _memkv_kernel,
        grid=(n // tm,),
        in_specs=[row(D_MODEL)] + [_full(w.shape) for w in weights],
        out_specs=[row(MEM_W), row(MEM_W)],
        out_shape=[jax.ShapeDtypeStruct((n, MEM_W), F32)] * 2,
        compiler_params=_params(("parallel",)),
        name="memkv",
    )(mem2d, *weights)


def _prep_layer(p, l):
    w_in = p['w_in'][l]
    o = np.cumsum([0, SB_W, SB_W, SB_W, GLA_QK_W, GLA_QK_W, GLA_W, GLA_GATE_RANK, GLA_W,
                   MLA_Q_LORA, MLA_KV_LORA, MLA_ROPE])
    seg = lambda i, j: w_in[:, o[i]:o[j]]
    pad_cols = lambda w, n: jnp.pad(w, ((0, 0), (0, n - w.shape[1])))
    row = lambda g, reps=1: jnp.tile(g, reps)[None, :].astype(F32)
    w_uq = p['w_uq'][l].reshape(MLA_Q_LORA, MLA_HEADS, MLA_NOPE + MLA_ROPE)
    w_ukv = p['w_ukv'][l].reshape(MLA_KV_LORA, MLA_HEADS, MLA_NOPE + MLA_V)
    place = np.zeros((MLA_ROPE, LANES), np.float32)
    place[np.arange(MLA_ROPE), MLA_NOPE + np.arange(MLA_ROPE)] = 1.0
    return {
        'g_mix': row(p['g_mix_norm'][l]),
        'w_sb': seg(0, 3).astype(BF16),
        'w_gl': jnp.concatenate([seg(3, 6), seg(7, 8)], axis=1).astype(BF16),
        'w_ag': pad_cols(seg(6, 7), LANES).astype(BF16),
        'w_ml': pad_cols(seg(8, 11), 4 * LANES).astype(BF16),
        'w_gate': jnp.pad(p['w_gla_gate'][l], ((0, LANES - GLA_GATE_RANK), (0, 0))).astype(BF16),
        'b_gate': row(p['b_gla_gate'][l]),
        'g_cq': row(p['g_cq'][l]),
        'w_uq': jnp.concatenate([w_uq[:, :, :MLA_NOPE].reshape(MLA_Q_LORA, -1),
                                 w_uq[:, :, MLA_NOPE:].reshape(MLA_Q_LORA, -1)], axis=1).astype(BF16),
        'g_qn': row(p['g_qn'][l], MLA_HEADS),
        'g_qr': row(p['g_qr'][l], MLA_HEADS),
        'g_kr': jnp.pad(p['g_kr'][l], (0, LANES - MLA_ROPE))[None, :],
        'g_ckv': row(p['g_ckv'][l]),
        'w_ukn': w_ukv[:, :, :MLA_NOPE].reshape(MLA_KV_LORA, -1).astype(BF16),
        'w_ukv_v': w_ukv[:, :, MLA_NOPE:].reshape(MLA_KV_LORA, -1).astype(BF16),
        'g_kn': row(p['g_kn'][l], MLA_HEADS),
        'w_ukn_t': w_ukv[:, :, :MLA_NOPE].reshape(MLA_KV_LORA, -1).T.astype(BF16),
        'w_ukv_v_t': w_ukv[:, :, MLA_NOPE:].reshape(MLA_KV_LORA, -1).T.astype(BF16),
        'g_mla_out_512': row(p['g_mla_out'][l], MLA_HEADS),
        'g_mla_out_col': p['g_mla_out'][l][:, None].astype(F32),
        'g_sb_out': row(p['g_sb_out'][l], SB_HEADS),
        'g_gla_out': row(p['g_gla_out'][l], GLA_HEADS),
        'w_out': p['w_out'][l].astype(BF16),
        'g_cross': row(p['g_cross_norm'][l]),
        'g_mem': row(p['g_mem_norm'][l]),
        'w_cq': p['w_cq'][l].astype(BF16),
        'w_ck': p['w_ck'][l].astype(BF16),
        'w_cv': p['w_cv'][l].astype(BF16),
        'g_cqn': row(p['g_cqn'][l]),
        'g_ckn': row(p['g_ckn'][l]),
        'w_co': p['w_co'][l].astype(BF16),
        'g_ffn': row(p['g_ffn_norm'][l]),
        'w_gate_ffn': p['w_gate'][l].astype(BF16),
        'w_up': p['w_up'][l].astype(BF16),
        'w_down': p['w_down'][l].astype(BF16),
        'seg64_512': _seg_matrix(512, 64),
        'seg32_256': _seg_matrix(256, 32),
        'seg64_256': _seg_matrix(256, 64),
        'rope_place': jnp.asarray(place, dtype=BF16),
    }


def _time_minor_caches(cache_sb_k, cache_sb_v, cache_mla_latent, cache_mla_krope):
    return (jnp.transpose(cache_sb_k, (0, 1, 3, 4, 2)), jnp.transpose(cache_sb_v, (0, 1, 3, 4, 2)),
            cache_mla_latent, jnp.transpose(cache_mla_krope, (0, 1, 3, 2)))


def _trunk_layer(x, caches, layer, gla_s0, mem_k, mem_v, wl):
    B, T, _ = x.shape
    has_past = caches is not None
    past = caches[0].shape[-1] if has_past else 0
    x2d = x.reshape(B * T, D_MODEL)
    qa, ka, va, qg, kg, vg, la, rg, *q_mla, lat, kr = _inproj(x2d, B, T, past, wl, tm=512,
                                                              time_minor=not has_past)
    r3 = lambda a: a.reshape(B, T, a.shape[-1])

    if has_past:
        sb_kt, sb_vt, lat_cache, kr_t_cache = caches
        o_a = _sb_attn(r3(qa), r3(ka), r3(va), sb_kt, sb_vt, layer, wl, tq=256, tk=256, window=512)
        ka_out = ka.reshape(B, T, SB_HEADS, SB_HD)
        va_out = va.reshape(B, T, SB_HEADS, SB_HD)
        kr_out = r3(kr)
    else:
        o_a = _sb_attn(r3(qa), ka, va, None, None, None, wl, tq=256, tk=256, window=None)
        ka_out = jnp.transpose(ka, (0, 3, 1, 2))
        va_out = jnp.transpose(va, (0, 3, 1, 2))
        kr_out = jnp.transpose(kr, (0, 2, 1))

    if gla_s0 is None:
        s0 = jnp.zeros((B, GLA_QK_W, GLA_W), F32)
    else:
        s0 = _state_to_blockdiag(gla_s0)
    o_b, s_bd = _gla(r3(qg), r3(kg), r3(vg), r3(la), r3(rg), s0, wl, tc=128, nb=4)
    gla_s = _state_from_blockdiag(s_bd)

    if has_past:
        o_c = _mla_dec(*q_mla, lat_cache, kr_t_cache, layer, lat, kr, T, wl, kc=1024)
    else:
        kcat, v_pairs = _mla_kv(r3(lat), kr, wl, tr=512)
        o_c = _mla_attn(*q_mla, kcat, v_pairs, wl, tq=256)

    x2d = _post(x2d, o_a.reshape(B * T, SB_W), o_b.reshape(B * T, GLA_W), o_c.reshape(B * T, MLA_W),
                mem_k.reshape(B, MEM_LEN, MEM_W), mem_v.reshape(B, MEM_LEN, MEM_W), T, wl, tm=512)
    x2d = _ffn(x2d, wl, tm=512, tf=D_FF // 2)
    return x2d.reshape(B, T, D_MODEL), ka_out, va_out, gla_s, r3(lat), kr_out


def kernel(x_prompt, x_sample, mem_prompt, cache_sb_k, cache_sb_v, state_gla, cache_mla_latent,
           cache_mla_krope, cache_mem_k, cache_mem_v, g_mix_norm, w_in, w_gla_gate, b_gla_gate,
           g_gla_out, g_sb_out, g_cq, w_uq, g_qn, g_qr, g_kr, g_ckv, w_ukv, g_kn, g_mla_out, w_out,
           g_cross_norm, g_mem_norm, w_cq, w_ck, w_cv, g_cqn, g_ckn, w_co, g_ffn_norm, w_gate,
           w_up, w_down):
    p = dict(g_mix_norm=g_mix_norm, w_in=w_in, w_gla_gate=w_gla_gate, b_gla_gate=b_gla_gate,
             g_gla_out=g_gla_out, g_sb_out=g_sb_out, g_cq=g_cq, w_uq=w_uq, g_qn=g_qn, g_qr=g_qr,
             g_kr=g_kr, g_ckv=g_ckv, w_ukv=w_ukv, g_kn=g_kn, g_mla_out=g_mla_out, w_out=w_out,
             g_cross_norm=g_cross_norm, g_mem_norm=g_mem_norm, w_cq=w_cq, w_ck=w_ck, w_cv=w_cv,
             g_cqn=g_cqn, g_ckn=g_ckn, w_co=w_co, g_ffn_norm=g_ffn_norm, w_gate=w_gate, w_up=w_up,
             w_down=w_down)
    depth = w_in.shape[0]
    B = x_prompt.shape[0]
    xp, xs = x_prompt, x_sample
    outs_p = [[] for _ in range(7)]
    outs_s = [[] for _ in range(5)]
    caches = _time_minor_caches(cache_sb_k, cache_sb_v, cache_mla_latent, cache_mla_krope)
    for l in range(depth):
        wl = _prep_layer(p, l)
        mk, mv = _memkv(mem_prompt.reshape(B * MEM_LEN, D_MODEL), wl, tm=512)
        mk = mk.reshape(B, MEM_LEN, MEM_HEADS, MEM_HD)
        mv = mv.reshape(B, MEM_LEN, MEM_HEADS, MEM_HD)
        xp, ka, va, st, lat, kr = _trunk_layer(xp, None, l, None, mk, mv, wl)
        for lst, val in zip(outs_p, (ka, va, st, lat, kr, mk, mv)):
            lst.append(val)
        xs, ka, va, st, lat, kr = _trunk_layer(xs, caches, l, state_gla[l], cache_mem_k[l], cache_mem_v[l], wl)
        for lst, val in zip(outs_s, (ka, va, st, lat, kr)):
            lst.append(val)
    return (xp, xs, *[jnp.stack(v) for v in outs_p], *[jnp.stack(v) for v in outs_s])
```

```python
import functools
import math

import jax
import jax.numpy as jnp
import numpy as np
from jax import lax
from jax.experimental import pallas as pl
from jax.experimental.pallas import tpu as pltpu

F32 = jnp.float32
BF16 = jnp.bfloat16

D_MODEL = 1024
CHUNK = 64
SB_HEADS, SB_HD = 4, 64
SB_W = SB_HEADS * SB_HD
GLA_HEADS, GLA_DK, GLA_DV = 4, 32, 64
GLA_QK_W = GLA_HEADS * GLA_DK
GLA_W = GLA_HEADS * GLA_DV
GLA_GATE_RANK = 16
GLA_TAU = 16.0
GLA_BLOCK = 16
MLA_HEADS = 8
MLA_Q_LORA, MLA_KV_LORA = 256, 128
MLA_NOPE, MLA_ROPE, MLA_V = 64, 32, 64
MLA_W = MLA_HEADS * MLA_V
ROPE_THETA = 10000.0
MEM_LEN, MEM_HEADS, MEM_HD = 256, 4, 128
MEM_W = MEM_HEADS * MEM_HD
D_FF = 2816
NORM_EPS = 1e-6

LANES = 128
VMEM_LIMIT = 56 * 1024 * 1024
SB_SCALE = 1.0 / math.sqrt(SB_HD)
GLA_SCALE = GLA_DK ** -0.5
MLA_SCALE = (MLA_NOPE + MLA_ROPE) ** -0.5
MEM_SCALE = MEM_HD ** -0.5
LOG2E = math.log2(math.e)
SB_Q_SCALE = SB_SCALE * LOG2E
MLA_Q_SCALE2 = MLA_SCALE * LOG2E
SB_DEAD_LOG2 = -105.0 * LOG2E
MASK_NEG = -1e30


def _dot(a, b):
    return jnp.dot(a, b, preferred_element_type=F32)


def _dot_nt(a, b):
    return lax.dot_general(a, b, (((1,), (1,)), ((), ())), preferred_element_type=F32)


def _dot_tn(a, b):
    return lax.dot_general(a, b, (((0,), (0,)), ((), ())), preferred_element_type=F32)


def _split(x):
    hi = x.astype(BF16)
    lo = (x - hi.astype(F32)).astype(BF16)
    return hi, lo


def _dot_split_lhs(x, m):
    hi, lo = _split(x)
    return _dot(hi, m) + _dot(lo, m)


def _dot_split_rhs(m, x):
    hi, lo = _split(x)
    return _dot(m, hi) + _dot(m, lo)


def _rms(x, g):
    ms = jnp.mean(x * x, axis=-1, keepdims=True)
    return x * lax.rsqrt(ms + NORM_EPS) * g


def _seg_rms(x, g, seg, width):
    ss = _dot_split_lhs(x * x, seg)
    return x * lax.rsqrt(ss * (1.0 / width) + NORM_EPS) * g


def _softplus(z):
    return jnp.maximum(z, 0.0) + jnp.log1p(jnp.exp(-jnp.abs(z)))


def _softplus2(z2):
    return jnp.maximum(z2, 0.0) + jnp.log2(1.0 + jnp.exp2(-jnp.abs(z2)))


def _sigmoid(z):
    return 1.0 / (1.0 + jnp.exp(-z))


def _swap_halves(x, seg):
    n = x.shape[-1]
    half = seg // 2
    lane = lax.broadcasted_iota(jnp.int32, x.shape, x.ndim - 1)
    first = (lane & (seg - 1)) < half
    return jnp.where(first, pltpu.roll(x, n - half, x.ndim - 1), pltpu.roll(x, half, x.ndim - 1))


def _div_pow2(x, d):
    return lax.shift_right_logical(x, int(math.log2(d)))


def _seg_matrix(n, width):
    i = np.arange(n) // width
    return jnp.asarray(i[:, None] == i[None, :], dtype=BF16)


def _full(shape):
    nd = len(shape)
    return pl.BlockSpec(shape, lambda *_: (0,) * nd)


def _params(sem):
    return pltpu.CompilerParams(dimension_semantics=sem, vmem_limit_bytes=VMEM_LIMIT)


def _inproj_kernel(x_ref, gmix_ref, wsb_ref, wgl_ref, wag_ref, wml_ref, wgate_ref, bgate_ref,
                   gcq_ref, wuq_ref, gqn_ref, gqr_ref, gkr_ref, gckv_ref, seg64_ref, seg32_ref,
                   cos_ref, sin_ref,
                   qa_ref, ka_ref, va_ref, qg_ref, kg_ref, vg_ref, la_ref, rg_ref, *mla_refs, time_minor):
    if time_minor:
        qcat_ref, lat_ref, kr_ref = mla_refs
    else:
        qn_ref, qr_ref, lat_ref, kr_ref = mla_refs
    tm = x_ref.shape[0]
    h = _rms(x_ref[...], gmix_ref[...]).astype(BF16)

    ml = _dot(h, wml_ref[...])
    ag = _dot(h, wag_ref[...])
    sb = _dot(h, wsb_ref[...])
    gl = _dot(h, wgl_ref[...])
    cq = ml[:, :MLA_Q_LORA]
    q = _dot(_rms(cq, gcq_ref[...]).astype(BF16), wuq_ref[...])
    gate = _dot(ag.astype(BF16), wgate_ref[...]) + bgate_ref[...]

    qa_ref[...] = (sb[:, :SB_W] * SB_Q_SCALE).astype(BF16)
    ka = sb[:, SB_W:2 * SB_W]
    va = sb[:, 2 * SB_W:]
    if time_minor:
        ka_ref[...] = ka.T.reshape(SB_HEADS, SB_HD, tm)
        va_ref[...] = va.T.reshape(SB_HEADS, SB_HD, tm)
    else:
        ka_ref[...] = ka
        va_ref[...] = va

    qg_ref[...] = gl[:, :GLA_QK_W]
    kg_ref[...] = gl[:, GLA_QK_W:2 * GLA_QK_W]
    vg_ref[...] = gl[:, 2 * GLA_QK_W:2 * GLA_QK_W + GLA_W]
    rg_ref[...] = gl[:, 2 * GLA_QK_W + GLA_W:]
    la_ref[...] = -_softplus(-gate) * (1.0 / GLA_TAU)

    ckv = ml[:, MLA_Q_LORA:MLA_Q_LORA + MLA_KV_LORA]
    krp = ml[:, MLA_Q_LORA + MLA_KV_LORA:]
    lat_ref[...] = _rms(ckv, gckv_ref[...])
    cos = cos_ref[...]
    sin = sin_ref[...]
    kr_ms = jnp.sum(krp * krp, axis=-1, keepdims=True) * (1.0 / MLA_ROPE)
    krn = krp * lax.rsqrt(kr_ms + NORM_EPS) * gkr_ref[...]
    kr_rot = krn * cos[:, :LANES] + _swap_halves(krn, MLA_ROPE) * sin[:, :LANES]
    if time_minor:
        kr_ref[...] = kr_rot.T[:MLA_ROPE, :]
    else:
        kr_ref[...] = kr_rot[:, :MLA_ROPE]

    nope_w = MLA_HEADS * MLA_NOPE
    q_scale = MLA_Q_SCALE2 if time_minor else MLA_SCALE
    qn = _seg_rms(q[:, :nope_w], gqn_ref[...], seg64_ref[...], MLA_NOPE) * q_scale
    qr = _seg_rms(q[:, nope_w:], gqr_ref[...], seg32_ref[...], MLA_ROPE)
    qr = (qr * cos + _swap_halves(qr, MLA_ROPE) * sin) * q_scale
    if not time_minor:
        qn_ref[...] = qn.astype(BF16)
        qr_ref[...] = qr.astype(BF16)
        return
    lane = lax.broadcasted_iota(jnp.int32, (1, LANES), 1)
    rope_heads_per_tile = LANES // MLA_ROPE
    for hd in range(MLA_HEADS):
        pair = qn[:, LANES * (hd // 2):LANES * (hd // 2 + 1)]
        nope = pair if hd % 2 == 0 else pltpu.roll(pair, MLA_NOPE, 1)
        tile = qr[:, LANES * (hd // rope_heads_per_tile):LANES * (hd // rope_heads_per_tile + 1)]
        shift = (MLA_NOPE - MLA_ROPE * (hd % rope_heads_per_tile)) % LANES
        rope = tile if shift == 0 else pltpu.roll(tile, shift, 1)
        qcat_ref[hd] = jnp.where(lane < MLA_NOPE, nope,
                                 jnp.where(lane < MLA_NOPE + MLA_ROPE, rope, 0.0)).astype(BF16)


def _rope_tables(pos):
    half = MLA_ROPE // 2
    freqs = ROPE_THETA ** (-jnp.arange(half, dtype=F32) / half)
    ang = pos.astype(F32)[:, None] * freqs[None, :]
    cos = jnp.cos(ang)
    sin = jnp.sin(ang)
    cos_t = jnp.tile(jnp.concatenate([cos, cos], axis=-1), (1, MLA_HEADS))
    sin_t = jnp.tile(jnp.concatenate([-sin, sin], axis=-1), (1, MLA_HEADS))
    return cos_t, sin_t


def _inproj(x2d, B, T, past, wl, tm, time_minor):
    n = x2d.shape[0]
    tm = min(tm, n)
    cos_t, sin_t = _rope_tables(past + jnp.arange(T, dtype=jnp.int32))
    if T < tm:
        cos_t = jnp.tile(cos_t, (tm // T, 1))
        sin_t = jnp.tile(sin_t, (tm // T, 1))
    n_tab = cos_t.shape[0] // tm
    row = lambda w: pl.BlockSpec((tm, w), lambda i: (i, 0))
    tab = pl.BlockSpec((tm, 2 * LANES), lambda i: (i % n_tab, 0))
    weights = (wl['g_mix'], wl['w_sb'], wl['w_gl'], wl['w_ag'], wl['w_ml'], wl['w_gate'], wl['b_gate'],
               wl['g_cq'], wl['w_uq'], wl['g_qn'], wl['g_qr'], wl['g_kr'], wl['g_ckv'],
               wl['seg64_512'], wl['seg32_256'])
    rowout = lambda w, dt: (row(w), jax.ShapeDtypeStruct((n, w), dt))
    if time_minor:
        assert T % tm == 0
        tpb = T // tm
        kv_out = (pl.BlockSpec((None, SB_HEADS, SB_HD, tm), lambda i: (i // tpb, 0, 0, i % tpb)),
                  jax.ShapeDtypeStruct((B, SB_HEADS, SB_HD, T), F32))
        kr_out = (pl.BlockSpec((None, MLA_ROPE, tm), lambda i: (i // tpb, 0, i % tpb)),
                  jax.ShapeDtypeStruct((B, MLA_ROPE, T), F32))
        q_out = [(pl.BlockSpec((None, MLA_HEADS, tm, LANES), lambda i: (i // tpb, 0, i % tpb, 0)),
                  jax.ShapeDtypeStruct((B, MLA_HEADS, T, LANES), BF16))]
    else:
        kv_out = rowout(SB_W, F32)
        kr_out = rowout(MLA_ROPE, F32)
        q_out = [rowout(MLA_HEADS * MLA_NOPE, BF16), rowout(MLA_HEADS * MLA_ROPE, BF16)]
    outs = [rowout(SB_W, BF16), kv_out, kv_out,
            rowout(GLA_QK_W, F32), rowout(GLA_QK_W, F32), rowout(GLA_W, F32), rowout(GLA_QK_W, F32),
            rowout(GLA_W, F32), *q_out, rowout(MLA_KV_LORA, F32), kr_out]
    return pl.pallas_call(
        functools.partial(_inproj_kernel, time_minor=time_minor),
        grid=(n // tm,),
        in_specs=[row(D_MODEL)] + [_full(w.shape) for w in weights] + [tab, tab],
        out_specs=[o[0] for o in outs],
        out_shape=[o[1] for o in outs],
        compiler_params=_params(("parallel",)),
        name="inproj",
    )(x2d, *weights, cos_t, sin_t)


def _sb_kernel(*refs, tq, tk, has_past, n_far_blocks, layer):
    if has_past:
        (q_ref, kn_ref, vn_ref, kt_ref, vt_ref, kt_all_ref, vt_all_ref, g_ref, seg_ref, o_ref,
         kbuf_ref, vbuf_ref) = refs
    else:
        q_ref, kt_ref, vt_ref, g_ref, seg_ref, o_ref = refs
    qi = pl.program_id(1)
    q = q_ref[...]
    lane = lax.broadcasted_iota(jnp.int32, (1, SB_W), 1)
    hmask = [(lane >= SB_HD * h) & (lane < SB_HD * (h + 1)) for h in range(SB_HEADS)]
    qh = [jnp.where(hmask[h], q, jnp.zeros_like(q)) for h in range(SB_HEADS)]

    def suffix_matrix(n):
        return (lax.broadcasted_iota(jnp.int32, (n, n), 0)
                > lax.broadcasted_iota(jnp.int32, (n, n), 1)).astype(BF16)

    def time_minor_block(ref, start, n):
        return ref[:, :, pl.ds(start, n)].reshape(SB_W, n).astype(BF16)

    def block(kblk, vblk, time_minor, vis, umat, cs, acc):
        heads = range(SB_HEADS)
        zs = [_dot(qh[h], kblk) if time_minor else _dot_nt(qh[h], kblk) for h in heads]
        lks = [-_softplus2(z) for z in zs]
        if vis is not None:
            lks = [jnp.where(vis, lk, 0.0) for lk in lks]
        splits = [_split(lk) for lk in lks]
        laters = [_dot(hi, umat) + _dot(lo, umat) for hi, lo in splits]
        ws = [jnp.exp2(zs[h] + lks[h] + laters[h] + cs[h]) for h in heads]
        if vis is not None:
            ws = [jnp.where(vis, w, 0.0) for w in ws]
        ws = [w.astype(BF16) for w in ws]
        pvs = [_dot_nt(w, vblk) if time_minor else _dot(w, vblk) for w in ws]
        for h in heads:
            acc = acc + jnp.where(hmask[h], pvs[h], 0.0)
        new_cs = [cs[h] + laters[h][:, :1] + lks[h][:, :1] for h in heads]
        return new_cs, acc

    def dead(cs):
        m = cs[0]
        for c in cs[1:]:
            m = jnp.maximum(m, c)
        return jnp.max(m)

    vis = (lax.broadcasted_iota(jnp.int32, (tq, tq), 1)
           < lax.broadcasted_iota(jnp.int32, (tq, tq), 0))
    zero_c = jnp.zeros((tq, 1), F32)
    init = ([zero_c] * SB_HEADS, jnp.zeros((tq, SB_W), F32))
    if has_past:
        cs, acc = block(kn_ref[...].astype(BF16), vn_ref[...].astype(BF16), False, vis,
                        suffix_matrix(tq), *init)
        kb0 = jnp.int32(kt_ref.shape[-1] // tk - 1)
    else:
        q0 = pl.multiple_of(qi * tq, tq)
        cs, acc = block(time_minor_block(kt_ref, q0, tq), time_minor_block(vt_ref, q0, tq), True, vis,
                        suffix_matrix(tq), *init)
        kb0 = qi - 1

    umat = suffix_matrix(tk)

    def walk(fetch, kb0, cs, acc):
        def cond(st):
            return jnp.logical_and(st[0] >= 0, st[1] > SB_DEAD_LOG2)

        def body(st):
            kb = st[0]
            cs, acc = list(st[2:2 + SB_HEADS]), st[2 + SB_HEADS]
            kblk, vblk = fetch(pl.multiple_of(kb * tk, tk))
            cs, acc = block(kblk, vblk, True, None, umat, cs, acc)
            return (kb - 1, dead(cs), *cs, acc)

        st = lax.while_loop(cond, body, (kb0, dead(cs), *cs, acc))
        return list(st[2:2 + SB_HEADS]), st[2 + SB_HEADS]

    cs, acc = walk(lambda start: (time_minor_block(kt_ref, start, tk), time_minor_block(vt_ref, start, tk)),
                   kb0, cs, acc)
    if has_past and n_far_blocks:
        b = pl.program_id(0)

        def fetch_far(start):
            pltpu.sync_copy(kt_all_ref.at[layer, b, :, :, pl.ds(start, tk)], kbuf_ref)
            pltpu.sync_copy(vt_all_ref.at[layer, b, :, :, pl.ds(start, tk)], vbuf_ref)
            return (kbuf_ref[...].reshape(SB_W, tk).astype(BF16), vbuf_ref[...].reshape(SB_W, tk).astype(BF16))

        cs, acc = walk(fetch_far, jnp.int32(n_far_blocks - 1), cs, acc)
    o_ref[...] = _seg_rms(acc, g_ref[...], seg_ref[...], SB_HD).astype(BF16)


def _sb_attn(q, k_new, v_new, kt_cache, vt_cache, layer, wl, tq, tk, window):
    B, T, _ = q.shape
    tq = min(tq, T)
    has_past = kt_cache is not None
    n_far_blocks = 0
    scratch = []
    in_specs = [pl.BlockSpec((None, tq, SB_W), lambda b, i: (b, i, 0))]
    if has_past:
        P = kt_cache.shape[-1]
        window = min(window, P)
        assert tq == T and P % window == 0 and window % tk == 0
        n_far_blocks = (P - window) // tk
        new = pl.BlockSpec((None, T, SB_W), lambda b, i: (b, 0, 0))
        near = pl.BlockSpec((None, None, SB_HEADS, SB_HD, window), lambda b, i: (layer, b, 0, 0, P // window - 1))
        hbm = pl.BlockSpec(memory_space=pl.ANY)
        in_specs += [new, new, near, near, hbm, hbm]
        args = [q, k_new, v_new, kt_cache, vt_cache, kt_cache, vt_cache]
        scratch = [pltpu.VMEM((SB_HEADS, SB_HD, tk), F32)] * 2
    else:
        assert tq == tk
        new = pl.BlockSpec((None, SB_HEADS, SB_HD, T), lambda b, i: (b, 0, 0, 0))
        in_specs += [new, new]
        args = [q, k_new, v_new]
    in_specs += [_full((1, SB_W)), _full((SB_W, SB_W))]
    args += [wl['g_sb_out'], wl['seg64_256']]
    return pl.pallas_call(
        functools.partial(_sb_kernel, tq=tq, tk=tk, has_past=has_past, n_far_blocks=n_far_blocks, layer=layer),
        grid=(B, T // tq),
        in_specs=in_specs,
        out_specs=pl.BlockSpec((None, tq, SB_W), lambda b, i: (b, i, 0)),
        out_shape=jax.ShapeDtypeStruct((B, T, SB_W), BF16),
        scratch_shapes=scratch,
        compiler_params=_params(("parallel", "parallel")),
        name="sb_attn",
    )(*args)


def _gla_kernel(q_ref, k_ref, v_ref, la_ref, rg_ref, s0_ref, g_ref, seg_ref, o_ref, s_ref, *, tc, nb):
    ci = pl.program_id(1)

    @pl.when(ci == 0)
    def _():
        s_ref[...] = s0_ref[...]

    nsub = tc // GLA_BLOCK
    r_i = lax.broadcasted_iota(jnp.int32, (tc, tc), 0)
    c_i = lax.broadcasted_iota(jnp.int32, (tc, tc), 1)
    same = _div_pow2(r_i, GLA_BLOCK) == _div_pow2(c_i, GLA_BLOCK)
    causal = jnp.logical_and(same, c_i <= r_i)
    causal_b = causal.astype(BF16)
    same_b = same.astype(BF16)
    lane_k = lax.broadcasted_iota(jnp.int32, (1, GLA_QK_W), 1)
    lane_v = lax.broadcasted_iota(jnp.int32, (1, GLA_W), 1)
    kms = [(lane_k >= GLA_DK * h) & (lane_k < GLA_DK * (h + 1)) for h in range(GLA_HEADS)]
    vms = [(lane_v >= GLA_DV * h) & (lane_v < GLA_DV * (h + 1)) for h in range(GLA_HEADS)]
    sr = _div_pow2(lax.broadcasted_iota(jnp.int32, (GLA_QK_W, GLA_W), 0), GLA_DK)
    sc = _div_pow2(lax.broadcasted_iota(jnp.int32, (GLA_QK_W, GLA_W), 1), GLA_DV)
    bd = sr == sc
    eye = (lax.broadcasted_iota(jnp.int32, (GLA_QK_W, GLA_QK_W), 0)
           == lax.broadcasted_iota(jnp.int32, (GLA_QK_W, GLA_QK_W), 1))
    g = g_ref[...]
    seg = seg_ref[...]

    rows_b = range(nb)
    heads = range(GLA_HEADS)
    las = [_split(la_ref[bi]) for bi in rows_b]
    bs = [_dot(causal_b, hi) + _dot(causal_b, lo) for hi, lo in las]
    bls = [_dot(same_b, hi) + _dot(same_b, lo) for hi, lo in las]
    qts = [(q_ref[bi] * jnp.exp(bs[bi]) * GLA_SCALE).astype(BF16) for bi in rows_b]
    kts = [(k_ref[bi] * jnp.exp(-bs[bi])).astype(BF16) for bi in rows_b]
    kds = [(k_ref[bi] * jnp.exp(bls[bi] - bs[bi])).astype(BF16) for bi in rows_b]
    vbs = [v_ref[bi].astype(BF16) for bi in rows_b]
    decays = [jnp.exp(bl) for bl in bls]

    atts = [[_dot_nt(jnp.where(kms[h], qts[bi], jnp.zeros_like(qts[bi])), kts[bi]) for h in heads]
            for bi in rows_b]
    atts = [[jnp.where(causal, a, 0.0).astype(BF16) for a in row] for row in atts]
    pvs = [[_dot(atts[bi][h], vbs[bi]) for h in heads] for bi in rows_b]
    intras = []
    for bi in rows_b:
        intra = jnp.zeros((tc, GLA_W), F32)
        for h in heads:
            intra = intra + jnp.where(vms[h], pvs[bi][h], 0.0)
        intras.append(intra)

    states = [s_ref[bi] for bi in rows_b]
    inters = [[] for _ in rows_b]
    for j in range(nsub):
        rows = slice(j * GLA_BLOCK, (j + 1) * GLA_BLOCK)
        for bi in rows_b:
            inters[bi].append(_dot(qts[bi][rows], states[bi].astype(BF16)))
        upds = [_dot_tn(kds[bi][rows], vbs[bi][rows]) for bi in rows_b]
        for bi in rows_b:
            d_row = decays[bi][j * GLA_BLOCK:j * GLA_BLOCK + 1]
            d_col = jnp.sum(jnp.where(eye, d_row, 0.0), axis=1, keepdims=True)
            states[bi] = d_col * states[bi] + jnp.where(bd, upds[bi], 0.0)
    os_ = [intras[bi] + jnp.concatenate(inters[bi], axis=0) for bi in rows_b]
    splits = [_split(o * o) for o in os_]
    sss = [_dot(hi, seg) + _dot(lo, seg) for hi, lo in splits]
    for bi in rows_b:
        s_ref[bi] = states[bi]
        o = os_[bi] * lax.rsqrt(sss[bi] * (1.0 / GLA_DV) + NORM_EPS) * g
        rg = rg_ref[bi]
        o_ref[bi] = (o * (rg * _sigmoid(rg))).astype(BF16)


def _gla(qg, kg, vg, la, rg, s0_bd, wl, tc, nb):
    B, T, _ = qg.shape
    tc = min(tc, T)
    nb = min(nb, B)
    blk = lambda w: pl.BlockSpec((nb, tc, w), lambda b, c: (b, c, 0))
    st = pl.BlockSpec((nb, GLA_QK_W, GLA_W), lambda b, c: (b, 0, 0))
    return pl.pallas_call(
        functools.partial(_gla_kernel, tc=tc, nb=nb),
        grid=(B // nb, T // tc),
        in_specs=[blk(GLA_QK_W), blk(GLA_QK_W), blk(GLA_W), blk(GLA_QK_W), blk(GLA_W), st,
                  _full((1, GLA_W)), _full((GLA_W, GLA_W))],
        out_specs=[blk(GLA_W), st],
        out_shape=[jax.ShapeDtypeStruct((B, T, GLA_W), BF16),
                   jax.ShapeDtypeStruct((B, GLA_QK_W, GLA_W), F32)],
        compiler_params=_params(("parallel", "arbitrary")),
        name="gla",
    )(qg, kg, vg, la, rg, s0_bd, wl['g_gla_out'], wl['seg64_256'])


def _state_to_blockdiag(s):
    B = s.shape[0]
    eye = jnp.eye(GLA_HEADS, dtype=s.dtype)
    return jnp.einsum('bhkv,hg->bhkgv', s, eye).reshape(B, GLA_QK_W, GLA_W)


def _state_from_blockdiag(s):
    B = s.shape[0]
    s5 = s.reshape(B, GLA_HEADS, GLA_DK, GLA_HEADS, GLA_DV)
    return jnp.stack([s5[:, h, :, h, :] for h in range(GLA_HEADS)], axis=1)


def _mla_kv_kernel(lat_ref, kr_ref, wn_ref, wvt_ref, gkn_ref, seg_ref, place_ref, kcat_ref, vt_ref):
    tr = lat_ref.shape[0]
    lat = lat_ref[...].astype(BF16)
    kn = _seg_rms(_dot(lat, wn_ref[...]), gkn_ref[...], seg_ref[...], MLA_NOPE)
    vt_ref[...] = _dot_nt(wvt_ref[...], lat).reshape(MLA_HEADS, MLA_V, tr).astype(BF16)
    krp = _dot_tn(kr_ref[...].astype(BF16), place_ref[...])
    lane = lax.broadcasted_iota(jnp.int32, (1, LANES), 1)
    for p in range(MLA_HEADS // 2):
        pair = kn[:, LANES * p:LANES * (p + 1)]
        kcat_ref[2 * p] = jnp.where(lane < MLA_NOPE, pair, krp).astype(BF16)
        kcat_ref[2 * p + 1] = jnp.where(lane < MLA_NOPE, pltpu.roll(pair, MLA_NOPE, 1), krp).astype(BF16)


def _mla_kv(lat, kr_t, wl, tr):
    B, L, _ = lat.shape
    tr = min(tr, L)
    return pl.pallas_call(
        _mla_kv_kernel,
        grid=(B, L // tr),
        in_specs=[pl.BlockSpec((None, tr, MLA_KV_LORA), lambda b, i: (b, i, 0)),
                  pl.BlockSpec((None, MLA_ROPE, tr), lambda b, i: (b, 0, i)),
                  _full(wl['w_ukn'].shape), _full(wl['w_ukv_v_t'].shape), _full((1, MLA_HEADS * MLA_NOPE)),
                  _full(wl['seg64_512'].shape), _full((MLA_ROPE, LANES))],
        out_specs=[pl.BlockSpec((None, MLA_HEADS, tr, LANES), lambda b, i: (b, 0, i, 0)),
                   pl.BlockSpec((None, MLA_HEADS, MLA_V, tr), lambda b, i: (b, 0, 0, i))],
        out_shape=[jax.ShapeDtypeStruct((B, MLA_HEADS, L, LANES), BF16),
                   jax.ShapeDtypeStruct((B, MLA_HEADS, MLA_V, L), BF16)],
        compiler_params=_params(("parallel", "parallel")),
        name="mla_kv",
    )(lat, kr_t, wl['w_ukn'], wl['w_ukv_v_t'], wl['g_kn'], wl['seg64_512'], wl['rope_place'])


def _mla_attn_kernel(q_ref, k_ref, vt_ref, g_ref, o_ref, *, tq):
    qi = pl.program_id(1)
    q0 = pl.multiple_of(qi * tq, tq)
    vis = (_div_pow2(lax.broadcasted_iota(jnp.int32, (tq, tq), 0), CHUNK)
           <= _div_pow2(lax.broadcasted_iota(jnp.int32, (tq, tq), 1), CHUNK))
    qs = [q_ref[h] for h in range(MLA_HEADS)]

    heads = range(MLA_HEADS)

    def step(start, width, st, mask):
        ss = [_dot_nt(k_ref[h, pl.ds(start, width), :], qs[h]) for h in heads]
        if mask:
            ss = [jnp.where(vis, s, MASK_NEG) for s in ss]
        out = []
        for h in heads:
            bm = jnp.max(ss[h], axis=0, keepdims=True)
            m_new = bm if st is None else jnp.maximum(st[3 * h], bm)
            p = jnp.exp2(ss[h] - m_new)
            l_blk = jnp.sum(p, axis=0, keepdims=True)
            pv = _dot(vt_ref[h, :, pl.ds(start, width)], p.astype(BF16))
            if st is None:
                out += [m_new, l_blk, pv]
            else:
                a = jnp.exp2(st[3 * h] - m_new)
                out += [m_new, a * st[3 * h + 1] + l_blk, a * st[3 * h + 2] + pv]
        return tuple(out)

    st = step(q0, tq, None, True)
    st = lax.fori_loop(0, lax.shift_right_logical(qi, 1),
                       lambda kb, st: step(pl.multiple_of(kb * (2 * tq), 2 * tq), 2 * tq, st, False), st)
    st = lax.cond((qi & 1) == 1, lambda st: step(pl.multiple_of((qi - 1) * tq, tq), tq, st, False),
                  lambda st: st, st)
    g = g_ref[...]
    outs = []
    for h in heads:
        o = st[3 * h + 2] / st[3 * h + 1]
        ms = jnp.mean(o * o, axis=0, keepdims=True)
        outs.append(o * lax.rsqrt(ms + NORM_EPS) * g)
    o_ref[...] = jnp.concatenate(outs, axis=0).T.astype(BF16)


def _mla_attn(qcat, kcat, v_t, wl, tq):
    B, _, T, _ = qcat.shape
    tq = min(tq, T)
    return pl.pallas_call(
        functools.partial(_mla_attn_kernel, tq=tq),
        grid=(B, T // tq),
        in_specs=[pl.BlockSpec((None, MLA_HEADS, tq, LANES), lambda b, i: (b, 0, i, 0)),
                  pl.BlockSpec((None, MLA_HEADS, T, LANES), lambda b, i: (b, 0, 0, 0)),
                  pl.BlockSpec((None, MLA_HEADS, MLA_V, T), lambda b, i: (b, 0, 0, 0)),
                  _full((MLA_V, 1))],
        out_specs=pl.BlockSpec((None, tq, MLA_W), lambda b, i: (b, i, 0)),
        out_shape=jax.ShapeDtypeStruct((B, T, MLA_W), BF16),
        compiler_params=_params(("parallel", "parallel")),
        name="mla_attn",
    )(qcat, kcat, v_t, wl['g_mla_out_col'])


def _mla_dec_kernel(qn_ref, qr_ref, lat_ref, krt_ref, latn_ref, krn_ref, wknt_ref, wv_ref, gkn_ref,
                    g_ref, seg_ref, o_ref, s_ref, *, kc):
    nb, P = lat_ref.shape[:2]
    T = qn_ref.shape[0] // nb
    H = MLA_HEADS
    wknt = wknt_ref[...]
    gkn = gkn_ref[...]

    def queries(b):
        rows = slice(b * T, (b + 1) * T)
        qn = (qn_ref[rows, :].astype(F32) * gkn).astype(BF16)
        qabs = jnp.concatenate(
            [_dot(qn[:, MLA_NOPE * h:MLA_NOPE * (h + 1)], wknt[MLA_NOPE * h:MLA_NOPE * (h + 1), :])
             for h in range(H)], axis=0).astype(BF16)
        qr = qr_ref[rows, :]
        qrs = jnp.concatenate([qr[:, MLA_ROPE * h:MLA_ROPE * (h + 1)] for h in range(H)], axis=0)
        return qabs, qrs

    def scores(qabs, lat_b, sr, n):
        kn_t = _dot_nt(wknt, lat_b)
        ss = jnp.sum((kn_t * kn_t).reshape(H, MLA_NOPE, n), axis=1)
        r = lax.rsqrt(ss * (1.0 / MLA_NOPE) + NORM_EPS)
        sn = _dot_nt(qabs, lat_b).reshape(H, T, n)
        return (sn * r[:, None, :]).reshape(H * T, n) + sr

    def all_scores(b, qabs, qrs):
        m = None
        for c in range(P // kc):
            cols = slice(c * kc, (c + 1) * kc)
            s = scores(qabs, lat_ref[b, cols, :].astype(BF16), _dot(qrs, krt_ref[b, :, cols].astype(BF16)), kc)
            s_ref[b, :, cols] = s
            mc = jnp.max(s, axis=-1, keepdims=True)
            m = mc if m is None else jnp.maximum(m, mc)
        rows = slice(b * T, (b + 1) * T)
        s_new = scores(qabs, latn_ref[rows, :].astype(BF16), _dot_nt(qrs, krn_ref[rows, :].astype(BF16)), T)
        return jnp.maximum(m, jnp.max(s_new, axis=-1, keepdims=True)), s_new

    def context(b, m, s_new):
        p = jnp.exp(s_new - m)
        l = jnp.sum(p, axis=-1, keepdims=True)
        ctx = _dot(p.astype(BF16), latn_ref[b * T:(b + 1) * T, :].astype(BF16))
        for c in range(P // kc):
            cols = slice(c * kc, (c + 1) * kc)
            p = jnp.exp(s_ref[b, :, cols] - m)
            l = l + jnp.sum(p, axis=-1, keepdims=True)
            ctx = ctx + _dot(p.astype(BF16), lat_ref[b, cols, :].astype(BF16))
        return (ctx / l).astype(BF16)

    qs = [queries(b) for b in range(nb)]
    stats = [all_scores(b, *qs[b]) for b in range(nb)]
    ctxs = [context(b, *stats[b]) for b in range(nb)]

    wv = wv_ref[...]
    lane = lax.broadcasted_iota(jnp.int32, (1, MLA_W), 1)
    for b in range(nb):
        o = jnp.zeros((T, MLA_W), F32)
        for h in range(H):
            oh = _dot(ctxs[b][T * h:T * (h + 1)], wv)
            o = jnp.where((lane >= MLA_V * h) & (lane < MLA_V * (h + 1)), oh, o)
        o_ref[b * T:(b + 1) * T, :] = _seg_rms(o, g_ref[...], seg_ref[...], MLA_V).astype(BF16)


def _mla_dec(qn, qr, lat_cache, kr_t_cache, layer, lat_new, kr_new, T, wl, kc, nb):
    n = qn.shape[0]
    B = n // T
    P = lat_cache.shape[2]
    row = lambda w: pl.BlockSpec((nb * T, w), lambda b: (b, 0))
    return pl.pallas_call(
        functools.partial(_mla_dec_kernel, kc=kc),
        grid=(B // nb,),
        in_specs=[row(MLA_HEADS * MLA_NOPE), row(MLA_HEADS * MLA_ROPE),
                  pl.BlockSpec((None, nb, P, MLA_KV_LORA), lambda b: (layer, b, 0, 0)),
                  pl.BlockSpec((None, nb, MLA_ROPE, P), lambda b: (layer, b, 0, 0)),
                  row(MLA_KV_LORA), row(MLA_ROPE),
                  _full(wl['w_ukn_t'].shape), _full(wl['w_ukv_v'].shape), _full((1, MLA_HEADS * MLA_NOPE)),
                  _full((1, MLA_W)), _full((MLA_W, MLA_W))],
        out_specs=row(MLA_W),
        out_shape=jax.ShapeDtypeStruct((n, MLA_W), BF16),
        scratch_shapes=[pltpu.VMEM((nb, MLA_HEADS * T, P), F32)],
        compiler_params=_params(("parallel",)),
        name="mla_dec",
    )(qn, qr, lat_cache, kr_t_cache, lat_new, kr_new, wl['w_ukn_t'], wl['w_ukv_v'], wl['g_kn'],
      wl['g_mla_out_512'], wl['seg64_512'])


def _post_kernel(x_ref, oa_ref, ob_ref, oc_ref, mk_ref, mv_ref, wo_ref, gcross_ref, wcq_ref, gcqn_ref,
                 wco_ref, o_ref, att_ref, *, nb, tt):
    wo = wo_ref
    mix = (_dot(oa_ref[...], wo[:SB_W, :]) + _dot(ob_ref[...], wo[SB_W:SB_W + GLA_W, :])
           + _dot(oc_ref[...], wo[SB_W + GLA_W:, :]))
    x1 = x_ref[...] + mix
    h = _rms(x1, gcross_ref[...]).astype(BF16)
    q = _dot(h, wcq_ref[...])
    gq = gcqn_ref[...]
    qs = []
    for hd in range(MEM_HEADS):
        qh = _rms(q[:, MEM_HD * hd:MEM_HD * (hd + 1)], gq).astype(BF16)
        qs.append(qh)
    pairs = [(bi, hd) for bi in range(nb) for hd in range(MEM_HEADS)]
    rows = lambda bi: slice(bi * tt, (bi + 1) * tt)
    cols = lambda hd: slice(MEM_HD * hd, MEM_HD * (hd + 1))
    ss = [_dot_nt(qs[hd][rows(bi)], mk_ref[bi, :, cols(hd)].astype(BF16)) * MEM_SCALE for bi, hd in pairs]
    ps, ls = [], []
    for s in ss:
        p = jnp.exp(s - jnp.max(s, axis=-1, keepdims=True))
        ls.append(jnp.sum(p, axis=-1, keepdims=True))
        ps.append(p.astype(BF16))
    pvs = [_dot(ps[i], mv_ref[bi, :, cols(hd)].astype(BF16)) for i, (bi, hd) in enumerate(pairs)]
    for i, (bi, hd) in enumerate(pairs):
        att_ref[rows(bi), cols(hd)] = pvs[i] / ls[i]
    o_ref[...] = x1 + _dot(att_ref[...].astype(BF16), wco_ref[...])


def _post(x2d, oa, ob, oc, mem_k, mem_v, T, wl, tm):
    n = x2d.shape[0]
    tm = min(tm, n)
    tt = min(T, tm)
    nb = tm // tt
    tiles_per_b = T // tt
    row = lambda w: pl.BlockSpec((tm, w), lambda i: (i, 0))
    mem = pl.BlockSpec((nb, MEM_LEN, MEM_W), lambda i: (i // tiles_per_b if nb == 1 else i, 0, 0))
    weights = (wl['w_out'], wl['g_cross'], wl['w_cq'], wl['g_cqn'], wl['w_co'])
    return pl.pallas_call(
        functools.partial(_post_kernel, nb=nb, tt=tt),
        grid=(n // tm,),
        in_specs=[row(D_MODEL), row(SB_W), row(GLA_W), row(MLA_W), mem, mem] + [_full(w.shape) for w in weights],
        out_specs=row(D_MODEL),
        out_shape=jax.ShapeDtypeStruct((n, D_MODEL), F32),
        scratch_shapes=[pltpu.VMEM((tm, MEM_W), F32)],
        compiler_params=_params(("parallel",)),
        name="post",
    )(x2d, oa, ob, oc, mem_k, mem_v, *weights)


def _ffn_kernel(x_ref, g_ref, wg_ref, wu_ref, wd_ref, o_ref, *, n_chunks):
    x = x_ref[...]
    h = _rms(x, g_ref[...]).astype(BF16)
    tf = wg_ref.shape[1] // n_chunks
    acc = x
    for c in range(n_chunks):
        cols = slice(c * tf, (c + 1) * tf)
        gate = _dot(h, wg_ref[:, cols])
        up = _dot(h, wu_ref[:, cols])
        act = (gate * _sigmoid(gate) * up).astype(BF16)
        acc = acc + _dot(act, wd_ref[cols, :])
    o_ref[...] = acc


def _ffn(x2d, wl, tm, n_chunks):
    n = x2d.shape[0]
    tm = min(tm, n)
    resident = lambda shape: pl.BlockSpec(shape, lambda i: (0, 0), pipeline_mode=pl.Buffered(1))
    return pl.pallas_call(
        functools.partial(_ffn_kernel, n_chunks=n_chunks),
        grid=(n // tm,),
        in_specs=[pl.BlockSpec((tm, D_MODEL), lambda i: (i, 0)), _full((1, D_MODEL)),
                  resident((D_MODEL, D_FF)), resident((D_MODEL, D_FF)), resident((D_FF, D_MODEL))],
        out_specs=pl.BlockSpec((tm, D_MODEL), lambda i: (i, 0)),
        out_shape=jax.ShapeDtypeStruct((n, D_MODEL), F32),
        compiler_params=_params(("parallel",)),
        name="ffn",
    )(x2d, wl['g_ffn'], wl['w_gate_ffn'], wl['w_up'], wl['w_down'])


def _memkv_kernel(m_ref, g_ref, wk_ref, wv_ref, gk_ref, k_ref, v_ref):
    h = _rms(m_ref[...], g_ref[...]).astype(BF16)
    k = _dot(h, wk_ref[...])
    gk = gk_ref[...]
    for hd in range(MEM_HEADS):
        cols = slice(MEM_HD * hd, MEM_HD * (hd + 1))
        k_ref[:, cols] = _rms(k[:, cols], gk)
    v_ref[...] = _dot(h, wv_ref[...])


def _memkv(mem2d, wl, tm):
    n = mem2d.shape[0]
    tm = min(tm, n)
    row = lambda w: pl.BlockSpec((tm, w), lambda i: (i, 0))
    weights = (wl['g_mem'], wl['w_ck'], wl['w_cv'], wl['g_ckn'])
    return pl.pallas_call(
        _memkv_kernel,
        grid=(n // tm,),
        in_specs=[row(D_MODEL)] + [_full(w.shape) for w in weights],
        out_specs=[row(MEM_W), row(MEM_W)],
        out_shape=[jax.ShapeDtypeStruct((n, MEM_W), F32)] * 2,
        compiler_params=_params(("parallel",)),
        name="memkv",
    )(mem2d, *weights)


def _prep_layer(p, l):
    w_in = p['w_in'][l]
    o = np.cumsum([0, SB_W, SB_W, SB_W, GLA_QK_W, GLA_QK_W, GLA_W, GLA_GATE_RANK, GLA_W,
                   MLA_Q_LORA, MLA_KV_LORA, MLA_ROPE])
    seg = lambda i, j: w_in[:, o[i]:o[j]]
    pad_cols = lambda w, n: jnp.pad(w, ((0, 0), (0, n - w.shape[1])))
    row = lambda g, reps=1: jnp.tile(g, reps)[None, :].astype(F32)
    w_uq = p['w_uq'][l].reshape(MLA_Q_LORA, MLA_HEADS, MLA_NOPE + MLA_ROPE)
    w_ukv = p['w_ukv'][l].reshape(MLA_KV_LORA, MLA_HEADS, MLA_NOPE + MLA_V)
    place = np.zeros((MLA_ROPE, LANES), np.float32)
    place[np.arange(MLA_ROPE), MLA_NOPE + np.arange(MLA_ROPE)] = 1.0
    return {
        'g_mix': row(p['g_mix_norm'][l]),
        'w_sb': seg(0, 3).astype(BF16),
        'w_gl': jnp.concatenate([seg(3, 6), seg(7, 8)], axis=1).astype(BF16),
        'w_ag': pad_cols(seg(6, 7), LANES).astype(BF16),
        'w_ml': pad_cols(seg(8, 11), 4 * LANES).astype(BF16),
        'w_gate': jnp.pad(p['w_gla_gate'][l], ((0, LANES - GLA_GATE_RANK), (0, 0))).astype(BF16),
        'b_gate': row(p['b_gla_gate'][l]),
        'g_cq': row(p['g_cq'][l]),
        'w_uq': jnp.concatenate([w_uq[:, :, :MLA_NOPE].reshape(MLA_Q_LORA, -1),
                                 w_uq[:, :, MLA_NOPE:].reshape(MLA_Q_LORA, -1)], axis=1).astype(BF16),
        'g_qn': row(p['g_qn'][l], MLA_HEADS),
        'g_qr': row(p['g_qr'][l], MLA_HEADS),
        'g_kr': jnp.pad(p['g_kr'][l], (0, LANES - MLA_ROPE))[None, :],
        'g_ckv': row(p['g_ckv'][l]),
        'w_ukn': w_ukv[:, :, :MLA_NOPE].reshape(MLA_KV_LORA, -1).astype(BF16),
        'w_ukv_v': w_ukv[:, :, MLA_NOPE:].reshape(MLA_KV_LORA, -1).astype(BF16),
        'g_kn': row(p['g_kn'][l], MLA_HEADS),
        'w_ukn_t': w_ukv[:, :, :MLA_NOPE].reshape(MLA_KV_LORA, -1).T.astype(BF16),
        'w_ukv_v_t': w_ukv[:, :, MLA_NOPE:].reshape(MLA_KV_LORA, -1).T.astype(BF16),
        'g_mla_out_512': row(p['g_mla_out'][l], MLA_HEADS),
        'g_mla_out_col': p['g_mla_out'][l][:, None].astype(F32),
        'g_sb_out': row(p['g_sb_out'][l], SB_HEADS),
        'g_gla_out': row(p['g_gla_out'][l], GLA_HEADS),
        'w_out': p['w_out'][l].astype(BF16),
        'g_cross': row(p['g_cross_norm'][l]),
        'g_mem': row(p['g_mem_norm'][l]),
        'w_cq': p['w_cq'][l].astype(BF16),
        'w_ck': p['w_ck'][l].astype(BF16),
        'w_cv': p['w_cv'][l].astype(BF16),
        'g_cqn': row(p['g_cqn'][l]),
        'g_ckn': row(p['g_ckn'][l]),
        'w_co': p['w_co'][l].astype(BF16),
        'g_ffn': row(p['g_ffn_norm'][l]),
        'w_gate_ffn': p['w_gate'][l].astype(BF16),
        'w_up': p['w_up'][l].astype(BF16),
        'w_down': p['w_down'][l].astype(BF16),
        'seg64_512': _seg_matrix(512, 64),
        'seg32_256': _seg_matrix(256, 32),
        'seg64_256': _seg_matrix(256, 64),
        'rope_place': jnp.asarray(place, dtype=BF16),
    }


def _time_minor_caches(cache_sb_k, cache_sb_v, cache_mla_latent, cache_mla_krope):
    return (jnp.transpose(cache_sb_k, (0, 1, 3, 4, 2)), jnp.transpose(cache_sb_v, (0, 1, 3, 4, 2)),
            cache_mla_latent, jnp.transpose(cache_mla_krope, (0, 1, 3, 2)))


def _trunk_layer(x, caches, layer, gla_s0, mem_k, mem_v, wl):
    B, T, _ = x.shape
    has_past = caches is not None
    past = caches[0].shape[-1] if has_past else 0
    x2d = x.reshape(B * T, D_MODEL)
    qa, ka, va, qg, kg, vg, la, rg, *q_mla, lat, kr = _inproj(x2d, B, T, past, wl, tm=512,
                                                              time_minor=not has_past)
    r3 = lambda a: a.reshape(B, T, a.shape[-1])

    if has_past:
        sb_kt, sb_vt, lat_cache, kr_t_cache = caches
        o_a = _sb_attn(r3(qa), r3(ka), r3(va), sb_kt, sb_vt, layer, wl, tq=256, tk=256, window=512)
        ka_out = ka.reshape(B, T, SB_HEADS, SB_HD)
        va_out = va.reshape(B, T, SB_HEADS, SB_HD)
        kr_out = r3(kr)
    else:
        o_a = _sb_attn(r3(qa), ka, va, None, None, None, wl, tq=256, tk=256, window=None)
        ka_out = jnp.transpose(ka, (0, 3, 1, 2))
        va_out = jnp.transpose(va, (0, 3, 1, 2))
        kr_out = jnp.transpose(kr, (0, 2, 1))

    if gla_s0 is None:
        s0 = jnp.zeros((B, GLA_QK_W, GLA_W), F32)
    else:
        s0 = _state_to_blockdiag(gla_s0)
    o_b, s_bd = _gla(r3(qg), r3(kg), r3(vg), r3(la), r3(rg), s0, wl, tc=128, nb=4)
    gla_s = _state_from_blockdiag(s_bd)

    if has_past:
        o_c = _mla_dec(*q_mla, lat_cache, kr_t_cache, layer, lat, kr, T, wl, kc=1024, nb=2)
    else:
        kcat, v_pairs = _mla_kv(r3(lat), kr, wl, tr=512)
        o_c = _mla_attn(*q_mla, kcat, v_pairs, wl, tq=256)

    x2d = _post(x2d, o_a.reshape(B * T, SB_W), o_b.reshape(B * T, GLA_W), o_c.reshape(B * T, MLA_W),
                mem_k.reshape(B, MEM_LEN, MEM_W), mem_v.reshape(B, MEM_LEN, MEM_W), T, wl, tm=512)
    x2d = _ffn(x2d, wl, tm=512, n_chunks=2)
    return x2d.reshape(B, T, D_MODEL), ka_out, va_out, gla_s, r3(lat), kr_out


def kernel(x_prompt, x_sample, mem_prompt, cache_sb_k, cache_sb_v, state_gla, cache_mla_latent,
           cache_mla_krope, cache_mem_k, cache_mem_v, g_mix_norm, w_in, w_gla_gate, b_gla_gate,
           g_gla_out, g_sb_out, g_cq, w_uq, g_qn, g_qr, g_kr, g_ckv, w_ukv, g_kn, g_mla_out, w_out,
           g_cross_norm, g_mem_norm, w_cq, w_ck, w_cv, g_cqn, g_ckn, w_co, g_ffn_norm, w_gate,
           w_up, w_down):
    p = dict(g_mix_norm=g_mix_norm, w_in=w_in, w_gla_gate=w_gla_gate, b_gla_gate=b_gla_gate,
             g_gla_out=g_gla_out, g_sb_out=g_sb_out, g_cq=g_cq, w_uq=w_uq, g_qn=g_qn, g_qr=g_qr,
             g_kr=g_kr, g_ckv=g_ckv, w_ukv=w_ukv, g_kn=g_kn, g_mla_out=g_mla_out, w_out=w_out,
             g_cross_norm=g_cross_norm, g_mem_norm=g_mem_norm, w_cq=w_cq, w_ck=w_ck, w_cv=w_cv,
             g_cqn=g_cqn, g_ckn=g_ckn, w_co=w_co, g_ffn_norm=g_ffn_norm, w_gate=w_gate, w_up=w_up,
             w_down=w_down)
    depth = w_in.shape[0]
    B = x_prompt.shape[0]
    xp, xs = x_prompt, x_sample
    outs_p = [[] for _ in range(7)]
    outs_s = [[] for _ in range(5)]
    caches = _time_minor_caches(cache_sb_k, cache_sb_v, cache_mla_latent, cache_mla_krope)
    for l in range(depth):
        wl = _prep_layer(p, l)
        mk, mv = _memkv(mem_prompt.reshape(B * MEM_LEN, D_MODEL), wl, tm=512)
        mk = mk.reshape(B, MEM_LEN, MEM_HEADS, MEM_HD)
        mv = mv.reshape(B, MEM_LEN, MEM_HEADS, MEM_HD)
        xp, ka, va, st, lat, kr = _trunk_layer(xp, None, l, None, mk, mv, wl)
        for lst, val in zip(outs_p, (ka, va, st, lat, kr, mk, mv)):
            lst.append(val)
        xs, ka, va, st, lat, kr = _trunk_layer(xs, caches, l, state_gla[l], cache_mem_k[l], cache_mem_v[l], wl)
        for lst, val in zip(outs_s, (ka, va, st, lat, kr)):
            lst.append(val)
    return (xp, xs, *[jnp.stack(v) for v in outs_p], *[jnp.stack(v) for v in outs_s])
```

```python
import functools
import math

import jax
import jax.numpy as jnp
import numpy as np
from jax import lax
from jax.experimental import pallas as pl
from jax.experimental.pallas import tpu as pltpu

F32 = jnp.float32
BF16 = jnp.bfloat16

D_MODEL = 1024
CHUNK = 64
SB_HEADS, SB_HD = 4, 64
SB_W = SB_HEADS * SB_HD
GLA_HEADS, GLA_DK, GLA_DV = 4, 32, 64
GLA_QK_W = GLA_HEADS * GLA_DK
GLA_W = GLA_HEADS * GLA_DV
GLA_GATE_RANK = 16
GLA_TAU = 16.0
GLA_BLOCK = 16
MLA_HEADS = 8
MLA_Q_LORA, MLA_KV_LORA = 256, 128
MLA_NOPE, MLA_ROPE, MLA_V = 64, 32, 64
MLA_W = MLA_HEADS * MLA_V
ROPE_THETA = 10000.0
MEM_LEN, MEM_HEADS, MEM_HD = 256, 4, 128
MEM_W = MEM_HEADS * MEM_HD
D_FF = 2816
NORM_EPS = 1e-6

LANES = 128
VMEM_LIMIT = 56 * 1024 * 1024
SB_SCALE = 1.0 / math.sqrt(SB_HD)
GLA_SCALE = GLA_DK ** -0.5
MLA_SCALE = (MLA_NOPE + MLA_ROPE) ** -0.5
MEM_SCALE = MEM_HD ** -0.5
LOG2E = math.log2(math.e)
SB_Q_SCALE = SB_SCALE * LOG2E
MLA_Q_SCALE2 = MLA_SCALE * LOG2E
SB_DEAD_LOG2 = -105.0 * LOG2E
MASK_NEG = -1e30


def _dot(a, b):
    return jnp.dot(a, b, preferred_element_type=F32)


def _dot_nt(a, b):
    return lax.dot_general(a, b, (((1,), (1,)), ((), ())), preferred_element_type=F32)


def _dot_tn(a, b):
    return lax.dot_general(a, b, (((0,), (0,)), ((), ())), preferred_element_type=F32)


def _split(x):
    hi = x.astype(BF16)
    lo = (x - hi.astype(F32)).astype(BF16)
    return hi, lo


def _dot_split_lhs(x, m):
    hi, lo = _split(x)
    return _dot(hi, m) + _dot(lo, m)


def _dot_split_rhs(m, x):
    hi, lo = _split(x)
    return _dot(m, hi) + _dot(m, lo)


def _rms(x, g):
    ms = jnp.mean(x * x, axis=-1, keepdims=True)
    return x * lax.rsqrt(ms + NORM_EPS) * g


def _seg_rms(x, g, seg, width):
    ss = _dot_split_lhs(x * x, seg)
    return x * lax.rsqrt(ss * (1.0 / width) + NORM_EPS) * g


def _softplus(z):
    return jnp.maximum(z, 0.0) + jnp.log1p(jnp.exp(-jnp.abs(z)))


def _softplus2(z2):
    return jnp.maximum(z2, 0.0) + jnp.log2(1.0 + jnp.exp2(-jnp.abs(z2)))


def _sigmoid(z):
    return 1.0 / (1.0 + jnp.exp(-z))


def _swap_halves(x, seg):
    n = x.shape[-1]
    half = seg // 2
    lane = lax.broadcasted_iota(jnp.int32, x.shape, x.ndim - 1)
    first = (lane & (seg - 1)) < half
    return jnp.where(first, pltpu.roll(x, n - half, x.ndim - 1), pltpu.roll(x, half, x.ndim - 1))


def _div_pow2(x, d):
    return lax.shift_right_logical(x, int(math.log2(d)))


def _seg_matrix(n, width):
    i = np.arange(n) // width
    return jnp.asarray(i[:, None] == i[None, :], dtype=BF16)


def _full(shape):
    nd = len(shape)
    return pl.BlockSpec(shape, lambda *_: (0,) * nd)


def _params(sem):
    return pltpu.CompilerParams(dimension_semantics=sem, vmem_limit_bytes=VMEM_LIMIT)


def _inproj_kernel(x_ref, gmix_ref, wsb_ref, wgl_ref, wag_ref, wml_ref, wgate_ref, bgate_ref,
                   gcq_ref, wuq_ref, gqn_ref, gqr_ref, gkr_ref, gckv_ref, seg64_ref, seg32_ref,
                   cos_ref, sin_ref,
                   qa_ref, ka_ref, va_ref, qg_ref, kg_ref, vg_ref, la_ref, rg_ref, *mla_refs, time_minor):
    if time_minor:
        qcat_ref, lat_ref, kr_ref = mla_refs
    else:
        qn_ref, qr_ref, lat_ref, kr_ref = mla_refs
    tm = x_ref.shape[0]
    h = _rms(x_ref[...], gmix_ref[...]).astype(BF16)

    ml = _dot(h, wml_ref[...])
    ag = _dot(h, wag_ref[...])
    sb = _dot(h, wsb_ref[...])
    gl = _dot(h, wgl_ref[...])
    cq = ml[:, :MLA_Q_LORA]
    q = _dot(_rms(cq, gcq_ref[...]).astype(BF16), wuq_ref[...])
    gate = _dot(ag.astype(BF16), wgate_ref[...]) + bgate_ref[...]

    qa_ref[...] = (sb[:, :SB_W] * SB_Q_SCALE).astype(BF16)
    ka = sb[:, SB_W:2 * SB_W]
    va = sb[:, 2 * SB_W:]
    if time_minor:
        ka_ref[...] = ka.T.reshape(SB_HEADS, SB_HD, tm)
        va_ref[...] = va.T.reshape(SB_HEADS, SB_HD, tm)
    else:
        ka_ref[...] = ka
        va_ref[...] = va

    qg_ref[...] = gl[:, :GLA_QK_W]
    kg_ref[...] = gl[:, GLA_QK_W:2 * GLA_QK_W]
    vg_ref[...] = gl[:, 2 * GLA_QK_W:2 * GLA_QK_W + GLA_W]
    rg_ref[...] = gl[:, 2 * GLA_QK_W + GLA_W:]
    la_ref[...] = -_softplus(-gate) * (1.0 / GLA_TAU)

    ckv = ml[:, MLA_Q_LORA:MLA_Q_LORA + MLA_KV_LORA]
    krp = ml[:, MLA_Q_LORA + MLA_KV_LORA:]
    lat_ref[...] = _rms(ckv, gckv_ref[...])
    cos = cos_ref[...]
    sin = sin_ref[...]
    kr_ms = jnp.sum(krp * krp, axis=-1, keepdims=True) * (1.0 / MLA_ROPE)
    krn = krp * lax.rsqrt(kr_ms + NORM_EPS) * gkr_ref[...]
    kr_rot = krn * cos[:, :LANES] + _swap_halves(krn, MLA_ROPE) * sin[:, :LANES]
    if time_minor:
        kr_ref[...] = kr_rot.T[:MLA_ROPE, :]
    else:
        kr_ref[...] = kr_rot[:, :MLA_ROPE]

    nope_w = MLA_HEADS * MLA_NOPE
    q_scale = MLA_Q_SCALE2 if time_minor else MLA_SCALE
    qn = _seg_rms(q[:, :nope_w], gqn_ref[...], seg64_ref[...], MLA_NOPE) * q_scale
    qr = _seg_rms(q[:, nope_w:], gqr_ref[...], seg32_ref[...], MLA_ROPE)
    qr = (qr * cos + _swap_halves(qr, MLA_ROPE) * sin) * q_scale
    if not time_minor:
        qn_ref[...] = qn.astype(BF16)
        qr_ref[...] = qr.astype(BF16)
        return
    lane = lax.broadcasted_iota(jnp.int32, (1, LANES), 1)
    rope_heads_per_tile = LANES // MLA_ROPE
    for hd in range(MLA_HEADS):
        pair = qn[:, LANES * (hd // 2):LANES * (hd // 2 + 1)]
        nope = pair if hd % 2 == 0 else pltpu.roll(pair, MLA_NOPE, 1)
        tile = qr[:, LANES * (hd // rope_heads_per_tile):LANES * (hd // rope_heads_per_tile + 1)]
        shift = (MLA_NOPE - MLA_ROPE * (hd % rope_heads_per_tile)) % LANES
        rope = tile if shift == 0 else pltpu.roll(tile, shift, 1)
        qcat_ref[hd] = jnp.where(lane < MLA_NOPE, nope,
                                 jnp.where(lane < MLA_NOPE + MLA_ROPE, rope, 0.0)).astype(BF16)


def _rope_tables(pos):
    half = MLA_ROPE // 2
    freqs = ROPE_THETA ** (-jnp.arange(half, dtype=F32) / half)
    ang = pos.astype(F32)[:, None] * freqs[None, :]
    cos = jnp.cos(ang)
    sin = jnp.sin(ang)
    cos_t = jnp.tile(jnp.concatenate([cos, cos], axis=-1), (1, MLA_HEADS))
    sin_t = jnp.tile(jnp.concatenate([-sin, sin], axis=-1), (1, MLA_HEADS))
    return cos_t, sin_t


def _inproj(x2d, B, T, past, wl, tm, time_minor):
    n = x2d.shape[0]
    tm = min(tm, n)
    cos_t, sin_t = _rope_tables(past + jnp.arange(T, dtype=jnp.int32))
    if T < tm:
        cos_t = jnp.tile(cos_t, (tm // T, 1))
        sin_t = jnp.tile(sin_t, (tm // T, 1))
    n_tab = cos_t.shape[0] // tm
    row = lambda w: pl.BlockSpec((tm, w), lambda i: (i, 0))
    tab = pl.BlockSpec((tm, 2 * LANES), lambda i: (i % n_tab, 0))
    weights = (wl['g_mix'], wl['w_sb'], wl['w_gl'], wl['w_ag'], wl['w_ml'], wl['w_gate'], wl['b_gate'],
               wl['g_cq'], wl['w_uq'], wl['g_qn'], wl['g_qr'], wl['g_kr'], wl['g_ckv'],
               wl['seg64_512'], wl['seg32_256'])
    rowout = lambda w, dt: (row(w), jax.ShapeDtypeStruct((n, w), dt))
    if time_minor:
        assert T % tm == 0
        tpb = T // tm
        kv_out = (pl.BlockSpec((None, SB_HEADS, SB_HD, tm), lambda i: (i // tpb, 0, 0, i % tpb)),
                  jax.ShapeDtypeStruct((B, SB_HEADS, SB_HD, T), F32))
        kr_out = (pl.BlockSpec((None, MLA_ROPE, tm), lambda i: (i // tpb, 0, i % tpb)),
                  jax.ShapeDtypeStruct((B, MLA_ROPE, T), F32))
        q_out = [(pl.BlockSpec((None, MLA_HEADS, tm, LANES), lambda i: (i // tpb, 0, i % tpb, 0)),
                  jax.ShapeDtypeStruct((B, MLA_HEADS, T, LANES), BF16))]
    else:
        kv_out = rowout(SB_W, F32)
        kr_out = rowout(MLA_ROPE, F32)
        q_out = [rowout(MLA_HEADS * MLA_NOPE, BF16), rowout(MLA_HEADS * MLA_ROPE, BF16)]
    outs = [rowout(SB_W, BF16), kv_out, kv_out,
            rowout(GLA_QK_W, F32), rowout(GLA_QK_W, F32), rowout(GLA_W, F32), rowout(GLA_QK_W, F32),
            rowout(GLA_W, F32), *q_out, rowout(MLA_KV_LORA, F32), kr_out]
    return pl.pallas_call(
        functools.partial(_inproj_kernel, time_minor=time_minor),
        grid=(n // tm,),
        in_specs=[row(D_MODEL)] + [_full(w.shape) for w in weights] + [tab, tab],
        out_specs=[o[0] for o in outs],
        out_shape=[o[1] for o in outs],
        compiler_params=_params(("parallel",)),
        name="inproj",
    )(x2d, *weights, cos_t, sin_t)


def _sb_kernel(*refs, tq, tk, has_past, n_far_blocks, layer):
    if has_past:
        (q_ref, kn_ref, vn_ref, kt_ref, vt_ref, kt_all_ref, vt_all_ref, g_ref, seg_ref, o_ref,
         kbuf_ref, vbuf_ref) = refs
    else:
        q_ref, kt_ref, vt_ref, g_ref, seg_ref, o_ref = refs
    qi = pl.program_id(1)
    q = q_ref[...]
    lane = lax.broadcasted_iota(jnp.int32, (1, SB_W), 1)
    hmask = [(lane >= SB_HD * h) & (lane < SB_HD * (h + 1)) for h in range(SB_HEADS)]
    qh = [jnp.where(hmask[h], q, jnp.zeros_like(q)) for h in range(SB_HEADS)]

    def suffix_matrix(n):
        return (lax.broadcasted_iota(jnp.int32, (n, n), 0)
                > lax.broadcasted_iota(jnp.int32, (n, n), 1)).astype(BF16)

    def time_minor_block(ref, start, n):
        return ref[:, :, pl.ds(start, n)].reshape(SB_W, n).astype(BF16)

    def block(kblk, vblk, time_minor, vis, umat, cs, acc):
        heads = range(SB_HEADS)
        zs = [_dot(qh[h], kblk) if time_minor else _dot_nt(qh[h], kblk) for h in heads]
        lks = [-_softplus2(z) for z in zs]
        if vis is not None:
            lks = [jnp.where(vis, lk, 0.0) for lk in lks]
        splits = [_split(lk) for lk in lks]
        laters = [_dot(hi, umat) + _dot(lo, umat) for hi, lo in splits]
        ws = [jnp.exp2(zs[h] + lks[h] + laters[h] + cs[h]) for h in heads]
        if vis is not None:
            ws = [jnp.where(vis, w, 0.0) for w in ws]
        ws = [w.astype(BF16) for w in ws]
        pvs = [_dot_nt(w, vblk) if time_minor else _dot(w, vblk) for w in ws]
        for h in heads:
            acc = acc + jnp.where(hmask[h], pvs[h], 0.0)
        new_cs = [cs[h] + laters[h][:, :1] + lks[h][:, :1] for h in heads]
        return new_cs, acc

    def dead(cs):
        m = cs[0]
        for c in cs[1:]:
            m = jnp.maximum(m, c)
        return jnp.max(m)

    vis = (lax.broadcasted_iota(jnp.int32, (tq, tq), 1)
           < lax.broadcasted_iota(jnp.int32, (tq, tq), 0))
    zero_c = jnp.zeros((tq, 1), F32)
    init = ([zero_c] * SB_HEADS, jnp.zeros((tq, SB_W), F32))
    if has_past:
        cs, acc = block(kn_ref[...].astype(BF16), vn_ref[...].astype(BF16), False, vis,
                        suffix_matrix(tq), *init)
        kb0 = jnp.int32(kt_ref.shape[-1] // tk - 1)
    else:
        q0 = pl.multiple_of(qi * tq, tq)
        cs, acc = block(time_minor_block(kt_ref, q0, tq), time_minor_block(vt_ref, q0, tq), True, vis,
                        suffix_matrix(tq), *init)
        kb0 = qi - 1

    umat = suffix_matrix(tk)

    def walk(fetch, kb0, cs, acc):
        def cond(st):
            return jnp.logical_and(st[0] >= 0, st[1] > SB_DEAD_LOG2)

        def body(st):
            kb = st[0]
            cs, acc = list(st[2:2 + SB_HEADS]), st[2 + SB_HEADS]
            kblk, vblk = fetch(pl.multiple_of(kb * tk, tk))
            cs, acc = block(kblk, vblk, True, None, umat, cs, acc)
            return (kb - 1, dead(cs), *cs, acc)

        st = lax.while_loop(cond, body, (kb0, dead(cs), *cs, acc))
        return list(st[2:2 + SB_HEADS]), st[2 + SB_HEADS]

    cs, acc = walk(lambda start: (time_minor_block(kt_ref, start, tk), time_minor_block(vt_ref, start, tk)),
                   kb0, cs, acc)
    if has_past and n_far_blocks:
        b = pl.program_id(0)

        def fetch_far(start):
            pltpu.sync_copy(kt_all_ref.at[layer, b, :, :, pl.ds(start, tk)], kbuf_ref)
            pltpu.sync_copy(vt_all_ref.at[layer, b, :, :, pl.ds(start, tk)], vbuf_ref)
            return (kbuf_ref[...].reshape(SB_W, tk).astype(BF16), vbuf_ref[...].reshape(SB_W, tk).astype(BF16))

        cs, acc = walk(fetch_far, jnp.int32(n_far_blocks - 1), cs, acc)
    o_ref[...] = _seg_rms(acc, g_ref[...], seg_ref[...], SB_HD).astype(BF16)


def _sb_attn(q, k_new, v_new, kt_cache, vt_cache, layer, wl, tq, tk, window):
    B, T, _ = q.shape
    tq = min(tq, T)
    has_past = kt_cache is not None
    n_far_blocks = 0
    scratch = []
    in_specs = [pl.BlockSpec((None, tq, SB_W), lambda b, i: (b, i, 0))]
    if has_past:
        P = kt_cache.shape[-1]
        window = min(window, P)
        assert tq == T and P % window == 0 and window % tk == 0
        n_far_blocks = (P - window) // tk
        new = pl.BlockSpec((None, T, SB_W), lambda b, i: (b, 0, 0))
        near = pl.BlockSpec((None, None, SB_HEADS, SB_HD, window), lambda b, i: (layer, b, 0, 0, P // window - 1))
        hbm = pl.BlockSpec(memory_space=pl.ANY)
        in_specs += [new, new, near, near, hbm, hbm]
        args = [q, k_new, v_new, kt_cache, vt_cache, kt_cache, vt_cache]
        scratch = [pltpu.VMEM((SB_HEADS, SB_HD, tk), F32)] * 2
    else:
        assert tq == tk
        new = pl.BlockSpec((None, SB_HEADS, SB_HD, T), lambda b, i: (b, 0, 0, 0))
        in_specs += [new, new]
        args = [q, k_new, v_new]
    in_specs += [_full((1, SB_W)), _full((SB_W, SB_W))]
    args += [wl['g_sb_out'], wl['seg64_256']]
    return pl.pallas_call(
        functools.partial(_sb_kernel, tq=tq, tk=tk, has_past=has_past, n_far_blocks=n_far_blocks, layer=layer),
        grid=(B, T // tq),
        in_specs=in_specs,
        out_specs=pl.BlockSpec((None, tq, SB_W), lambda b, i: (b, i, 0)),
        out_shape=jax.ShapeDtypeStruct((B, T, SB_W), BF16),
        scratch_shapes=scratch,
        compiler_params=_params(("parallel", "parallel")),
        name="sb_attn",
    )(*args)


def _gla_kernel(q_ref, k_ref, v_ref, la_ref, rg_ref, s0_ref, g_ref, seg_ref, o_ref, s_ref, *, tc, nb):
    ci = pl.program_id(1)

    @pl.when(ci == 0)
    def _():
        s_ref[...] = s0_ref[...]

    nsub = tc // GLA_BLOCK
    r_i = lax.broadcasted_iota(jnp.int32, (tc, tc), 0)
    c_i = lax.broadcasted_iota(jnp.int32, (tc, tc), 1)
    same = _div_pow2(r_i, GLA_BLOCK) == _div_pow2(c_i, GLA_BLOCK)
    causal = jnp.logical_and(same, c_i <= r_i)
    causal_b = causal.astype(BF16)
    lane_k = lax.broadcasted_iota(jnp.int32, (1, GLA_QK_W), 1)
    lane_v = lax.broadcasted_iota(jnp.int32, (1, GLA_W), 1)
    kms = [(lane_k >= GLA_DK * h) & (lane_k < GLA_DK * (h + 1)) for h in range(GLA_HEADS)]
    vms = [(lane_v >= GLA_DV * h) & (lane_v < GLA_DV * (h + 1)) for h in range(GLA_HEADS)]
    sr = _div_pow2(lax.broadcasted_iota(jnp.int32, (GLA_QK_W, GLA_W), 0), GLA_DK)
    sc = _div_pow2(lax.broadcasted_iota(jnp.int32, (GLA_QK_W, GLA_W), 1), GLA_DV)
    bd = sr == sc
    eye = (lax.broadcasted_iota(jnp.int32, (GLA_QK_W, GLA_QK_W), 0)
           == lax.broadcasted_iota(jnp.int32, (GLA_QK_W, GLA_QK_W), 1))
    g = g_ref[...]
    seg = seg_ref[...]

    rows_b = range(nb)
    heads = range(GLA_HEADS)
    las = [_split(la_ref[bi]) for bi in rows_b]
    bs = [_dot(causal_b, hi) + _dot(causal_b, lo) for hi, lo in las]
    bls = [jnp.broadcast_to(b.reshape(nsub, GLA_BLOCK, GLA_QK_W)[:, GLA_BLOCK - 1:, :],
                            (nsub, GLA_BLOCK, GLA_QK_W)).reshape(tc, GLA_QK_W) for b in bs]
    qts = [(q_ref[bi] * jnp.exp(bs[bi]) * GLA_SCALE).astype(BF16) for bi in rows_b]
    kts = [(k_ref[bi] * jnp.exp(-bs[bi])).astype(BF16) for bi in rows_b]
    kds = [(k_ref[bi] * jnp.exp(bls[bi] - bs[bi])).astype(BF16) for bi in rows_b]
    vbs = [v_ref[bi].astype(BF16) for bi in rows_b]
    decays = [jnp.exp(bl) for bl in bls]

    atts = [[_dot_nt(jnp.where(kms[h], qts[bi], jnp.zeros_like(qts[bi])), kts[bi]) for h in heads]
            for bi in rows_b]
    atts = [[jnp.where(causal, a, 0.0).astype(BF16) for a in row] for row in atts]
    pvs = [[_dot(atts[bi][h], vbs[bi]) for h in heads] for bi in rows_b]
    intras = []
    for bi in rows_b:
        intra = jnp.zeros((tc, GLA_W), F32)
        for h in heads:
            intra = intra + jnp.where(vms[h], pvs[bi][h], 0.0)
        intras.append(intra)

    states = [s_ref[bi] for bi in rows_b]
    inters = [[] for _ in rows_b]
    for j in range(nsub):
        rows = slice(j * GLA_BLOCK, (j + 1) * GLA_BLOCK)
        for bi in rows_b:
            inters[bi].append(_dot(qts[bi][rows], states[bi].astype(BF16)))
        upds = [_dot_tn(kds[bi][rows], vbs[bi][rows]) for bi in rows_b]
        for bi in rows_b:
            d_row = decays[bi][j * GLA_BLOCK:j * GLA_BLOCK + 1]
            d_col = jnp.sum(jnp.where(eye, d_row, 0.0), axis=1, keepdims=True)
            states[bi] = d_col * states[bi] + jnp.where(bd, upds[bi], 0.0)
    os_ = [intras[bi] + jnp.concatenate(inters[bi], axis=0) for bi in rows_b]
    splits = [_split(o * o) for o in os_]
    sss = [_dot(hi, seg) + _dot(lo, seg) for hi, lo in splits]
    for bi in rows_b:
        s_ref[bi] = states[bi]
        o = os_[bi] * lax.rsqrt(sss[bi] * (1.0 / GLA_DV) + NORM_EPS) * g
        rg = rg_ref[bi]
        o_ref[bi] = (o * (rg * _sigmoid(rg))).astype(BF16)


def _gla(qg, kg, vg, la, rg, s0_bd, wl, tc, nb):
    B, T, _ = qg.shape
    tc = min(tc, T)
    nb = min(nb, B)
    blk = lambda w: pl.BlockSpec((nb, tc, w), lambda b, c: (b, c, 0))
    st = pl.BlockSpec((nb, GLA_QK_W, GLA_W), lambda b, c: (b, 0, 0))
    return pl.pallas_call(
        functools.partial(_gla_kernel, tc=tc, nb=nb),
        grid=(B // nb, T // tc),
        in_specs=[blk(GLA_QK_W), blk(GLA_QK_W), blk(GLA_W), blk(GLA_QK_W), blk(GLA_W), st,
                  _full((1, GLA_W)), _full((GLA_W, GLA_W))],
        out_specs=[blk(GLA_W), st],
        out_shape=[jax.ShapeDtypeStruct((B, T, GLA_W), BF16),
                   jax.ShapeDtypeStruct((B, GLA_QK_W, GLA_W), F32)],
        compiler_params=_params(("parallel", "arbitrary")),
        name="gla",
    )(qg, kg, vg, la, rg, s0_bd, wl['g_gla_out'], wl['seg64_256'])


def _state_to_blockdiag(s):
    B = s.shape[0]
    eye = jnp.eye(GLA_HEADS, dtype=s.dtype)
    return jnp.einsum('bhkv,hg->bhkgv', s, eye).reshape(B, GLA_QK_W, GLA_W)


def _state_from_blockdiag(s):
    B = s.shape[0]
    s5 = s.reshape(B, GLA_HEADS, GLA_DK, GLA_HEADS, GLA_DV)
    return jnp.stack([s5[:, h, :, h, :] for h in range(GLA_HEADS)], axis=1)


def _mla_kv_kernel(lat_ref, kr_ref, wn_ref, wvt_ref, gkn_ref, seg_ref, place_ref, kcat_ref, vt_ref):
    tr = lat_ref.shape[0]
    lat = lat_ref[...].astype(BF16)
    kn = _seg_rms(_dot(lat, wn_ref[...]), gkn_ref[...], seg_ref[...], MLA_NOPE)
    vt_ref[...] = _dot_nt(wvt_ref[...], lat).reshape(MLA_HEADS, MLA_V, tr).astype(BF16)
    krp = _dot_tn(kr_ref[...].astype(BF16), place_ref[...])
    lane = lax.broadcasted_iota(jnp.int32, (1, LANES), 1)
    for p in range(MLA_HEADS // 2):
        pair = kn[:, LANES * p:LANES * (p + 1)]
        kcat_ref[2 * p] = jnp.where(lane < MLA_NOPE, pair, krp).astype(BF16)
        kcat_ref[2 * p + 1] = jnp.where(lane < MLA_NOPE, pltpu.roll(pair, MLA_NOPE, 1), krp).astype(BF16)


def _mla_kv(lat, kr_t, wl, tr):
    B, L, _ = lat.shape
    tr = min(tr, L)
    return pl.pallas_call(
        _mla_kv_kernel,
        grid=(B, L // tr),
        in_specs=[pl.BlockSpec((None, tr, MLA_KV_LORA), lambda b, i: (b, i, 0)),
                  pl.BlockSpec((None, MLA_ROPE, tr), lambda b, i: (b, 0, i)),
                  _full(wl['w_ukn'].shape), _full(wl['w_ukv_v_t'].shape), _full((1, MLA_HEADS * MLA_NOPE)),
                  _full(wl['seg64_512'].shape), _full((MLA_ROPE, LANES))],
        out_specs=[pl.BlockSpec((None, MLA_HEADS, tr, LANES), lambda b, i: (b, 0, i, 0)),
                   pl.BlockSpec((None, MLA_HEADS, MLA_V, tr), lambda b, i: (b, 0, 0, i))],
        out_shape=[jax.ShapeDtypeStruct((B, MLA_HEADS, L, LANES), BF16),
                   jax.ShapeDtypeStruct((B, MLA_HEADS, MLA_V, L), BF16)],
        compiler_params=_params(("parallel", "parallel")),
        name="mla_kv",
    )(lat, kr_t, wl['w_ukn'], wl['w_ukv_v_t'], wl['g_kn'], wl['seg64_512'], wl['rope_place'])


def _mla_attn_kernel(q_ref, k_ref, vt_ref, g_ref, o_ref, *, tq):
    qi = pl.program_id(1)
    q0 = pl.multiple_of(qi * tq, tq)
    vis = (_div_pow2(lax.broadcasted_iota(jnp.int32, (tq, tq), 0), CHUNK)
           <= _div_pow2(lax.broadcasted_iota(jnp.int32, (tq, tq), 1), CHUNK))
    qs = [q_ref[h] for h in range(MLA_HEADS)]

    heads = range(MLA_HEADS)

    def step(start, width, st, mask):
        ss = [_dot_nt(k_ref[h, pl.ds(start, width), :], qs[h]) for h in heads]
        if mask:
            ss = [jnp.where(vis, s, MASK_NEG) for s in ss]
        out = []
        for h in heads:
            bm = jnp.max(ss[h], axis=0, keepdims=True)
            m_new = bm if st is None else jnp.maximum(st[3 * h], bm)
            p = jnp.exp2(ss[h] - m_new)
            l_blk = jnp.sum(p, axis=0, keepdims=True)
            pv = _dot(vt_ref[h, :, pl.ds(start, width)], p.astype(BF16))
            if st is None:
                out += [m_new, l_blk, pv]
            else:
                a = jnp.exp2(st[3 * h] - m_new)
                out += [m_new, a * st[3 * h + 1] + l_blk, a * st[3 * h + 2] + pv]
        return tuple(out)

    st = step(q0, tq, None, True)
    st = lax.fori_loop(0, lax.shift_right_logical(qi, 1),
                       lambda kb, st: step(pl.multiple_of(kb * (2 * tq), 2 * tq), 2 * tq, st, False), st)
    st = lax.cond((qi & 1) == 1, lambda st: step(pl.multiple_of((qi - 1) * tq, tq), tq, st, False),
                  lambda st: st, st)
    g = g_ref[...]
    outs = []
    for h in heads:
        o = st[3 * h + 2] / st[3 * h + 1]
        ms = jnp.mean(o * o, axis=0, keepdims=True)
        outs.append(o * lax.rsqrt(ms + NORM_EPS) * g)
    o_ref[...] = jnp.concatenate(outs, axis=0).T.astype(BF16)


def _mla_attn(qcat, kcat, v_t, wl, tq):
    B, _, T, _ = qcat.shape
    tq = min(tq, T)
    return pl.pallas_call(
        functools.partial(_mla_attn_kernel, tq=tq),
        grid=(B, T // tq),
        in_specs=[pl.BlockSpec((None, MLA_HEADS, tq, LANES), lambda b, i: (b, 0, i, 0)),
                  pl.BlockSpec((None, MLA_HEADS, T, LANES), lambda b, i: (b, 0, 0, 0)),
                  pl.BlockSpec((None, MLA_HEADS, MLA_V, T), lambda b, i: (b, 0, 0, 0)),
                  _full((MLA_V, 1))],
        out_specs=pl.BlockSpec((None, tq, MLA_W), lambda b, i: (b, i, 0)),
        out_shape=jax.ShapeDtypeStruct((B, T, MLA_W), BF16),
        compiler_params=_params(("parallel", "parallel")),
        name="mla_attn",
    )(qcat, kcat, v_t, wl['g_mla_out_col'])


def _mla_dec_kernel(qn_ref, qr_ref, lat_ref, krt_ref, latn_ref, krn_ref, wknt_ref, wv_ref, gkn_ref,
                    g_ref, seg_ref, o_ref, s_ref, *, kc):
    nb, P = lat_ref.shape[:2]
    T = qn_ref.shape[0] // nb
    H = MLA_HEADS
    wknt = wknt_ref[...]
    gkn = gkn_ref[...]

    def queries(b):
        rows = slice(b * T, (b + 1) * T)
        qn = (qn_ref[rows, :].astype(F32) * gkn).astype(BF16)
        qabs = jnp.concatenate(
            [_dot(qn[:, MLA_NOPE * h:MLA_NOPE * (h + 1)], wknt[MLA_NOPE * h:MLA_NOPE * (h + 1), :])
             for h in range(H)], axis=0).astype(BF16)
        qr = qr_ref[rows, :]
        qrs = jnp.concatenate([qr[:, MLA_ROPE * h:MLA_ROPE * (h + 1)] for h in range(H)], axis=0)
        return qabs, qrs

    def scores(qabs, lat_b, sr, n):
        kn_t = _dot_nt(wknt, lat_b)
        ss = jnp.sum((kn_t * kn_t).reshape(H, MLA_NOPE, n), axis=1)
        r = lax.rsqrt(ss * (1.0 / MLA_NOPE) + NORM_EPS)
        sn = _dot_nt(qabs, lat_b).reshape(H, T, n)
        return (sn * r[:, None, :]).reshape(H * T, n) + sr

    def all_scores(b, qabs, qrs):
        m = None
        for c in range(P // kc):
            cols = slice(c * kc, (c + 1) * kc)
            s = scores(qabs, lat_ref[b, cols, :].astype(BF16), _dot(qrs, krt_ref[b, :, cols].astype(BF16)), kc)
            s_ref[b, :, cols] = s
            mc = jnp.max(s, axis=-1, keepdims=True)
            m = mc if m is None else jnp.maximum(m, mc)
        rows = slice(b * T, (b + 1) * T)
        s_new = scores(qabs, latn_ref[rows, :].astype(BF16), _dot_nt(qrs, krn_ref[rows, :].astype(BF16)), T)
        return jnp.maximum(m, jnp.max(s_new, axis=-1, keepdims=True)), s_new

    def context(b, m, s_new):
        p = jnp.exp(s_new - m)
        l = jnp.sum(p, axis=-1, keepdims=True)
        ctx = _dot(p.astype(BF16), latn_ref[b * T:(b + 1) * T, :].astype(BF16))
        for c in range(P // kc):
            cols = slice(c * kc, (c + 1) * kc)
            p = jnp.exp(s_ref[b, :, cols] - m)
            l = l + jnp.sum(p, axis=-1, keepdims=True)
            ctx = ctx + _dot(p.astype(BF16), lat_ref[b, cols, :].astype(BF16))
        return (ctx / l).astype(BF16)

    qs = [queries(b) for b in range(nb)]
    stats = [all_scores(b, *qs[b]) for b in range(nb)]
    ctxs = [context(b, *stats[b]) for b in range(nb)]

    wv = wv_ref[...]
    lane = lax.broadcasted_iota(jnp.int32, (1, MLA_W), 1)
    for b in range(nb):
        o = jnp.zeros((T, MLA_W), F32)
        for h in range(H):
            oh = _dot(ctxs[b][T * h:T * (h + 1)], wv)
            o = jnp.where((lane >= MLA_V * h) & (lane < MLA_V * (h + 1)), oh, o)
        o_ref[b * T:(b + 1) * T, :] = _seg_rms(o, g_ref[...], seg_ref[...], MLA_V).astype(BF16)


def _mla_dec(qn, qr, lat_cache, kr_t_cache, layer, lat_new, kr_new, T, wl, kc, nb):
    n = qn.shape[0]
    B = n // T
    P = lat_cache.shape[2]
    row = lambda w: pl.BlockSpec((nb * T, w), lambda b: (b, 0))
    return pl.pallas_call(
        functools.partial(_mla_dec_kernel, kc=kc),
        grid=(B // nb,),
        in_specs=[row(MLA_HEADS * MLA_NOPE), row(MLA_HEADS * MLA_ROPE),
                  pl.BlockSpec((None, nb, P, MLA_KV_LORA), lambda b: (layer, b, 0, 0)),
                  pl.BlockSpec((None, nb, MLA_ROPE, P), lambda b: (layer, b, 0, 0)),
                  row(MLA_KV_LORA), row(MLA_ROPE),
                  _full(wl['w_ukn_t'].shape), _full(wl['w_ukv_v'].shape), _full((1, MLA_HEADS * MLA_NOPE)),
                  _full((1, MLA_W)), _full((MLA_W, MLA_W))],
        out_specs=row(MLA_W),
        out_shape=jax.ShapeDtypeStruct((n, MLA_W), BF16),
        scratch_shapes=[pltpu.VMEM((nb, MLA_HEADS * T, P), F32)],
        compiler_params=_params(("parallel",)),
        name="mla_dec",
    )(qn, qr, lat_cache, kr_t_cache, lat_new, kr_new, wl['w_ukn_t'], wl['w_ukv_v'], wl['g_kn'],
      wl['g_mla_out_512'], wl['seg64_512'])


def _post_kernel(x_ref, oa_ref, ob_ref, oc_ref, mk_ref, mv_ref, wo_ref, gcross_ref, wcq_ref, gcqn_ref,
                 wco_ref, o_ref, att_ref, *, nb, tt):
    wo = wo_ref
    mix = (_dot(oa_ref[...], wo[:SB_W, :]) + _dot(ob_ref[...], wo[SB_W:SB_W + GLA_W, :])
           + _dot(oc_ref[...], wo[SB_W + GLA_W:, :]))
    x1 = x_ref[...] + mix
    h = _rms(x1, gcross_ref[...]).astype(BF16)
    q = _dot(h, wcq_ref[...])
    gq = gcqn_ref[...]
    qs = []
    for hd in range(MEM_HEADS):
        qh = _rms(q[:, MEM_HD * hd:MEM_HD * (hd + 1)], gq).astype(BF16)
        qs.append(qh)
    pairs = [(bi, hd) for bi in range(nb) for hd in range(MEM_HEADS)]
    rows = lambda bi: slice(bi * tt, (bi + 1) * tt)
    cols = lambda hd: slice(MEM_HD * hd, MEM_HD * (hd + 1))
    ss = [_dot_nt(qs[hd][rows(bi)], mk_ref[bi, :, cols(hd)].astype(BF16)) * MEM_SCALE for bi, hd in pairs]
    ps, ls = [], []
    for s in ss:
        p = jnp.exp(s - jnp.max(s, axis=-1, keepdims=True))
        ls.append(jnp.sum(p, axis=-1, keepdims=True))
        ps.append(p.astype(BF16))
    pvs = [_dot(ps[i], mv_ref[bi, :, cols(hd)].astype(BF16)) for i, (bi, hd) in enumerate(pairs)]
    for i, (bi, hd) in enumerate(pairs):
        att_ref[rows(bi), cols(hd)] = pvs[i] / ls[i]
    o_ref[...] = x1 + _dot(att_ref[...].astype(BF16), wco_ref[...])


def _post(x2d, oa, ob, oc, mem_k, mem_v, layer, T, wl, tm):
    n = x2d.shape[0]
    tm = min(tm, n)
    tt = min(T, tm)
    nb = tm // tt
    tiles_per_b = T // tt
    row = lambda w: pl.BlockSpec((tm, w), lambda i: (i, 0))
    mem = pl.BlockSpec((None, nb, MEM_LEN, MEM_W),
                       lambda i: (layer, i // tiles_per_b if nb == 1 else i, 0, 0))
    weights = (wl['w_out'], wl['g_cross'], wl['w_cq'], wl['g_cqn'], wl['w_co'])
    return pl.pallas_call(
        functools.partial(_post_kernel, nb=nb, tt=tt),
        grid=(n // tm,),
        in_specs=[row(D_MODEL), row(SB_W), row(GLA_W), row(MLA_W), mem, mem] + [_full(w.shape) for w in weights],
        out_specs=row(D_MODEL),
        out_shape=jax.ShapeDtypeStruct((n, D_MODEL), F32),
        scratch_shapes=[pltpu.VMEM((tm, MEM_W), F32)],
        compiler_params=_params(("parallel",)),
        name="post",
    )(x2d, oa, ob, oc, mem_k, mem_v, *weights)


def _ffn_kernel(x_ref, g_ref, wg_ref, wu_ref, wd_ref, o_ref, *, n_chunks):
    x = x_ref[...]
    h = _rms(x, g_ref[...]).astype(BF16)
    tf = wg_ref.shape[1] // n_chunks
    acc = x
    for c in range(n_chunks):
        cols = slice(c * tf, (c + 1) * tf)
        gate = _dot(h, wg_ref[:, cols])
        up = _dot(h, wu_ref[:, cols])
        act = (gate * _sigmoid(gate) * up).astype(BF16)
        acc = acc + _dot(act, wd_ref[cols, :])
    o_ref[...] = acc


def _ffn(x2d, wl, tm, n_chunks):
    n = x2d.shape[0]
    tm = min(tm, n)
    resident = lambda shape: pl.BlockSpec(shape, lambda i: (0, 0), pipeline_mode=pl.Buffered(1))
    return pl.pallas_call(
        functools.partial(_ffn_kernel, n_chunks=n_chunks),
        grid=(n // tm,),
        in_specs=[pl.BlockSpec((tm, D_MODEL), lambda i: (i, 0)), _full((1, D_MODEL)),
                  resident((D_MODEL, D_FF)), resident((D_MODEL, D_FF)), resident((D_FF, D_MODEL))],
        out_specs=pl.BlockSpec((tm, D_MODEL), lambda i: (i, 0)),
        out_shape=jax.ShapeDtypeStruct((n, D_MODEL), F32),
        compiler_params=_params(("parallel",)),
        name="ffn",
    )(x2d, wl['g_ffn'], wl['w_gate_ffn'], wl['w_up'], wl['w_down'])


def _memkv_kernel(m_ref, g_ref, wk_ref, wv_ref, gk_ref, k_ref, v_ref):
    h = _rms(m_ref[...], g_ref[...]).astype(BF16)
    k = _dot(h, wk_ref[...])
    gk = gk_ref[...]
    for hd in range(MEM_HEADS):
        cols = slice(MEM_HD * hd, MEM_HD * (hd + 1))
        k_ref[:, cols] = _rms(k[:, cols], gk)
    v_ref[...] = _dot(h, wv_ref[...])


def _memkv(mem2d, wl, tm):
    n = mem2d.shape[0]
    tm = min(tm, n)
    row = lambda w: pl.BlockSpec((tm, w), lambda i: (i, 0))
    weights = (wl['g_mem'], wl['w_ck'], wl['w_cv'], wl['g_ckn'])
    return pl.pallas_call(
        _memkv_kernel,
        grid=(n // tm,),
        in_specs=[row(D_MODEL)] + [_full(w.shape) for w in weights],
        out_specs=[row(MEM_W), row(MEM_W)],
        out_shape=[jax.ShapeDtypeStruct((n, MEM_W), F32)] * 2,
        compiler_params=_params(("parallel",)),
        name="memkv",
    )(mem2d, *weights)


def _prep_layer(p, l):
    w_in = p['w_in'][l]
    o = np.cumsum([0, SB_W, SB_W, SB_W, GLA_QK_W, GLA_QK_W, GLA_W, GLA_GATE_RANK, GLA_W,
                   MLA_Q_LORA, MLA_KV_LORA, MLA_ROPE])
    seg = lambda i, j: w_in[:, o[i]:o[j]]
    pad_cols = lambda w, n: jnp.pad(w, ((0, 0), (0, n - w.shape[1])))
    row = lambda g, reps=1: jnp.tile(g, reps)[None, :].astype(F32)
    w_uq = p['w_uq'][l].reshape(MLA_Q_LORA, MLA_HEADS, MLA_NOPE + MLA_ROPE)
    w_ukv = p['w_ukv'][l].reshape(MLA_KV_LORA, MLA_HEADS, MLA_NOPE + MLA_V)
    place = np.zeros((MLA_ROPE, LANES), np.float32)
    place[np.arange(MLA_ROPE), MLA_NOPE + np.arange(MLA_ROPE)] = 1.0
    return {
        'g_mix': row(p['g_mix_norm'][l]),
        'w_sb': seg(0, 3).astype(BF16),
        'w_gl': jnp.concatenate([seg(3, 6), seg(7, 8)], axis=1).astype(BF16),
        'w_ag': pad_cols(seg(6, 7), LANES).astype(BF16),
        'w_ml': pad_cols(seg(8, 11), 4 * LANES).astype(BF16),
        'w_gate': jnp.pad(p['w_gla_gate'][l], ((0, LANES - GLA_GATE_RANK), (0, 0))).astype(BF16),
        'b_gate': row(p['b_gla_gate'][l]),
        'g_cq': row(p['g_cq'][l]),
        'w_uq': jnp.concatenate([w_uq[:, :, :MLA_NOPE].reshape(MLA_Q_LORA, -1),
                                 w_uq[:, :, MLA_NOPE:].reshape(MLA_Q_LORA, -1)], axis=1).astype(BF16),
        'g_qn': row(p['g_qn'][l], MLA_HEADS),
        'g_qr': row(p['g_qr'][l], MLA_HEADS),
        'g_kr': jnp.pad(p['g_kr'][l], (0, LANES - MLA_ROPE))[None, :],
        'g_ckv': row(p['g_ckv'][l]),
        'w_ukn': w_ukv[:, :, :MLA_NOPE].reshape(MLA_KV_LORA, -1).astype(BF16),
        'w_ukv_v': w_ukv[:, :, MLA_NOPE:].reshape(MLA_KV_LORA, -1).astype(BF16),
        'g_kn': row(p['g_kn'][l], MLA_HEADS),
        'w_ukn_t': w_ukv[:, :, :MLA_NOPE].reshape(MLA_KV_LORA, -1).T.astype(BF16),
        'w_ukv_v_t': w_ukv[:, :, MLA_NOPE:].reshape(MLA_KV_LORA, -1).T.astype(BF16),
        'g_mla_out_512': row(p['g_mla_out'][l], MLA_HEADS),
        'g_mla_out_col': p['g_mla_out'][l][:, None].astype(F32),
        'g_sb_out': row(p['g_sb_out'][l], SB_HEADS),
        'g_gla_out': row(p['g_gla_out'][l], GLA_HEADS),
        'w_out': p['w_out'][l].astype(BF16),
        'g_cross': row(p['g_cross_norm'][l]),
        'g_mem': row(p['g_mem_norm'][l]),
        'w_cq': p['w_cq'][l].astype(BF16),
        'w_ck': p['w_ck'][l].astype(BF16),
        'w_cv': p['w_cv'][l].astype(BF16),
        'g_cqn': row(p['g_cqn'][l]),
        'g_ckn': row(p['g_ckn'][l]),
        'w_co': p['w_co'][l].astype(BF16),
        'g_ffn': row(p['g_ffn_norm'][l]),
        'w_gate_ffn': p['w_gate'][l].astype(BF16),
        'w_up': p['w_up'][l].astype(BF16),
        'w_down': p['w_down'][l].astype(BF16),
        'seg64_512': _seg_matrix(512, 64),
        'seg32_256': _seg_matrix(256, 32),
        'seg64_256': _seg_matrix(256, 64),
        'rope_place': jnp.asarray(place, dtype=BF16),
    }


def _time_minor_caches(cache_sb_k, cache_sb_v, cache_mla_latent, cache_mla_krope):
    return (jnp.transpose(cache_sb_k, (0, 1, 3, 4, 2)), jnp.transpose(cache_sb_v, (0, 1, 3, 4, 2)),
            cache_mla_latent, jnp.transpose(cache_mla_krope, (0, 1, 3, 2)))


def _trunk_layer(x, caches, layer, gla_s0, mem_k, mem_v, wl):
    B, T, _ = x.shape
    has_past = caches is not None
    past = caches[0].shape[-1] if has_past else 0
    x2d = x.reshape(B * T, D_MODEL)
    qa, ka, va, qg, kg, vg, la, rg, *q_mla, lat, kr = _inproj(x2d, B, T, past, wl, tm=512,
                                                              time_minor=not has_past)
    r3 = lambda a: a.reshape(B, T, a.shape[-1])

    if has_past:
        sb_kt, sb_vt, lat_cache, kr_t_cache = caches
        o_a = _sb_attn(r3(qa), r3(ka), r3(va), sb_kt, sb_vt, layer, wl, tq=256, tk=256, window=512)
        ka_out = ka.reshape(B, T, SB_HEADS, SB_HD)
        va_out = va.reshape(B, T, SB_HEADS, SB_HD)
        kr_out = r3(kr)
    else:
        o_a = _sb_attn(r3(qa), ka, va, None, None, None, wl, tq=256, tk=256, window=None)
        ka_out = jnp.transpose(ka, (0, 3, 1, 2))
        va_out = jnp.transpose(va, (0, 3, 1, 2))
        kr_out = jnp.transpose(kr, (0, 2, 1))

    if gla_s0 is None:
        s0 = jnp.zeros((B, GLA_QK_W, GLA_W), F32)
    else:
        s0 = _state_to_blockdiag(gla_s0)
    o_b, s_bd = _gla(r3(qg), r3(kg), r3(vg), r3(la), r3(rg), s0, wl, tc=128, nb=4)
    gla_s = _state_from_blockdiag(s_bd)

    if has_past:
        o_c = _mla_dec(*q_mla, lat_cache, kr_t_cache, layer, lat, kr, T, wl, kc=1024, nb=2)
    else:
        kcat, v_pairs = _mla_kv(r3(lat), kr, wl, tr=512)
        o_c = _mla_attn(*q_mla, kcat, v_pairs, wl, tq=256)

    x2d = _post(x2d, o_a.reshape(B * T, SB_W), o_b.reshape(B * T, GLA_W), o_c.reshape(B * T, MLA_W),
                mem_k, mem_v, layer if has_past else 0, T, wl, tm=512)
    x2d = _ffn(x2d, wl, tm=512, n_chunks=2)
    return x2d.reshape(B, T, D_MODEL), ka_out, va_out, gla_s, r3(lat), kr_out


def kernel(x_prompt, x_sample, mem_prompt, cache_sb_k, cache_sb_v, state_gla, cache_mla_latent,
           cache_mla_krope, cache_mem_k, cache_mem_v, g_mix_norm, w_in, w_gla_gate, b_gla_gate,
           g_gla_out, g_sb_out, g_cq, w_uq, g_qn, g_qr, g_kr, g_ckv, w_ukv, g_kn, g_mla_out, w_out,
           g_cross_norm, g_mem_norm, w_cq, w_ck, w_cv, g_cqn, g_ckn, w_co, g_ffn_norm, w_gate,
           w_up, w_down):
    p = dict(g_mix_norm=g_mix_norm, w_in=w_in, w_gla_gate=w_gla_gate, b_gla_gate=b_gla_gate,
             g_gla_out=g_gla_out, g_sb_out=g_sb_out, g_cq=g_cq, w_uq=w_uq, g_qn=g_qn, g_qr=g_qr,
             g_kr=g_kr, g_ckv=g_ckv, w_ukv=w_ukv, g_kn=g_kn, g_mla_out=g_mla_out, w_out=w_out,
             g_cross_norm=g_cross_norm, g_mem_norm=g_mem_norm, w_cq=w_cq, w_ck=w_ck, w_cv=w_cv,
             g_cqn=g_cqn, g_ckn=g_ckn, w_co=w_co, g_ffn_norm=g_ffn_norm, w_gate=w_gate, w_up=w_up,
             w_down=w_down)
    depth = w_in.shape[0]
    B = x_prompt.shape[0]
    xp, xs = x_prompt, x_sample
    outs_p = [[] for _ in range(7)]
    outs_s = [[] for _ in range(5)]
    caches = _time_minor_caches(cache_sb_k, cache_sb_v, cache_mla_latent, cache_mla_krope)
    mem_k_cache = cache_mem_k.reshape(depth, -1, MEM_LEN, MEM_W)
    mem_v_cache = cache_mem_v.reshape(depth, -1, MEM_LEN, MEM_W)
    for l in range(depth):
        wl = _prep_layer(p, l)
        mk, mv = _memkv(mem_prompt.reshape(B * MEM_LEN, D_MODEL), wl, tm=512)
        xp, ka, va, st, lat, kr = _trunk_layer(xp, None, l, None, mk.reshape(1, B, MEM_LEN, MEM_W),
                                               mv.reshape(1, B, MEM_LEN, MEM_W), wl)
        mk = mk.reshape(B, MEM_LEN, MEM_HEADS, MEM_HD)
        mv = mv.reshape(B, MEM_LEN, MEM_HEADS, MEM_HD)
        for lst, val in zip(outs_p, (ka, va, st, lat, kr, mk, mv)):
            lst.append(val)
        xs, ka, va, st, lat, kr = _trunk_layer(xs, caches, l, state_gla[l], mem_k_cache, mem_v_cache, wl)
        for lst, val in zip(outs_s, (ka, va, st, lat, kr)):
            lst.append(val)
    return (xp, xs, *[jnp.stack(v) for v in outs_p], *[jnp.stack(v) for v in outs_s])
```

```python
import functools
import math

import jax
import jax.numpy as jnp
import numpy as np
from jax import lax
from jax.experimental import pallas as pl
from jax.experimental.pallas import tpu as pltpu

F32 = jnp.float32
BF16 = jnp.bfloat16

D_MODEL = 1024
CHUNK = 64
SB_HEADS, SB_HD = 4, 64
SB_W = SB_HEADS * SB_HD
GLA_HEADS, GLA_DK, GLA_DV = 4, 32, 64
GLA_QK_W = GLA_HEADS * GLA_DK
GLA_W = GLA_HEADS * GLA_DV
GLA_GATE_RANK = 16
GLA_TAU = 16.0
GLA_BLOCK = 16
MLA_HEADS = 8
MLA_Q_LORA, MLA_KV_LORA = 256, 128
MLA_NOPE, MLA_ROPE, MLA_V = 64, 32, 64
MLA_W = MLA_HEADS * MLA_V
ROPE_THETA = 10000.0
MEM_LEN, MEM_HEADS, MEM_HD = 256, 4, 128
MEM_W = MEM_HEADS * MEM_HD
D_FF = 2816
NORM_EPS = 1e-6

LANES = 128
VMEM_LIMIT = 56 * 1024 * 1024
SB_SCALE = 1.0 / math.sqrt(SB_HD)
GLA_SCALE = GLA_DK ** -0.5
MLA_SCALE = (MLA_NOPE + MLA_ROPE) ** -0.5
MEM_SCALE = MEM_HD ** -0.5
LOG2E = math.log2(math.e)
SB_Q_SCALE = SB_SCALE * LOG2E
MLA_Q_SCALE2 = MLA_SCALE * LOG2E
SB_DEAD_LOG2 = -105.0 * LOG2E
MASK_NEG = -1e30


def _dot(a, b):
    return jnp.dot(a, b, preferred_element_type=F32)


def _dot_nt(a, b):
    return lax.dot_general(a, b, (((1,), (1,)), ((), ())), preferred_element_type=F32)


def _dot_tn(a, b):
    return lax.dot_general(a, b, (((0,), (0,)), ((), ())), preferred_element_type=F32)


def _split(x):
    hi = x.astype(BF16)
    lo = (x - hi.astype(F32)).astype(BF16)
    return hi, lo


def _dot_split_lhs(x, m):
    hi, lo = _split(x)
    return _dot(hi, m) + _dot(lo, m)


def _dot_split_rhs(m, x):
    hi, lo = _split(x)
    return _dot(m, hi) + _dot(m, lo)


def _rms(x, g):
    ms = jnp.mean(x * x, axis=-1, keepdims=True)
    return x * lax.rsqrt(ms + NORM_EPS) * g


def _seg_rms(x, g, seg, width):
    ss = _dot_split_lhs(x * x, seg)
    return x * lax.rsqrt(ss * (1.0 / width) + NORM_EPS) * g


def _softplus(z):
    return jnp.maximum(z, 0.0) + jnp.log1p(jnp.exp(-jnp.abs(z)))


def _softplus2(z2):
    return jnp.maximum(z2, 0.0) + jnp.log2(1.0 + jnp.exp2(-jnp.abs(z2)))


def _sigmoid(z):
    return 1.0 / (1.0 + jnp.exp(-z))


def _swap_halves(x, seg):
    n = x.shape[-1]
    half = seg // 2
    lane = lax.broadcasted_iota(jnp.int32, x.shape, x.ndim - 1)
    first = (lane & (seg - 1)) < half
    return jnp.where(first, pltpu.roll(x, n - half, x.ndim - 1), pltpu.roll(x, half, x.ndim - 1))


def _div_pow2(x, d):
    return lax.shift_right_logical(x, int(math.log2(d)))


def _seg_matrix(n, width):
    i = np.arange(n) // width
    return jnp.asarray(i[:, None] == i[None, :], dtype=BF16)


def _full(shape):
    nd = len(shape)
    return pl.BlockSpec(shape, lambda *_: (0,) * nd)


def _params(sem):
    return pltpu.CompilerParams(dimension_semantics=sem, vmem_limit_bytes=VMEM_LIMIT)


N_INPROJ_INPUTS = 18
N_CACHE_OUTPUTS = 4


def _inproj_kernel(*refs, time_minor, n_prev):
    (x_ref, gmix_ref, wsb_ref, wgl_ref, wag_ref, wml_ref, wgate_ref, bgate_ref, gcq_ref, wuq_ref,
     gqn_ref, gqr_ref, gkr_ref, gckv_ref, seg64_ref, seg32_ref, cos_ref, sin_ref) = refs[:N_INPROJ_INPUTS]
    prev_refs = refs[N_INPROJ_INPUTS:N_INPROJ_INPUTS + N_CACHE_OUTPUTS * n_prev]
    qa_ref, ka_ref, va_ref, qg_ref, kg_ref, vg_ref, la_ref, rg_ref, *mla_refs = refs[N_INPROJ_INPUTS + len(prev_refs):]
    if time_minor:
        qcat_ref, lat_ref, kr_ref = mla_refs
    else:
        qn_ref, qr_ref, lat_ref, kr_ref = mla_refs
    if n_prev:
        cache_refs = (ka_ref, va_ref, lat_ref, kr_ref)
        for j in range(n_prev):
            for out_ref, prev_ref in zip(cache_refs, prev_refs[N_CACHE_OUTPUTS * j:N_CACHE_OUTPUTS * (j + 1)]):
                out_ref[j] = prev_ref[...]
        ka_ref, va_ref, lat_ref, kr_ref = (r.at[n_prev] for r in cache_refs)
    tm = x_ref.shape[0]
    h = _rms(x_ref[...], gmix_ref[...]).astype(BF16)

    ml = _dot(h, wml_ref[...])
    ag = _dot(h, wag_ref[...])
    sb = _dot(h, wsb_ref[...])
    gl = _dot(h, wgl_ref[...])
    cq = ml[:, :MLA_Q_LORA]
    q = _dot(_rms(cq, gcq_ref[...]).astype(BF16), wuq_ref[...])
    gate = _dot(ag.astype(BF16), wgate_ref[...]) + bgate_ref[...]

    qa_ref[...] = (sb[:, :SB_W] * SB_Q_SCALE).astype(BF16)
    ka = sb[:, SB_W:2 * SB_W]
    va = sb[:, 2 * SB_W:]
    if time_minor:
        ka_ref[...] = ka.T.reshape(SB_HEADS, SB_HD, tm)
        va_ref[...] = va.T.reshape(SB_HEADS, SB_HD, tm)
    else:
        ka_ref[...] = ka
        va_ref[...] = va

    qg_ref[...] = gl[:, :GLA_QK_W]
    kg_ref[...] = gl[:, GLA_QK_W:2 * GLA_QK_W]
    vg_ref[...] = gl[:, 2 * GLA_QK_W:2 * GLA_QK_W + GLA_W]
    rg_ref[...] = gl[:, 2 * GLA_QK_W + GLA_W:]
    la_ref[...] = -_softplus(-gate) * (1.0 / GLA_TAU)

    ckv = ml[:, MLA_Q_LORA:MLA_Q_LORA + MLA_KV_LORA]
    krp = ml[:, MLA_Q_LORA + MLA_KV_LORA:]
    lat_ref[...] = _rms(ckv, gckv_ref[...])
    cos = cos_ref[...]
    sin = sin_ref[...]
    kr_ms = jnp.sum(krp * krp, axis=-1, keepdims=True) * (1.0 / MLA_ROPE)
    krn = krp * lax.rsqrt(kr_ms + NORM_EPS) * gkr_ref[...]
    kr_rot = krn * cos[:, :LANES] + _swap_halves(krn, MLA_ROPE) * sin[:, :LANES]
    if time_minor:
        kr_ref[...] = kr_rot.T[:MLA_ROPE, :]
    else:
        kr_ref[...] = kr_rot[:, :MLA_ROPE]

    nope_w = MLA_HEADS * MLA_NOPE
    q_scale = MLA_Q_SCALE2 if time_minor else MLA_SCALE
    qn = _seg_rms(q[:, :nope_w], gqn_ref[...], seg64_ref[...], MLA_NOPE) * q_scale
    qr = _seg_rms(q[:, nope_w:], gqr_ref[...], seg32_ref[...], MLA_ROPE)
    qr = (qr * cos + _swap_halves(qr, MLA_ROPE) * sin) * q_scale
    if not time_minor:
        qn_ref[...] = qn.astype(BF16)
        qr_ref[...] = qr.astype(BF16)
        return
    lane = lax.broadcasted_iota(jnp.int32, (1, LANES), 1)
    rope_heads_per_tile = LANES // MLA_ROPE
    for hd in range(MLA_HEADS):
        pair = qn[:, LANES * (hd // 2):LANES * (hd // 2 + 1)]
        nope = pair if hd % 2 == 0 else pltpu.roll(pair, MLA_NOPE, 1)
        tile = qr[:, LANES * (hd // rope_heads_per_tile):LANES * (hd // rope_heads_per_tile + 1)]
        shift = (MLA_NOPE - MLA_ROPE * (hd % rope_heads_per_tile)) % LANES
        rope = tile if shift == 0 else pltpu.roll(tile, shift, 1)
        qcat_ref[hd] = jnp.where(lane < MLA_NOPE, nope,
                                 jnp.where(lane < MLA_NOPE + MLA_ROPE, rope, 0.0)).astype(BF16)


def _rope_tables(pos):
    half = MLA_ROPE // 2
    freqs = ROPE_THETA ** (-jnp.arange(half, dtype=F32) / half)
    ang = pos.astype(F32)[:, None] * freqs[None, :]
    cos = jnp.cos(ang)
    sin = jnp.sin(ang)
    cos_t = jnp.tile(jnp.concatenate([cos, cos], axis=-1), (1, MLA_HEADS))
    sin_t = jnp.tile(jnp.concatenate([-sin, sin], axis=-1), (1, MLA_HEADS))
    return cos_t, sin_t


def _inproj(x2d, B, T, past, wl, tm, time_minor, prev=()):
    n = x2d.shape[0]
    tm = min(tm, n)
    cos_t, sin_t = _rope_tables(past + jnp.arange(T, dtype=jnp.int32))
    if T < tm:
        cos_t = jnp.tile(cos_t, (tm // T, 1))
        sin_t = jnp.tile(sin_t, (tm // T, 1))
    n_tab = cos_t.shape[0] // tm
    row = lambda w: pl.BlockSpec((tm, w), lambda i: (i, 0))
    tab = pl.BlockSpec((tm, 2 * LANES), lambda i: (i % n_tab, 0))
    weights = (wl['g_mix'], wl['w_sb'], wl['w_gl'], wl['w_ag'], wl['w_ml'], wl['w_gate'], wl['b_gate'],
               wl['g_cq'], wl['w_uq'], wl['g_qn'], wl['g_qr'], wl['g_kr'], wl['g_ckv'],
               wl['seg64_512'], wl['seg32_256'])
    rowout = lambda w, dt: (row(w), jax.ShapeDtypeStruct((n, w), dt))
    n_prev = len(prev)
    prev_specs = []
    if time_minor:
        assert T % tm == 0
        tpb = T // tm
        kv_spec = pl.BlockSpec((None, SB_HEADS, SB_HD, tm), lambda i: (i // tpb, 0, 0, i % tpb))
        kr_spec = pl.BlockSpec((None, MLA_ROPE, tm), lambda i: (i // tpb, 0, i % tpb))
        kv_out = (kv_spec, jax.ShapeDtypeStruct((B, SB_HEADS, SB_HD, T), F32))
        kr_out = (kr_spec, jax.ShapeDtypeStruct((B, MLA_ROPE, T), F32))
        lat_out = rowout(MLA_KV_LORA, F32)
        if n_prev:
            d = n_prev + 1
            prev_specs = [kv_spec, kv_spec, row(MLA_KV_LORA), kr_spec] * n_prev
            kv_out = (pl.BlockSpec((d, None, SB_HEADS, SB_HD, tm), lambda i: (0, i // tpb, 0, 0, i % tpb)),
                      jax.ShapeDtypeStruct((d, B, SB_HEADS, SB_HD, T), F32))
            kr_out = (pl.BlockSpec((d, None, MLA_ROPE, tm), lambda i: (0, i // tpb, 0, i % tpb)),
                      jax.ShapeDtypeStruct((d, B, MLA_ROPE, T), F32))
            lat_out = (pl.BlockSpec((d, tm, MLA_KV_LORA), lambda i: (0, i, 0)),
                       jax.ShapeDtypeStruct((d, n, MLA_KV_LORA), F32))
        q_out = [(pl.BlockSpec((None, MLA_HEADS, tm, LANES), lambda i: (i // tpb, 0, i % tpb, 0)),
                  jax.ShapeDtypeStruct((B, MLA_HEADS, T, LANES), BF16))]
    else:
        assert not n_prev
        kv_out = rowout(SB_W, F32)
        kr_out = rowout(MLA_ROPE, F32)
        lat_out = rowout(MLA_KV_LORA, F32)
        q_out = [rowout(MLA_HEADS * MLA_NOPE, BF16), rowout(MLA_HEADS * MLA_ROPE, BF16)]
    outs = [rowout(SB_W, BF16), kv_out, kv_out,
            rowout(GLA_QK_W, F32), rowout(GLA_QK_W, F32), rowout(GLA_W, F32), rowout(GLA_QK_W, F32),
            rowout(GLA_W, F32), *q_out, lat_out, kr_out]
    return pl.pallas_call(
        functools.partial(_inproj_kernel, time_minor=time_minor, n_prev=n_prev),
        grid=(n // tm,),
        in_specs=[row(D_MODEL)] + [_full(w.shape) for w in weights] + [tab, tab] + prev_specs,
        out_specs=[o[0] for o in outs],
        out_shape=[o[1] for o in outs],
        compiler_params=_params(("parallel",)),
        name="inproj",
    )(x2d, *weights, cos_t, sin_t, *[a for layer_arrays in prev for a in layer_arrays])


def _sb_kernel(*refs, tq, tk, has_past, n_far_blocks, layer):
    if has_past:
        (q_ref, kn_ref, vn_ref, kt_ref, vt_ref, kt_all_ref, vt_all_ref, g_ref, seg_ref, o_ref,
         kbuf_ref, vbuf_ref) = refs
    else:
        q_ref, kt_ref, vt_ref, g_ref, seg_ref, o_ref = refs
    qi = pl.program_id(1)
    q = q_ref[...]
    lane = lax.broadcasted_iota(jnp.int32, (1, SB_W), 1)
    hmask = [(lane >= SB_HD * h) & (lane < SB_HD * (h + 1)) for h in range(SB_HEADS)]
    qh = [jnp.where(hmask[h], q, jnp.zeros_like(q)) for h in range(SB_HEADS)]

    def suffix_matrix(n):
        return (lax.broadcasted_iota(jnp.int32, (n, n), 0)
                > lax.broadcasted_iota(jnp.int32, (n, n), 1)).astype(BF16)

    def time_minor_block(ref, start, n):
        return ref[:, :, pl.ds(start, n)].reshape(SB_W, n).astype(BF16)

    def block(kblk, vblk, time_minor, vis, umat, cs, acc):
        heads = range(SB_HEADS)
        zs = [_dot(qh[h], kblk) if time_minor else _dot_nt(qh[h], kblk) for h in heads]
        lks = [-_softplus2(z) for z in zs]
        if vis is not None:
            lks = [jnp.where(vis, lk, 0.0) for lk in lks]
        splits = [_split(lk) for lk in lks]
        laters = [_dot(hi, umat) + _dot(lo, umat) for hi, lo in splits]
        ws = [jnp.exp2(zs[h] + lks[h] + laters[h] + cs[h]) for h in heads]
        if vis is not None:
            ws = [jnp.where(vis, w, 0.0) for w in ws]
        ws = [w.astype(BF16) for w in ws]
        pvs = [_dot_nt(w, vblk) if time_minor else _dot(w, vblk) for w in ws]
        for h in heads:
            acc = acc + jnp.where(hmask[h], pvs[h], 0.0)
        new_cs = [cs[h] + laters[h][:, :1] + lks[h][:, :1] for h in heads]
        return new_cs, acc

    def dead(cs):
        m = cs[0]
        for c in cs[1:]:
            m = jnp.maximum(m, c)
        return jnp.max(m)

    vis = (lax.broadcasted_iota(jnp.int32, (tq, tq), 1)
           < lax.broadcasted_iota(jnp.int32, (tq, tq), 0))
    zero_c = jnp.zeros((tq, 1), F32)
    init = ([zero_c] * SB_HEADS, jnp.zeros((tq, SB_W), F32))
    if has_past:
        cs, acc = block(kn_ref[...].astype(BF16), vn_ref[...].astype(BF16), False, vis,
                        suffix_matrix(tq), *init)
        kb0 = jnp.int32(kt_ref.shape[-1] // tk - 1)
    else:
        q0 = pl.multiple_of(qi * tq, tq)
        cs, acc = block(time_minor_block(kt_ref, q0, tq), time_minor_block(vt_ref, q0, tq), True, vis,
                        suffix_matrix(tq), *init)
        kb0 = qi - 1

    umat = suffix_matrix(tk)

    def walk(fetch, kb0, cs, acc):
        def cond(st):
            return jnp.logical_and(st[0] >= 0, st[1] > SB_DEAD_LOG2)

        def body(st):
            kb = st[0]
            cs, acc = list(st[2:2 + SB_HEADS]), st[2 + SB_HEADS]
            kblk, vblk = fetch(pl.multiple_of(kb * tk, tk))
            cs, acc = block(kblk, vblk, True, None, umat, cs, acc)
            return (kb - 1, dead(cs), *cs, acc)

        st = lax.while_loop(cond, body, (kb0, dead(cs), *cs, acc))
        return list(st[2:2 + SB_HEADS]), st[2 + SB_HEADS]

    cs, acc = walk(lambda start: (time_minor_block(kt_ref, start, tk), time_minor_block(vt_ref, start, tk)),
                   kb0, cs, acc)
    if has_past and n_far_blocks:
        b = pl.program_id(0)

        def fetch_far(start):
            pltpu.sync_copy(kt_all_ref.at[layer, b, :, :, pl.ds(start, tk)], kbuf_ref)
            pltpu.sync_copy(vt_all_ref.at[layer, b, :, :, pl.ds(start, tk)], vbuf_ref)
            return (kbuf_ref[...].reshape(SB_W, tk).astype(BF16), vbuf_ref[...].reshape(SB_W, tk).astype(BF16))

        cs, acc = walk(fetch_far, jnp.int32(n_far_blocks - 1), cs, acc)
    o_ref[...] = _seg_rms(acc, g_ref[...], seg_ref[...], SB_HD).astype(BF16)


def _sb_attn(q, k_new, v_new, kt_cache, vt_cache, layer, wl, tq, tk, window):
    B, T, _ = q.shape
    tq = min(tq, T)
    has_past = kt_cache is not None
    n_far_blocks = 0
    scratch = []
    in_specs = [pl.BlockSpec((None, tq, SB_W), lambda b, i: (b, i, 0))]
    if has_past:
        P = kt_cache.shape[-1]
        window = min(window, P)
        assert tq == T and P % window == 0 and window % tk == 0
        n_far_blocks = (P - window) // tk
        new = pl.BlockSpec((None, T, SB_W), lambda b, i: (b, 0, 0))
        near = pl.BlockSpec((None, None, SB_HEADS, SB_HD, window), lambda b, i: (layer, b, 0, 0, P // window - 1))
        hbm = pl.BlockSpec(memory_space=pl.ANY)
        in_specs += [new, new, near, near, hbm, hbm]
        args = [q, k_new, v_new, kt_cache, vt_cache, kt_cache, vt_cache]
        scratch = [pltpu.VMEM((SB_HEADS, SB_HD, tk), F32)] * 2
    else:
        assert tq == tk
        if k_new.ndim == 4:
            new = pl.BlockSpec((None, SB_HEADS, SB_HD, T), lambda b, i: (b, 0, 0, 0))
        else:
            new = pl.BlockSpec((None, None, SB_HEADS, SB_HD, T), lambda b, i: (layer, b, 0, 0, 0))
        in_specs += [new, new]
        args = [q, k_new, v_new]
    in_specs += [_full((1, SB_W)), _full((SB_W, SB_W))]
    args += [wl['g_sb_out'], wl['seg64_256']]
    return pl.pallas_call(
        functools.partial(_sb_kernel, tq=tq, tk=tk, has_past=has_past, n_far_blocks=n_far_blocks, layer=layer),
        grid=(B, T // tq),
        in_specs=in_specs,
        out_specs=pl.BlockSpec((None, tq, SB_W), lambda b, i: (b, i, 0)),
        out_shape=jax.ShapeDtypeStruct((B, T, SB_W), BF16),
        scratch_shapes=scratch,
        compiler_params=_params(("parallel", "parallel")),
        name="sb_attn",
    )(*args)


def _gla_kernel(q_ref, k_ref, v_ref, la_ref, rg_ref, s0_ref, g_ref, seg_ref, o_ref, s_ref, *, tc, nb):
    ci = pl.program_id(1)

    @pl.when(ci == 0)
    def _():
        s_ref[...] = s0_ref[...]

    nsub = tc // GLA_BLOCK
    r_i = lax.broadcasted_iota(jnp.int32, (tc, tc), 0)
    c_i = lax.broadcasted_iota(jnp.int32, (tc, tc), 1)
    same = _div_pow2(r_i, GLA_BLOCK) == _div_pow2(c_i, GLA_BLOCK)
    causal = jnp.logical_and(same, c_i <= r_i)
    causal_b = causal.astype(BF16)
    lane_k = lax.broadcasted_iota(jnp.int32, (1, GLA_QK_W), 1)
    lane_v = lax.broadcasted_iota(jnp.int32, (1, GLA_W), 1)
    kms = [(lane_k >= GLA_DK * h) & (lane_k < GLA_DK * (h + 1)) for h in range(GLA_HEADS)]
    vms = [(lane_v >= GLA_DV * h) & (lane_v < GLA_DV * (h + 1)) for h in range(GLA_HEADS)]
    sr = _div_pow2(lax.broadcasted_iota(jnp.int32, (GLA_QK_W, GLA_W), 0), GLA_DK)
    sc = _div_pow2(lax.broadcasted_iota(jnp.int32, (GLA_QK_W, GLA_W), 1), GLA_DV)
    bd = sr == sc
    eye = (lax.broadcasted_iota(jnp.int32, (GLA_QK_W, GLA_QK_W), 0)
           == lax.broadcasted_iota(jnp.int32, (GLA_QK_W, GLA_QK_W), 1))
    g = g_ref[...]
    seg = seg_ref[...]

    rows_b = range(nb)
    heads = range(GLA_HEADS)
    las = [_split(la_ref[bi]) for bi in rows_b]
    bs = [_dot(causal_b, hi) + _dot(causal_b, lo) for hi, lo in las]
    bls = [jnp.broadcast_to(b.reshape(nsub, GLA_BLOCK, GLA_QK_W)[:, GLA_BLOCK - 1:, :],
                            (nsub, GLA_BLOCK, GLA_QK_W)).reshape(tc, GLA_QK_W) for b in bs]
    qts = [(q_ref[bi] * jnp.exp(bs[bi]) * GLA_SCALE).astype(BF16) for bi in rows_b]
    kts = [(k_ref[bi] * jnp.exp(-bs[bi])).astype(BF16) for bi in rows_b]
    kds = [(k_ref[bi] * jnp.exp(bls[bi] - bs[bi])).astype(BF16) for bi in rows_b]
    vbs = [v_ref[bi].astype(BF16) for bi in rows_b]
    decays = [jnp.exp(bl) for bl in bls]

    atts = [[_dot_nt(jnp.where(kms[h], qts[bi], jnp.zeros_like(qts[bi])), kts[bi]) for h in heads]
            for bi in rows_b]
    atts = [[jnp.where(causal, a, 0.0).astype(BF16) for a in row] for row in atts]
    pvs = [[_dot(atts[bi][h], vbs[bi]) for h in heads] for bi in rows_b]
    intras = []
    for bi in rows_b:
        intra = jnp.zeros((tc, GLA_W), F32)
        for h in heads:
            intra = intra + jnp.where(vms[h], pvs[bi][h], 0.0)
        intras.append(intra)

    states = [s_ref[bi] for bi in rows_b]
    inters = [[] for _ in rows_b]
    for j in range(nsub):
        rows = slice(j * GLA_BLOCK, (j + 1) * GLA_BLOCK)
        for bi in rows_b:
            inters[bi].append(_dot(qts[bi][rows], states[bi].astype(BF16)))
        upds = [_dot_tn(kds[bi][rows], vbs[bi][rows]) for bi in rows_b]
        for bi in rows_b:
            d_row = decays[bi][j * GLA_BLOCK:j * GLA_BLOCK + 1]
            d_col = jnp.sum(jnp.where(eye, d_row, 0.0), axis=1, keepdims=True)
            states[bi] = d_col * states[bi] + jnp.where(bd, upds[bi], 0.0)
    os_ = [intras[bi] + jnp.concatenate(inters[bi], axis=0) for bi in rows_b]
    splits = [_split(o * o) for o in os_]
    sss = [_dot(hi, seg) + _dot(lo, seg) for hi, lo in splits]
    for bi in rows_b:
        s_ref[bi] = states[bi]
        o = os_[bi] * lax.rsqrt(sss[bi] * (1.0 / GLA_DV) + NORM_EPS) * g
        rg = rg_ref[bi]
        o_ref[bi] = (o * (rg * _sigmoid(rg))).astype(BF16)


def _gla(qg, kg, vg, la, rg, s0_bd, wl, tc, nb):
    B, T, _ = qg.shape
    tc = min(tc, T)
    nb = min(nb, B)
    blk = lambda w: pl.BlockSpec((nb, tc, w), lambda b, c: (b, c, 0))
    st = pl.BlockSpec((nb, GLA_QK_W, GLA_W), lambda b, c: (b, 0, 0))
    return pl.pallas_call(
        functools.partial(_gla_kernel, tc=tc, nb=nb),
        grid=(B // nb, T // tc),
        in_specs=[blk(GLA_QK_W), blk(GLA_QK_W), blk(GLA_W), blk(GLA_QK_W), blk(GLA_W), st,
                  _full((1, GLA_W)), _full((GLA_W, GLA_W))],
        out_specs=[blk(GLA_W), st],
        out_shape=[jax.ShapeDtypeStruct((B, T, GLA_W), BF16),
                   jax.ShapeDtypeStruct((B, GLA_QK_W, GLA_W), F32)],
        compiler_params=_params(("parallel", "arbitrary")),
        name="gla",
    )(qg, kg, vg, la, rg, s0_bd, wl['g_gla_out'], wl['seg64_256'])


def _state_to_blockdiag(s):
    B = s.shape[0]
    eye = jnp.eye(GLA_HEADS, dtype=s.dtype)
    return jnp.einsum('bhkv,hg->bhkgv', s, eye).reshape(B, GLA_QK_W, GLA_W)


def _state_from_blockdiag(s):
    B = s.shape[0]
    s5 = s.reshape(B, GLA_HEADS, GLA_DK, GLA_HEADS, GLA_DV)
    return jnp.stack([s5[:, h, :, h, :] for h in range(GLA_HEADS)], axis=1)


def _mla_kv_kernel(lat_ref, kr_ref, wn_ref, wvt_ref, gkn_ref, seg_ref, place_ref, kcat_ref, vt_ref):
    tr = lat_ref.shape[0]
    lat = lat_ref[...].astype(BF16)
    kn = _seg_rms(_dot(lat, wn_ref[...]), gkn_ref[...], seg_ref[...], MLA_NOPE)
    vt_ref[...] = _dot_nt(wvt_ref[...], lat).reshape(MLA_HEADS, MLA_V, tr).astype(BF16)
    krp = _dot_tn(kr_ref[...].astype(BF16), place_ref[...])
    lane = lax.broadcasted_iota(jnp.int32, (1, LANES), 1)
    for p in range(MLA_HEADS // 2):
        pair = kn[:, LANES * p:LANES * (p + 1)]
        kcat_ref[2 * p] = jnp.where(lane < MLA_NOPE, pair, krp).astype(BF16)
        kcat_ref[2 * p + 1] = jnp.where(lane < MLA_NOPE, pltpu.roll(pair, MLA_NOPE, 1), krp).astype(BF16)


def _mla_kv(lat, kr_t, layer, wl, tr):
    B, L, _ = lat.shape[-3:]
    tr = min(tr, L)
    if lat.ndim == 3:
        lat_spec = pl.BlockSpec((None, tr, MLA_KV_LORA), lambda b, i: (b, i, 0))
        kr_spec = pl.BlockSpec((None, MLA_ROPE, tr), lambda b, i: (b, 0, i))
    else:
        lat_spec = pl.BlockSpec((None, None, tr, MLA_KV_LORA), lambda b, i: (layer, b, i, 0))
        kr_spec = pl.BlockSpec((None, None, MLA_ROPE, tr), lambda b, i: (layer, b, 0, i))
    return pl.pallas_call(
        _mla_kv_kernel,
        grid=(B, L // tr),
        in_specs=[lat_spec, kr_spec,
                  _full(wl['w_ukn'].shape), _full(wl['w_ukv_v_t'].shape), _full((1, MLA_HEADS * MLA_NOPE)),
                  _full(wl['seg64_512'].shape), _full((MLA_ROPE, LANES))],
        out_specs=[pl.BlockSpec((None, MLA_HEADS, tr, LANES), lambda b, i: (b, 0, i, 0)),
                   pl.BlockSpec((None, MLA_HEADS, MLA_V, tr), lambda b, i: (b, 0, 0, i))],
        out_shape=[jax.ShapeDtypeStruct((B, MLA_HEADS, L, LANES), BF16),
                   jax.ShapeDtypeStruct((B, MLA_HEADS, MLA_V, L), BF16)],
        compiler_params=_params(("parallel", "parallel")),
        name="mla_kv",
    )(lat, kr_t, wl['w_ukn'], wl['w_ukv_v_t'], wl['g_kn'], wl['seg64_512'], wl['rope_place'])


def _mla_attn_kernel(q_ref, k_ref, vt_ref, g_ref, o_ref, *, tq):
    qi = pl.program_id(1)
    q0 = pl.multiple_of(qi * tq, tq)
    vis = (_div_pow2(lax.broadcasted_iota(jnp.int32, (tq, tq), 0), CHUNK)
           <= _div_pow2(lax.broadcasted_iota(jnp.int32, (tq, tq), 1), CHUNK))
    qs = [q_ref[h] for h in range(MLA_HEADS)]

    heads = range(MLA_HEADS)

    def step(start, width, st, mask):
        ss = [_dot_nt(k_ref[h, pl.ds(start, width), :], qs[h]) for h in heads]
        if mask:
            ss = [jnp.where(vis, s, MASK_NEG) for s in ss]
        out = []
        for h in heads:
            bm = jnp.max(ss[h], axis=0, keepdims=True)
            m_new = bm if st is None else jnp.maximum(st[3 * h], bm)
            p = jnp.exp2(ss[h] - m_new)
            l_blk = jnp.sum(p, axis=0, keepdims=True)
            pv = _dot(vt_ref[h, :, pl.ds(start, width)], p.astype(BF16))
            if st is None:
                out += [m_new, l_blk, pv]
            else:
                a = jnp.exp2(st[3 * h] - m_new)
                out += [m_new, a * st[3 * h + 1] + l_blk, a * st[3 * h + 2] + pv]
        return tuple(out)

    st = step(q0, tq, None, True)
    st = lax.fori_loop(0, lax.shift_right_logical(qi, 1),
                       lambda kb, st: step(pl.multiple_of(kb * (2 * tq), 2 * tq), 2 * tq, st, False), st)
    st = lax.cond((qi & 1) == 1, lambda st: step(pl.multiple_of((qi - 1) * tq, tq), tq, st, False),
                  lambda st: st, st)
    g = g_ref[...]
    outs = []
    for h in heads:
        o = st[3 * h + 2] / st[3 * h + 1]
        ms = jnp.mean(o * o, axis=0, keepdims=True)
        outs.append(o * lax.rsqrt(ms + NORM_EPS) * g)
    o_ref[...] = jnp.concatenate(outs, axis=0).T.astype(BF16)


def _mla_attn(qcat, kcat, v_t, wl, tq):
    B, _, T, _ = qcat.shape
    tq = min(tq, T)
    return pl.pallas_call(
        functools.partial(_mla_attn_kernel, tq=tq),
        grid=(B, T // tq),
        in_specs=[pl.BlockSpec((None, MLA_HEADS, tq, LANES), lambda b, i: (b, 0, i, 0)),
                  pl.BlockSpec((None, MLA_HEADS, T, LANES), lambda b, i: (b, 0, 0, 0)),
                  pl.BlockSpec((None, MLA_HEADS, MLA_V, T), lambda b, i: (b, 0, 0, 0)),
                  _full((MLA_V, 1))],
        out_specs=pl.BlockSpec((None, tq, MLA_W), lambda b, i: (b, i, 0)),
        out_shape=jax.ShapeDtypeStruct((B, T, MLA_W), BF16),
        compiler_params=_params(("parallel", "parallel")),
        name="mla_attn",
    )(qcat, kcat, v_t, wl['g_mla_out_col'])


def _mla_dec_kernel(qn_ref, qr_ref, lat_ref, krt_ref, latn_ref, krn_ref, wknt_ref, wv_ref, gkn_ref,
                    g_ref, seg_ref, o_ref, s_ref, *, kc):
    nb, P = lat_ref.shape[:2]
    T = qn_ref.shape[0] // nb
    H = MLA_HEADS
    wknt = wknt_ref[...]
    gkn = gkn_ref[...]

    def queries(b):
        rows = slice(b * T, (b + 1) * T)
        qn = (qn_ref[rows, :].astype(F32) * gkn).astype(BF16)
        qabs = jnp.concatenate(
            [_dot(qn[:, MLA_NOPE * h:MLA_NOPE * (h + 1)], wknt[MLA_NOPE * h:MLA_NOPE * (h + 1), :])
             for h in range(H)], axis=0).astype(BF16)
        qr = qr_ref[rows, :]
        qrs = jnp.concatenate([qr[:, MLA_ROPE * h:MLA_ROPE * (h + 1)] for h in range(H)], axis=0)
        return qabs, qrs

    def scores(qabs, lat_b, sr, n):
        kn_t = _dot_nt(wknt, lat_b)
        ss = jnp.sum((kn_t * kn_t).reshape(H, MLA_NOPE, n), axis=1)
        r = lax.rsqrt(ss * (1.0 / MLA_NOPE) + NORM_EPS)
        sn = _dot_nt(qabs, lat_b).reshape(H, T, n)
        return (sn * r[:, None, :]).reshape(H * T, n) + sr

    def all_scores(b, qabs, qrs):
        m = None
        for c in range(P // kc):
            cols = slice(c * kc, (c + 1) * kc)
            s = scores(qabs, lat_ref[b, cols, :].astype(BF16), _dot(qrs, krt_ref[b, :, cols].astype(BF16)), kc)
            s_ref[b, :, cols] = s
            mc = jnp.max(s, axis=-1, keepdims=True)
            m = mc if m is None else jnp.maximum(m, mc)
        rows = slice(b * T, (b + 1) * T)
        s_new = scores(qabs, latn_ref[rows, :].astype(BF16), _dot_nt(qrs, krn_ref[rows, :].astype(BF16)), T)
        return jnp.maximum(m, jnp.max(s_new, axis=-1, keepdims=True)), s_new

    def context(b, m, s_new):
        p = jnp.exp(s_new - m)
        l = jnp.sum(p, axis=-1, keepdims=True)
        ctx = _dot(p.astype(BF16), latn_ref[b * T:(b + 1) * T, :].astype(BF16))
        for c in range(P // kc):
            cols = slice(c * kc, (c + 1) * kc)
            p = jnp.exp(s_ref[b, :, cols] - m)
            l = l + jnp.sum(p, axis=-1, keepdims=True)
            ctx = ctx + _dot(p.astype(BF16), lat_ref[b, cols, :].astype(BF16))
        return (ctx / l).astype(BF16)

    qs = [queries(b) for b in range(nb)]
    stats = [all_scores(b, *qs[b]) for b in range(nb)]
    ctxs = [context(b, *stats[b]) for b in range(nb)]

    wv = wv_ref[...]
    lane = lax.broadcasted_iota(jnp.int32, (1, MLA_W), 1)
    for b in range(nb):
        o = jnp.zeros((T, MLA_W), F32)
        for h in range(H):
            oh = _dot(ctxs[b][T * h:T * (h + 1)], wv)
            o = jnp.where((lane >= MLA_V * h) & (lane < MLA_V * (h + 1)), oh, o)
        o_ref[b * T:(b + 1) * T, :] = _seg_rms(o, g_ref[...], seg_ref[...], MLA_V).astype(BF16)


def _mla_dec(qn, qr, lat_cache, kr_t_cache, layer, lat_new, kr_new, T, wl, kc, nb):
    n = qn.shape[0]
    B = n // T
    P = lat_cache.shape[2]
    row = lambda w: pl.BlockSpec((nb * T, w), lambda b: (b, 0))
    return pl.pallas_call(
        functools.partial(_mla_dec_kernel, kc=kc),
        grid=(B // nb,),
        in_specs=[row(MLA_HEADS * MLA_NOPE), row(MLA_HEADS * MLA_ROPE),
                  pl.BlockSpec((None, nb, P, MLA_KV_LORA), lambda b: (layer, b, 0, 0)),
                  pl.BlockSpec((None, nb, MLA_ROPE, P), lambda b: (layer, b, 0, 0)),
                  row(MLA_KV_LORA), row(MLA_ROPE),
                  _full(wl['w_ukn_t'].shape), _full(wl['w_ukv_v'].shape), _full((1, MLA_HEADS * MLA_NOPE)),
                  _full((1, MLA_W)), _full((MLA_W, MLA_W))],
        out_specs=row(MLA_W),
        out_shape=jax.ShapeDtypeStruct((n, MLA_W), BF16),
        scratch_shapes=[pltpu.VMEM((nb, MLA_HEADS * T, P), F32)],
        compiler_params=_params(("parallel",)),
        name="mla_dec",
    )(qn, qr, lat_cache, kr_t_cache, lat_new, kr_new, wl['w_ukn_t'], wl['w_ukv_v'], wl['g_kn'],
      wl['g_mla_out_512'], wl['seg64_512'])


def _post_kernel(x_ref, oa_ref, ob_ref, oc_ref, mk_ref, mv_ref, wo_ref, gcross_ref, wcq_ref, gcqn_ref,
                 wco_ref, o_ref, att_ref, *, nb, tt):
    wo = wo_ref
    mix = (_dot(oa_ref[...], wo[:SB_W, :]) + _dot(ob_ref[...], wo[SB_W:SB_W + GLA_W, :])
           + _dot(oc_ref[...], wo[SB_W + GLA_W:, :]))
    x1 = x_ref[...] + mix
    h = _rms(x1, gcross_ref[...]).astype(BF16)
    q = _dot(h, wcq_ref[...])
    gq = gcqn_ref[...]
    qs = []
    for hd in range(MEM_HEADS):
        qh = _rms(q[:, MEM_HD * hd:MEM_HD * (hd + 1)], gq).astype(BF16)
        qs.append(qh)
    pairs = [(bi, hd) for bi in range(nb) for hd in range(MEM_HEADS)]
    rows = lambda bi: slice(bi * tt, (bi + 1) * tt)
    cols = lambda hd: slice(MEM_HD * hd, MEM_HD * (hd + 1))
    ss = [_dot_nt(qs[hd][rows(bi)], mk_ref[bi, :, cols(hd)].astype(BF16)) * MEM_SCALE for bi, hd in pairs]
    ps, ls = [], []
    for s in ss:
        p = jnp.exp(s - jnp.max(s, axis=-1, keepdims=True))
        ls.append(jnp.sum(p, axis=-1, keepdims=True))
        ps.append(p.astype(BF16))
    pvs = [_dot(ps[i], mv_ref[bi, :, cols(hd)].astype(BF16)) for i, (bi, hd) in enumerate(pairs)]
    for i, (bi, hd) in enumerate(pairs):
        att_ref[rows(bi), cols(hd)] = pvs[i] / ls[i]
    o_ref[...] = x1 + _dot(att_ref[...].astype(BF16), wco_ref[...])


def _post(x2d, oa, ob, oc, mem_k, mem_v, layer, T, wl, tm):
    n = x2d.shape[0]
    tm = min(tm, n)
    tt = min(T, tm)
    nb = tm // tt
    tiles_per_b = T // tt
    row = lambda w: pl.BlockSpec((tm, w), lambda i: (i, 0))
    mem = pl.BlockSpec((None, nb, MEM_LEN, MEM_W),
                       lambda i: (layer, i // tiles_per_b if nb == 1 else i, 0, 0))
    weights = (wl['w_out'], wl['g_cross'], wl['w_cq'], wl['g_cqn'], wl['w_co'])
    return pl.pallas_call(
        functools.partial(_post_kernel, nb=nb, tt=tt),
        grid=(n // tm,),
        in_specs=[row(D_MODEL), row(SB_W), row(GLA_W), row(MLA_W), mem, mem] + [_full(w.shape) for w in weights],
        out_specs=row(D_MODEL),
        out_shape=jax.ShapeDtypeStruct((n, D_MODEL), F32),
        scratch_shapes=[pltpu.VMEM((tm, MEM_W), F32)],
        compiler_params=_params(("parallel",)),
        name="post",
    )(x2d, oa, ob, oc, mem_k, mem_v, *weights)


def _ffn_kernel(x_ref, g_ref, wg_ref, wu_ref, wd_ref, o_ref, *, n_chunks):
    x = x_ref[...]
    h = _rms(x, g_ref[...]).astype(BF16)
    tf = wg_ref.shape[1] // n_chunks
    acc = x
    for c in range(n_chunks):
        cols = slice(c * tf, (c + 1) * tf)
        gate = _dot(h, wg_ref[:, cols])
        up = _dot(h, wu_ref[:, cols])
        act = (gate * _sigmoid(gate) * up).astype(BF16)
        acc = acc + _dot(act, wd_ref[cols, :])
    o_ref[...] = acc


def _ffn(x2d, wl, tm, n_chunks):
    n = x2d.shape[0]
    tm = min(tm, n)
    resident = lambda shape: pl.BlockSpec(shape, lambda i: (0, 0), pipeline_mode=pl.Buffered(1))
    return pl.pallas_call(
        functools.partial(_ffn_kernel, n_chunks=n_chunks),
        grid=(n // tm,),
        in_specs=[pl.BlockSpec((tm, D_MODEL), lambda i: (i, 0)), _full((1, D_MODEL)),
                  resident((D_MODEL, D_FF)), resident((D_MODEL, D_FF)), resident((D_FF, D_MODEL))],
        out_specs=pl.BlockSpec((tm, D_MODEL), lambda i: (i, 0)),
        out_shape=jax.ShapeDtypeStruct((n, D_MODEL), F32),
        compiler_params=_params(("parallel",)),
        name="ffn",
    )(x2d, wl['g_ffn'], wl['w_gate_ffn'], wl['w_up'], wl['w_down'])


def _memkv_kernel(m_ref, g_ref, wk_ref, wv_ref, gk_ref, k_ref, v_ref):
    h = _rms(m_ref[...], g_ref[...]).astype(BF16)
    k = _dot(h, wk_ref[...])
    gk = gk_ref[...]
    for hd in range(MEM_HEADS):
        cols = slice(MEM_HD * hd, MEM_HD * (hd + 1))
        k_ref[:, cols] = _rms(k[:, cols], gk)
    v_ref[...] = _dot(h, wv_ref[...])


def _memkv(mem2d, wl, tm):
    n = mem2d.shape[0]
    tm = min(tm, n)
    row = lambda w: pl.BlockSpec((tm, w), lambda i: (i, 0))
    weights = (wl['g_mem'], wl['w_ck'], wl['w_cv'], wl['g_ckn'])
    return pl.pallas_call(
        _memkv_kernel,
        grid=(n // tm,),
        in_specs=[row(D_MODEL)] + [_full(w.shape) for w in weights],
        out_specs=[row(MEM_W), row(MEM_W)],
        out_shape=[jax.ShapeDtypeStruct((n, MEM_W), F32)] * 2,
        compiler_params=_params(("parallel",)),
        name="memkv",
    )(mem2d, *weights)


def _prep_layer(p, l):
    w_in = p['w_in'][l]
    o = np.cumsum([0, SB_W, SB_W, SB_W, GLA_QK_W, GLA_QK_W, GLA_W, GLA_GATE_RANK, GLA_W,
                   MLA_Q_LORA, MLA_KV_LORA, MLA_ROPE])
    seg = lambda i, j: w_in[:, o[i]:o[j]]
    pad_cols = lambda w, n: jnp.pad(w, ((0, 0), (0, n - w.shape[1])))
    row = lambda g, reps=1: jnp.tile(g, reps)[None, :].astype(F32)
    w_uq = p['w_uq'][l].reshape(MLA_Q_LORA, MLA_HEADS, MLA_NOPE + MLA_ROPE)
    w_ukv = p['w_ukv'][l].reshape(MLA_KV_LORA, MLA_HEADS, MLA_NOPE + MLA_V)
    place = np.zeros((MLA_ROPE, LANES), np.float32)
    place[np.arange(MLA_ROPE), MLA_NOPE + np.arange(MLA_ROPE)] = 1.0
    return {
        'g_mix': row(p['g_mix_norm'][l]),
        'w_sb': seg(0, 3).astype(BF16),
        'w_gl': jnp.concatenate([seg(3, 6), seg(7, 8)], axis=1).astype(BF16),
        'w_ag': pad_cols(seg(6, 7), LANES).astype(BF16),
        'w_ml': pad_cols(seg(8, 11), 4 * LANES).astype(BF16),
        'w_gate': jnp.pad(p['w_gla_gate'][l], ((0, LANES - GLA_GATE_RANK), (0, 0))).astype(BF16),
        'b_gate': row(p['b_gla_gate'][l]),
        'g_cq': row(p['g_cq'][l]),
        'w_uq': jnp.concatenate([w_uq[:, :, :MLA_NOPE].reshape(MLA_Q_LORA, -1),
                                 w_uq[:, :, MLA_NOPE:].reshape(MLA_Q_LORA, -1)], axis=1).astype(BF16),
        'g_qn': row(p['g_qn'][l], MLA_HEADS),
        'g_qr': row(p['g_qr'][l], MLA_HEADS),
        'g_kr': jnp.pad(p['g_kr'][l], (0, LANES - MLA_ROPE))[None, :],
        'g_ckv': row(p['g_ckv'][l]),
        'w_ukn': w_ukv[:, :, :MLA_NOPE].reshape(MLA_KV_LORA, -1).astype(BF16),
        'w_ukv_v': w_ukv[:, :, MLA_NOPE:].reshape(MLA_KV_LORA, -1).astype(BF16),
        'g_kn': row(p['g_kn'][l], MLA_HEADS),
        'w_ukn_t': w_ukv[:, :, :MLA_NOPE].reshape(MLA_KV_LORA, -1).T.astype(BF16),
        'w_ukv_v_t': w_ukv[:, :, MLA_NOPE:].reshape(MLA_KV_LORA, -1).T.astype(BF16),
        'g_mla_out_512': row(p['g_mla_out'][l], MLA_HEADS),
        'g_mla_out_col': p['g_mla_out'][l][:, None].astype(F32),
        'g_sb_out': row(p['g_sb_out'][l], SB_HEADS),
        'g_gla_out': row(p['g_gla_out'][l], GLA_HEADS),
        'w_out': p['w_out'][l].astype(BF16),
        'g_cross': row(p['g_cross_norm'][l]),
        'g_mem': row(p['g_mem_norm'][l]),
        'w_cq': p['w_cq'][l].astype(BF16),
        'w_ck': p['w_ck'][l].astype(BF16),
        'w_cv': p['w_cv'][l].astype(BF16),
        'g_cqn': row(p['g_cqn'][l]),
        'g_ckn': row(p['g_ckn'][l]),
        'w_co': p['w_co'][l].astype(BF16),
        'g_ffn': row(p['g_ffn_norm'][l]),
        'w_gate_ffn': p['w_gate'][l].astype(BF16),
        'w_up': p['w_up'][l].astype(BF16),
        'w_down': p['w_down'][l].astype(BF16),
        'seg64_512': _seg_matrix(512, 64),
        'seg32_256': _seg_matrix(256, 32),
        'seg64_256': _seg_matrix(256, 64),
        'rope_place': jnp.asarray(place, dtype=BF16),
    }


def _time_minor_caches(cache_sb_k, cache_sb_v, cache_mla_latent, cache_mla_krope):
    return (jnp.transpose(cache_sb_k, (0, 1, 3, 4, 2)), jnp.transpose(cache_sb_v, (0, 1, 3, 4, 2)),
            cache_mla_latent, jnp.transpose(cache_mla_krope, (0, 1, 3, 2)))


def _trunk_layer(x, caches, layer, gla_s0, mem_k, mem_v, wl, prev=()):
    B, T, _ = x.shape
    has_past = caches is not None
    past = caches[0].shape[-1] if has_past else 0
    x2d = x.reshape(B * T, D_MODEL)
    qa, ka, va, qg, kg, vg, la, rg, *q_mla, lat, kr = _inproj(x2d, B, T, past, wl, tm=512,
                                                              time_minor=not has_past, prev=prev)
    r3 = lambda a: a.reshape(B, T, a.shape[-1])

    if has_past:
        sb_kt, sb_vt, lat_cache, kr_t_cache = caches
        o_a = _sb_attn(r3(qa), r3(ka), r3(va), sb_kt, sb_vt, layer, wl, tq=256, tk=256, window=512)
    else:
        o_a = _sb_attn(r3(qa), ka, va, None, None, len(prev), wl, tq=256, tk=256, window=None)

    if gla_s0 is None:
        s0 = jnp.zeros((B, GLA_QK_W, GLA_W), F32)
    else:
        s0 = _state_to_blockdiag(gla_s0)
    o_b, s_bd = _gla(r3(qg), r3(kg), r3(vg), r3(la), r3(rg), s0, wl, tc=128, nb=4)
    gla_s = _state_from_blockdiag(s_bd)

    if has_past:
        o_c = _mla_dec(*q_mla, lat_cache, kr_t_cache, layer, lat, kr, T, wl, kc=1024, nb=2)
    else:
        lat_b = lat.reshape(lat.shape[:-2] + (B, T, MLA_KV_LORA))
        kcat, v_t = _mla_kv(lat_b, kr, len(prev), wl, tr=512)
        o_c = _mla_attn(*q_mla, kcat, v_t, wl, tq=256)

    x2d = _post(x2d, o_a.reshape(B * T, SB_W), o_b.reshape(B * T, GLA_W), o_c.reshape(B * T, MLA_W),
                mem_k, mem_v, layer if has_past else 0, T, wl, tm=512)
    x2d = _ffn(x2d, wl, tm=512, n_chunks=2)
    return x2d.reshape(B, T, D_MODEL), gla_s, (ka, va, lat, kr)


def kernel(x_prompt, x_sample, mem_prompt, cache_sb_k, cache_sb_v, state_gla, cache_mla_latent,
           cache_mla_krope, cache_mem_k, cache_mem_v, g_mix_norm, w_in, w_gla_gate, b_gla_gate,
           g_gla_out, g_sb_out, g_cq, w_uq, g_qn, g_qr, g_kr, g_ckv, w_ukv, g_kn, g_mla_out, w_out,
           g_cross_norm, g_mem_norm, w_cq, w_ck, w_cv, g_cqn, g_ckn, w_co, g_ffn_norm, w_gate,
           w_up, w_down):
    p = dict(g_mix_norm=g_mix_norm, w_in=w_in, w_gla_gate=w_gla_gate, b_gla_gate=b_gla_gate,
             g_gla_out=g_gla_out, g_sb_out=g_sb_out, g_cq=g_cq, w_uq=w_uq, g_qn=g_qn, g_qr=g_qr,
             g_kr=g_kr, g_ckv=g_ckv, w_ukv=w_ukv, g_kn=g_kn, g_mla_out=g_mla_out, w_out=w_out,
             g_cross_norm=g_cross_norm, g_mem_norm=g_mem_norm, w_cq=w_cq, w_ck=w_ck, w_cv=w_cv,
             g_cqn=g_cqn, g_ckn=g_ckn, w_co=w_co, g_ffn_norm=g_ffn_norm, w_gate=w_gate, w_up=w_up,
             w_down=w_down)
    depth = w_in.shape[0]
    B, T = x_prompt.shape[:2]
    Bs, Ts = x_sample.shape[:2]
    xp, xs = x_prompt, x_sample
    states_p, mks, mvs, prev_p = [], [], [], []
    states_s, kas, vas, lats, krs = [], [], [], [], []
    caches = _time_minor_caches(cache_sb_k, cache_sb_v, cache_mla_latent, cache_mla_krope)
    mem_k_cache = cache_mem_k.reshape(depth, -1, MEM_LEN, MEM_W)
    mem_v_cache = cache_mem_v.reshape(depth, -1, MEM_LEN, MEM_W)
    for l in range(depth):
        wl = _prep_layer(p, l)
        mk, mv = _memkv(mem_prompt.reshape(B * MEM_LEN, D_MODEL), wl, tm=512)
        xp, st, new_p = _trunk_layer(xp, None, l, None, mk.reshape(1, B, MEM_LEN, MEM_W),
                                     mv.reshape(1, B, MEM_LEN, MEM_W), wl,
                                     prev=tuple(prev_p) if l == depth - 1 else ())
        prev_p.append(new_p)
        states_p.append(st)
        mks.append(mk.reshape(B, MEM_LEN, MEM_HEADS, MEM_HD))
        mvs.append(mv.reshape(B, MEM_LEN, MEM_HEADS, MEM_HD))
        xs, st, (ka, va, lat, kr) = _trunk_layer(xs, caches, l, state_gla[l], mem_k_cache, mem_v_cache, wl)
        states_s.append(st)
        kas.append(ka.reshape(Bs, Ts, SB_HEADS, SB_HD))
        vas.append(va.reshape(Bs, Ts, SB_HEADS, SB_HD))
        lats.append(lat.reshape(Bs, Ts, MLA_KV_LORA))
        krs.append(kr.reshape(Bs, Ts, MLA_ROPE))
    ka, va, lat, kr = prev_p[-1]
    if depth == 1:
        ka, va, lat, kr = ka[None], va[None], lat[None], kr[None]
    stack = jnp.stack
    return (xp, xs, jnp.transpose(ka, (0, 1, 4, 2, 3)), jnp.transpose(va, (0, 1, 4, 2, 3)), stack(states_p),
            lat.reshape(depth, B, T, MLA_KV_LORA), jnp.transpose(kr, (0, 1, 3, 2)), stack(mks), stack(mvs),
            stack(kas), stack(vas), stack(states_s), stack(lats), stack(krs))
```

```python
import functools
import math

import jax
import jax.numpy as jnp
import numpy as np
from jax import lax
from jax.experimental import pallas as pl
from jax.experimental.pallas import tpu as pltpu

F32 = jnp.float32
BF16 = jnp.bfloat16

D_MODEL = 1024
CHUNK = 64
SB_HEADS, SB_HD = 4, 64
SB_W = SB_HEADS * SB_HD
GLA_HEADS, GLA_DK, GLA_DV = 4, 32, 64
GLA_QK_W = GLA_HEADS * GLA_DK
GLA_W = GLA_HEADS * GLA_DV
GLA_GATE_RANK = 16
GLA_TAU = 16.0
GLA_BLOCK = 16
MLA_HEADS = 8
MLA_Q_LORA, MLA_KV_LORA = 256, 128
MLA_NOPE, MLA_ROPE, MLA_V = 64, 32, 64
MLA_W = MLA_HEADS * MLA_V
ROPE_THETA = 10000.0
MEM_LEN, MEM_HEADS, MEM_HD = 256, 4, 128
MEM_W = MEM_HEADS * MEM_HD
D_FF = 2816
NORM_EPS = 1e-6

LANES = 128
VMEM_LIMIT = 56 * 1024 * 1024
SB_SCALE = 1.0 / math.sqrt(SB_HD)
GLA_SCALE = GLA_DK ** -0.5
MLA_SCALE = (MLA_NOPE + MLA_ROPE) ** -0.5
MEM_SCALE = MEM_HD ** -0.5
LOG2E = math.log2(math.e)
SB_Q_SCALE = SB_SCALE * LOG2E
MLA_Q_SCALE2 = MLA_SCALE * LOG2E
SB_DEAD_LOG2 = -105.0 * LOG2E
MASK_NEG = -1e30


def _dot(a, b):
    return jnp.dot(a, b, preferred_element_type=F32)


def _dot_nt(a, b):
    return lax.dot_general(a, b, (((1,), (1,)), ((), ())), preferred_element_type=F32)


def _dot_tn(a, b):
    return lax.dot_general(a, b, (((0,), (0,)), ((), ())), preferred_element_type=F32)


def _split(x):
    hi = x.astype(BF16)
    lo = (x - hi.astype(F32)).astype(BF16)
    return hi, lo


def _dot_split_lhs(x, m):
    hi, lo = _split(x)
    return _dot(hi, m) + _dot(lo, m)


def _dot_split_rhs(m, x):
    hi, lo = _split(x)
    return _dot(m, hi) + _dot(m, lo)


def _rms(x, g):
    ms = jnp.mean(x * x, axis=-1, keepdims=True)
    return x * lax.rsqrt(ms + NORM_EPS) * g


def _seg_rms(x, g, seg, width):
    ss = _dot_split_lhs(x * x, seg)
    return x * lax.rsqrt(ss * (1.0 / width) + NORM_EPS) * g


def _softplus(z):
    return jnp.maximum(z, 0.0) + jnp.log1p(jnp.exp(-jnp.abs(z)))


def _softplus2(z2):
    return jnp.maximum(z2, 0.0) + jnp.log2(1.0 + jnp.exp2(-jnp.abs(z2)))


def _sigmoid(z):
    return 1.0 / (1.0 + jnp.exp(-z))


def _swap_halves(x, seg):
    n = x.shape[-1]
    half = seg // 2
    lane = lax.broadcasted_iota(jnp.int32, x.shape, x.ndim - 1)
    first = (lane & (seg - 1)) < half
    return jnp.where(first, pltpu.roll(x, n - half, x.ndim - 1), pltpu.roll(x, half, x.ndim - 1))


def _div_pow2(x, d):
    return lax.shift_right_logical(x, int(math.log2(d)))


def _seg_matrix(n, width):
    i = np.arange(n) // width
    return jnp.asarray(i[:, None] == i[None, :], dtype=BF16)


def _full(shape):
    nd = len(shape)
    return pl.BlockSpec(shape, lambda *_: (0,) * nd)


def _params(sem):
    return pltpu.CompilerParams(dimension_semantics=sem, vmem_limit_bytes=VMEM_LIMIT)


N_INPROJ_INPUTS = 18
N_CACHE_OUTPUTS = 4


def _inproj_kernel(*refs, time_minor, n_prev):
    (x_ref, gmix_ref, wsb_ref, wgl_ref, wag_ref, wml_ref, wgate_ref, bgate_ref, gcq_ref, wuq_ref,
     gqn_ref, gqr_ref, gkr_ref, gckv_ref, seg64_ref, seg32_ref, cos_ref, sin_ref) = refs[:N_INPROJ_INPUTS]
    prev_refs = refs[N_INPROJ_INPUTS:N_INPROJ_INPUTS + N_CACHE_OUTPUTS * n_prev]
    qa_ref, ka_ref, va_ref, qg_ref, kg_ref, vg_ref, la_ref, rg_ref, *mla_refs = refs[N_INPROJ_INPUTS + len(prev_refs):]
    if time_minor:
        qcat_ref, lat_ref, kr_ref = mla_refs
    else:
        qn_ref, qr_ref, lat_ref, kr_ref = mla_refs
    if n_prev:
        cache_refs = (ka_ref, va_ref, lat_ref, kr_ref)
        for j in range(n_prev):
            for out_ref, prev_ref in zip(cache_refs, prev_refs[N_CACHE_OUTPUTS * j:N_CACHE_OUTPUTS * (j + 1)]):
                out_ref[j] = prev_ref[...]
        ka_ref, va_ref, lat_ref, kr_ref = (r.at[n_prev] for r in cache_refs)
    tm = x_ref.shape[0]
    h = _rms(x_ref[...], gmix_ref[...]).astype(BF16)

    ml = _dot(h, wml_ref[...])
    ag = _dot(h, wag_ref[...])
    sb = _dot(h, wsb_ref[...])
    gl = _dot(h, wgl_ref[...])
    cq = ml[:, :MLA_Q_LORA]
    q = _dot(_rms(cq, gcq_ref[...]).astype(BF16), wuq_ref[...])
    gate = _dot(ag.astype(BF16), wgate_ref[...]) + bgate_ref[...]

    qa_ref[...] = (sb[:, :SB_W] * SB_Q_SCALE).astype(BF16)
    ka = sb[:, SB_W:2 * SB_W]
    va = sb[:, 2 * SB_W:]
    if time_minor:
        ka_ref[...] = ka.T.reshape(SB_HEADS, SB_HD, tm)
        va_ref[...] = va.T.reshape(SB_HEADS, SB_HD, tm)
    else:
        ka_ref[...] = ka
        va_ref[...] = va

    qg_ref[...] = gl[:, :GLA_QK_W]
    kg_ref[...] = gl[:, GLA_QK_W:2 * GLA_QK_W]
    vg_ref[...] = gl[:, 2 * GLA_QK_W:2 * GLA_QK_W + GLA_W]
    rg_ref[...] = gl[:, 2 * GLA_QK_W + GLA_W:]
    la_ref[...] = -_softplus(-gate) * (1.0 / GLA_TAU)

    ckv = ml[:, MLA_Q_LORA:MLA_Q_LORA + MLA_KV_LORA]
    krp = ml[:, MLA_Q_LORA + MLA_KV_LORA:]
    lat_ref[...] = _rms(ckv, gckv_ref[...])
    cos = cos_ref[...]
    sin = sin_ref[...]
    kr_ms = jnp.sum(krp * krp, axis=-1, keepdims=True) * (1.0 / MLA_ROPE)
    krn = krp * lax.rsqrt(kr_ms + NORM_EPS) * gkr_ref[...]
    kr_rot = krn * cos[:, :LANES] + _swap_halves(krn, MLA_ROPE) * sin[:, :LANES]
    if time_minor:
        kr_ref[...] = kr_rot.T[:MLA_ROPE, :]
    else:
        kr_ref[...] = kr_rot[:, :MLA_ROPE]

    nope_w = MLA_HEADS * MLA_NOPE
    q_scale = MLA_Q_SCALE2 if time_minor else MLA_SCALE
    qn = _seg_rms(q[:, :nope_w], gqn_ref[...], seg64_ref[...], MLA_NOPE) * q_scale
    qr = _seg_rms(q[:, nope_w:], gqr_ref[...], seg32_ref[...], MLA_ROPE)
    qr = (qr * cos + _swap_halves(qr, MLA_ROPE) * sin) * q_scale
    if not time_minor:
        qn_ref[...] = qn.astype(BF16)
        qr_ref[...] = qr.astype(BF16)
        return
    lane = lax.broadcasted_iota(jnp.int32, (1, LANES), 1)
    rope_heads_per_tile = LANES // MLA_ROPE
    for hd in range(MLA_HEADS):
        pair = qn[:, LANES * (hd // 2):LANES * (hd // 2 + 1)]
        nope = pair if hd % 2 == 0 else pltpu.roll(pair, MLA_NOPE, 1)
        tile = qr[:, LANES * (hd // rope_heads_per_tile):LANES * (hd // rope_heads_per_tile + 1)]
        shift = (MLA_NOPE - MLA_ROPE * (hd % rope_heads_per_tile)) % LANES
        rope = tile if shift == 0 else pltpu.roll(tile, shift, 1)
        qcat_ref[hd] = jnp.where(lane < MLA_NOPE, nope,
                                 jnp.where(lane < MLA_NOPE + MLA_ROPE, rope, 0.0)).astype(BF16)


def _rope_tables(pos):
    half = MLA_ROPE // 2
    freqs = ROPE_THETA ** (-jnp.arange(half, dtype=F32) / half)
    ang = pos.astype(F32)[:, None] * freqs[None, :]
    cos = jnp.cos(ang)
    sin = jnp.sin(ang)
    cos_t = jnp.tile(jnp.concatenate([cos, cos], axis=-1), (1, MLA_HEADS))
    sin_t = jnp.tile(jnp.concatenate([-sin, sin], axis=-1), (1, MLA_HEADS))
    return cos_t, sin_t


def _inproj(x2d, B, T, past, wl, tm, time_minor, prev=()):
    n = x2d.shape[0]
    tm = min(tm, n)
    cos_t, sin_t = _rope_tables(past + jnp.arange(T, dtype=jnp.int32))
    if T < tm:
        cos_t = jnp.tile(cos_t, (tm // T, 1))
        sin_t = jnp.tile(sin_t, (tm // T, 1))
    n_tab = cos_t.shape[0] // tm
    row = lambda w: pl.BlockSpec((tm, w), lambda i: (i, 0))
    tab = pl.BlockSpec((tm, 2 * LANES), lambda i: (i % n_tab, 0))
    weights = (wl['g_mix'], wl['w_sb'], wl['w_gl'], wl['w_ag'], wl['w_ml'], wl['w_gate'], wl['b_gate'],
               wl['g_cq'], wl['w_uq'], wl['g_qn'], wl['g_qr'], wl['g_kr'], wl['g_ckv'],
               wl['seg64_512'], wl['seg32_256'])
    rowout = lambda w, dt: (row(w), jax.ShapeDtypeStruct((n, w), dt))
    n_prev = len(prev)
    prev_specs = []
    if time_minor:
        assert T % tm == 0
        tpb = T // tm
        kv_spec = pl.BlockSpec((None, SB_HEADS, SB_HD, tm), lambda i: (i // tpb, 0, 0, i % tpb))
        kr_spec = pl.BlockSpec((None, MLA_ROPE, tm), lambda i: (i // tpb, 0, i % tpb))
        kv_out = (kv_spec, jax.ShapeDtypeStruct((B, SB_HEADS, SB_HD, T), F32))
        kr_out = (kr_spec, jax.ShapeDtypeStruct((B, MLA_ROPE, T), F32))
        lat_out = rowout(MLA_KV_LORA, F32)
        if n_prev:
            d = n_prev + 1
            prev_specs = [kv_spec, kv_spec, row(MLA_KV_LORA), kr_spec] * n_prev
            kv_out = (pl.BlockSpec((d, None, SB_HEADS, SB_HD, tm), lambda i: (0, i // tpb, 0, 0, i % tpb)),
                      jax.ShapeDtypeStruct((d, B, SB_HEADS, SB_HD, T), F32))
            kr_out = (pl.BlockSpec((d, None, MLA_ROPE, tm), lambda i: (0, i // tpb, 0, i % tpb)),
                      jax.ShapeDtypeStruct((d, B, MLA_ROPE, T), F32))
            lat_out = (pl.BlockSpec((d, tm, MLA_KV_LORA), lambda i: (0, i, 0)),
                       jax.ShapeDtypeStruct((d, n, MLA_KV_LORA), F32))
        q_out = [(pl.BlockSpec((None, MLA_HEADS, tm, LANES), lambda i: (i // tpb, 0, i % tpb, 0)),
                  jax.ShapeDtypeStruct((B, MLA_HEADS, T, LANES), BF16))]
    else:
        assert not n_prev
        kv_out = rowout(SB_W, F32)
        kr_out = rowout(MLA_ROPE, F32)
        lat_out = rowout(MLA_KV_LORA, F32)
        q_out = [rowout(MLA_HEADS * MLA_NOPE, BF16), rowout(MLA_HEADS * MLA_ROPE, BF16)]
    outs = [rowout(SB_W, BF16), kv_out, kv_out,
            rowout(GLA_QK_W, F32), rowout(GLA_QK_W, F32), rowout(GLA_W, F32), rowout(GLA_QK_W, F32),
            rowout(GLA_W, F32), *q_out, lat_out, kr_out]
    return pl.pallas_call(
        functools.partial(_inproj_kernel, time_minor=time_minor, n_prev=n_prev),
        grid=(n // tm,),
        in_specs=[row(D_MODEL)] + [_full(w.shape) for w in weights] + [tab, tab] + prev_specs,
        out_specs=[o[0] for o in outs],
        out_shape=[o[1] for o in outs],
        compiler_params=_params(("parallel",)),
        name="inproj",
    )(x2d, *weights, cos_t, sin_t, *[a for layer_arrays in prev for a in layer_arrays])


def _sb_kernel(*refs, tq, tk, has_past, n_far_blocks, layer):
    if has_past:
        (q_ref, kn_ref, vn_ref, kt_ref, vt_ref, kt_all_ref, vt_all_ref, g_ref, seg_ref, o_ref,
         kbuf_ref, vbuf_ref) = refs
    else:
        q_ref, kt_ref, vt_ref, g_ref, seg_ref, o_ref = refs
    qi = pl.program_id(1)
    q = q_ref[...]
    lane = lax.broadcasted_iota(jnp.int32, (1, SB_W), 1)
    hmask = [(lane >= SB_HD * h) & (lane < SB_HD * (h + 1)) for h in range(SB_HEADS)]
    qh = [jnp.where(hmask[h], q, jnp.zeros_like(q)) for h in range(SB_HEADS)]

    def suffix_matrix(n):
        return (lax.broadcasted_iota(jnp.int32, (n, n), 0)
                > lax.broadcasted_iota(jnp.int32, (n, n), 1)).astype(BF16)

    def time_minor_block(ref, start, n):
        return ref[:, :, pl.ds(start, n)].reshape(SB_W, n).astype(BF16)

    def block(kblk, vblk, time_minor, vis, umat, cs, acc):
        heads = range(SB_HEADS)
        zs = [_dot(qh[h], kblk) if time_minor else _dot_nt(qh[h], kblk) for h in heads]
        lks = [-_softplus2(z) for z in zs]
        if vis is not None:
            lks = [jnp.where(vis, lk, 0.0) for lk in lks]
        splits = [_split(lk) for lk in lks]
        laters = [_dot(hi, umat) + _dot(lo, umat) for hi, lo in splits]
        ws = [jnp.exp2(zs[h] + lks[h] + laters[h] + cs[h]) for h in heads]
        if vis is not None:
            ws = [jnp.where(vis, w, 0.0) for w in ws]
        ws = [w.astype(BF16) for w in ws]
        pvs = [_dot_nt(w, vblk) if time_minor else _dot(w, vblk) for w in ws]
        for h in heads:
            acc = acc + jnp.where(hmask[h], pvs[h], 0.0)
        new_cs = [cs[h] + laters[h][:, :1] + lks[h][:, :1] for h in heads]
        return new_cs, acc

    def dead(cs):
        m = cs[0]
        for c in cs[1:]:
            m = jnp.maximum(m, c)
        return jnp.max(m)

    vis = (lax.broadcasted_iota(jnp.int32, (tq, tq), 1)
           < lax.broadcasted_iota(jnp.int32, (tq, tq), 0))
    zero_c = jnp.zeros((tq, 1), F32)
    init = ([zero_c] * SB_HEADS, jnp.zeros((tq, SB_W), F32))
    if has_past:
        cs, acc = block(kn_ref[...].astype(BF16), vn_ref[...].astype(BF16), False, vis,
                        suffix_matrix(tq), *init)
        kb0 = jnp.int32(kt_ref.shape[-1] // tk - 1)
    else:
        q0 = pl.multiple_of(qi * tq, tq)
        cs, acc = block(time_minor_block(kt_ref, q0, tq), time_minor_block(vt_ref, q0, tq), True, vis,
                        suffix_matrix(tq), *init)
        kb0 = qi - 1

    umat = suffix_matrix(tk)

    def walk(fetch, kb0, cs, acc):
        def cond(st):
            return jnp.logical_and(st[0] >= 0, st[1] > SB_DEAD_LOG2)

        def body(st):
            kb = st[0]
            cs, acc = list(st[2:2 + SB_HEADS]), st[2 + SB_HEADS]
            kblk, vblk = fetch(pl.multiple_of(kb * tk, tk))
            cs, acc = block(kblk, vblk, True, None, umat, cs, acc)
            return (kb - 1, dead(cs), *cs, acc)

        st = lax.while_loop(cond, body, (kb0, dead(cs), *cs, acc))
        return list(st[2:2 + SB_HEADS]), st[2 + SB_HEADS]

    cs, acc = walk(lambda start: (time_minor_block(kt_ref, start, tk), time_minor_block(vt_ref, start, tk)),
                   kb0, cs, acc)
    if has_past and n_far_blocks:
        b = pl.program_id(0)

        def fetch_far(start):
            pltpu.sync_copy(kt_all_ref.at[layer, b, :, :, pl.ds(start, tk)], kbuf_ref)
            pltpu.sync_copy(vt_all_ref.at[layer, b, :, :, pl.ds(start, tk)], vbuf_ref)
            return (kbuf_ref[...].reshape(SB_W, tk).astype(BF16), vbuf_ref[...].reshape(SB_W, tk).astype(BF16))

        cs, acc = walk(fetch_far, jnp.int32(n_far_blocks - 1), cs, acc)
    o_ref[...] = _seg_rms(acc, g_ref[...], seg_ref[...], SB_HD).astype(BF16)


def _sb_attn(q, k_new, v_new, kt_cache, vt_cache, layer, wl, tq, tk, window):
    B, T, _ = q.shape
    tq = min(tq, T)
    has_past = kt_cache is not None
    n_far_blocks = 0
    scratch = []
    in_specs = [pl.BlockSpec((None, tq, SB_W), lambda b, i: (b, i, 0))]
    if has_past:
        P = kt_cache.shape[-1]
        window = min(window, P)
        assert tq == T and P % window == 0 and window % tk == 0
        n_far_blocks = (P - window) // tk
        new = pl.BlockSpec((None, T, SB_W), lambda b, i: (b, 0, 0))
        near = pl.BlockSpec((None, None, SB_HEADS, SB_HD, window), lambda b, i: (layer, b, 0, 0, P // window - 1))
        hbm = pl.BlockSpec(memory_space=pl.ANY)
        in_specs += [new, new, near, near, hbm, hbm]
        args = [q, k_new, v_new, kt_cache, vt_cache, kt_cache, vt_cache]
        scratch = [pltpu.VMEM((SB_HEADS, SB_HD, tk), F32)] * 2
    else:
        assert tq == tk
        if k_new.ndim == 4:
            new = pl.BlockSpec((None, SB_HEADS, SB_HD, T), lambda b, i: (b, 0, 0, 0))
        else:
            new = pl.BlockSpec((None, None, SB_HEADS, SB_HD, T), lambda b, i: (layer, b, 0, 0, 0))
        in_specs += [new, new]
        args = [q, k_new, v_new]
    in_specs += [_full((1, SB_W)), _full((SB_W, SB_W))]
    args += [wl['g_sb_out'], wl['seg64_256']]
    return pl.pallas_call(
        functools.partial(_sb_kernel, tq=tq, tk=tk, has_past=has_past, n_far_blocks=n_far_blocks, layer=layer),
        grid=(B, T // tq),
        in_specs=in_specs,
        out_specs=pl.BlockSpec((None, tq, SB_W), lambda b, i: (b, i, 0)),
        out_shape=jax.ShapeDtypeStruct((B, T, SB_W), BF16),
        scratch_shapes=scratch,
        compiler_params=_params(("parallel", "parallel")),
        name="sb_attn",
    )(*args)


def _gla_kernel(q_ref, k_ref, v_ref, la_ref, rg_ref, s0_ref, g_ref, seg_ref, o_ref, s_ref, *, tc, nb):
    ci = pl.program_id(1)

    @pl.when(ci == 0)
    def _():
        s_ref[...] = s0_ref[...]

    nsub = tc // GLA_BLOCK
    r_i = lax.broadcasted_iota(jnp.int32, (tc, tc), 0)
    c_i = lax.broadcasted_iota(jnp.int32, (tc, tc), 1)
    same = _div_pow2(r_i, GLA_BLOCK) == _div_pow2(c_i, GLA_BLOCK)
    causal = jnp.logical_and(same, c_i <= r_i)
    causal_b = causal.astype(BF16)
    lane_k = lax.broadcasted_iota(jnp.int32, (1, GLA_QK_W), 1)
    lane_v = lax.broadcasted_iota(jnp.int32, (1, GLA_W), 1)
    kms = [(lane_k >= GLA_DK * h) & (lane_k < GLA_DK * (h + 1)) for h in range(GLA_HEADS)]
    vms = [(lane_v >= GLA_DV * h) & (lane_v < GLA_DV * (h + 1)) for h in range(GLA_HEADS)]
    sr = _div_pow2(lax.broadcasted_iota(jnp.int32, (GLA_QK_W, GLA_W), 0), GLA_DK)
    sc = _div_pow2(lax.broadcasted_iota(jnp.int32, (GLA_QK_W, GLA_W), 1), GLA_DV)
    bd = sr == sc
    eye = (lax.broadcasted_iota(jnp.int32, (GLA_QK_W, GLA_QK_W), 0)
           == lax.broadcasted_iota(jnp.int32, (GLA_QK_W, GLA_QK_W), 1))
    g = g_ref[...]
    seg = seg_ref[...]

    rows_b = range(nb)
    heads = range(GLA_HEADS)
    las = [_split(la_ref[bi]) for bi in rows_b]
    bs = [_dot(causal_b, hi) + _dot(causal_b, lo) for hi, lo in las]
    bls = [jnp.broadcast_to(b.reshape(nsub, GLA_BLOCK, GLA_QK_W)[:, GLA_BLOCK - 1:, :],
                            (nsub, GLA_BLOCK, GLA_QK_W)).reshape(tc, GLA_QK_W) for b in bs]
    qts = [(q_ref[bi] * jnp.exp(bs[bi]) * GLA_SCALE).astype(BF16) for bi in rows_b]
    kts = [(k_ref[bi] * jnp.exp(-bs[bi])).astype(BF16) for bi in rows_b]
    kds = [(k_ref[bi] * jnp.exp(bls[bi] - bs[bi])).astype(BF16) for bi in rows_b]
    vbs = [v_ref[bi].astype(BF16) for bi in rows_b]
    decays = [jnp.exp(bl) for bl in bls]

    atts = [[_dot_nt(jnp.where(kms[h], qts[bi], jnp.zeros_like(qts[bi])), kts[bi]) for h in heads]
            for bi in rows_b]
    atts = [[jnp.where(causal, a, 0.0).astype(BF16) for a in row] for row in atts]
    pvs = [[_dot(atts[bi][h], vbs[bi]) for h in heads] for bi in rows_b]
    intras = []
    for bi in rows_b:
        intra = jnp.zeros((tc, GLA_W), F32)
        for h in heads:
            intra = intra + jnp.where(vms[h], pvs[bi][h], 0.0)
        intras.append(intra)

    states = [s_ref[bi] for bi in rows_b]
    inters = [[] for _ in rows_b]
    for j in range(nsub):
        rows = slice(j * GLA_BLOCK, (j + 1) * GLA_BLOCK)
        for bi in rows_b:
            inters[bi].append(_dot(qts[bi][rows], states[bi].astype(BF16)))
        upds = [_dot_tn(kds[bi][rows], vbs[bi][rows]) for bi in rows_b]
        for bi in rows_b:
            d_row = decays[bi][j * GLA_BLOCK:j * GLA_BLOCK + 1]
            d_col = jnp.sum(jnp.where(eye, d_row, 0.0), axis=1, keepdims=True)
            states[bi] = d_col * states[bi] + jnp.where(bd, upds[bi], 0.0)
    os_ = [intras[bi] + jnp.concatenate(inters[bi], axis=0) for bi in rows_b]
    splits = [_split(o * o) for o in os_]
    sss = [_dot(hi, seg) + _dot(lo, seg) for hi, lo in splits]
    for bi in rows_b:
        s_ref[bi] = states[bi]
        o = os_[bi] * lax.rsqrt(sss[bi] * (1.0 / GLA_DV) + NORM_EPS) * g
        rg = rg_ref[bi]
        o_ref[bi] = (o * (rg * _sigmoid(rg))).astype(BF16)


def _gla(qg, kg, vg, la, rg, s0_bd, wl, tc, nb):
    B, T, _ = qg.shape
    tc = min(tc, T)
    nb = min(nb, B)
    blk = lambda w: pl.BlockSpec((nb, tc, w), lambda b, c: (b, c, 0))
    st = pl.BlockSpec((nb, GLA_QK_W, GLA_W), lambda b, c: (b, 0, 0))
    return pl.pallas_call(
        functools.partial(_gla_kernel, tc=tc, nb=nb),
        grid=(B // nb, T // tc),
        in_specs=[blk(GLA_QK_W), blk(GLA_QK_W), blk(GLA_W), blk(GLA_QK_W), blk(GLA_W), st,
                  _full((1, GLA_W)), _full((GLA_W, GLA_W))],
        out_specs=[blk(GLA_W), st],
        out_shape=[jax.ShapeDtypeStruct((B, T, GLA_W), BF16),
                   jax.ShapeDtypeStruct((B, GLA_QK_W, GLA_W), F32)],
        compiler_params=_params(("parallel", "arbitrary")),
        name="gla",
    )(qg, kg, vg, la, rg, s0_bd, wl['g_gla_out'], wl['seg64_256'])


def _state_to_blockdiag(s):
    B = s.shape[0]
    eye = jnp.eye(GLA_HEADS, dtype=s.dtype)
    return jnp.einsum('bhkv,hg->bhkgv', s, eye).reshape(B, GLA_QK_W, GLA_W)


def _state_from_blockdiag(s):
    B = s.shape[0]
    s5 = s.reshape(B, GLA_HEADS, GLA_DK, GLA_HEADS, GLA_DV)
    return jnp.stack([s5[:, h, :, h, :] for h in range(GLA_HEADS)], axis=1)


def _mla_kv_kernel(lat_ref, kr_ref, wn_ref, wvt_ref, gkn_ref, seg_ref, place_ref, kcat_ref, vt_ref):
    tr = lat_ref.shape[0]
    lat = lat_ref[...].astype(BF16)
    kn = _seg_rms(_dot(lat, wn_ref[...]), gkn_ref[...], seg_ref[...], MLA_NOPE)
    vt_ref[...] = _dot_nt(wvt_ref[...], lat).reshape(MLA_HEADS, MLA_V, tr).astype(BF16)
    krp = _dot_tn(kr_ref[...].astype(BF16), place_ref[...])
    lane = lax.broadcasted_iota(jnp.int32, (1, LANES), 1)
    for p in range(MLA_HEADS // 2):
        pair = kn[:, LANES * p:LANES * (p + 1)]
        kcat_ref[2 * p] = jnp.where(lane < MLA_NOPE, pair, krp).astype(BF16)
        kcat_ref[2 * p + 1] = jnp.where(lane < MLA_NOPE, pltpu.roll(pair, MLA_NOPE, 1), krp).astype(BF16)


def _mla_kv(lat, kr_t, layer, wl, tr):
    B, L, _ = lat.shape[-3:]
    tr = min(tr, L)
    if lat.ndim == 3:
        lat_spec = pl.BlockSpec((None, tr, MLA_KV_LORA), lambda b, i: (b, i, 0))
        kr_spec = pl.BlockSpec((None, MLA_ROPE, tr), lambda b, i: (b, 0, i))
    else:
        lat_spec = pl.BlockSpec((None, None, tr, MLA_KV_LORA), lambda b, i: (layer, b, i, 0))
        kr_spec = pl.BlockSpec((None, None, MLA_ROPE, tr), lambda b, i: (layer, b, 0, i))
    return pl.pallas_call(
        _mla_kv_kernel,
        grid=(B, L // tr),
        in_specs=[lat_spec, kr_spec,
                  _full(wl['w_ukn'].shape), _full(wl['w_ukv_v_t'].shape), _full((1, MLA_HEADS * MLA_NOPE)),
                  _full(wl['seg64_512'].shape), _full((MLA_ROPE, LANES))],
        out_specs=[pl.BlockSpec((None, MLA_HEADS, tr, LANES), lambda b, i: (b, 0, i, 0)),
                   pl.BlockSpec((None, MLA_HEADS, MLA_V, tr), lambda b, i: (b, 0, 0, i))],
        out_shape=[jax.ShapeDtypeStruct((B, MLA_HEADS, L, LANES), BF16),
                   jax.ShapeDtypeStruct((B, MLA_HEADS, MLA_V, L), BF16)],
        compiler_params=_params(("parallel", "parallel")),
        name="mla_kv",
    )(lat, kr_t, wl['w_ukn'], wl['w_ukv_v_t'], wl['g_kn'], wl['seg64_512'], wl['rope_place'])


def _mla_attn_kernel(q_ref, k_ref, vt_ref, g_ref, o_ref, *, tq):
    qi = pl.program_id(1)
    q0 = pl.multiple_of(qi * tq, tq)
    vis = (_div_pow2(lax.broadcasted_iota(jnp.int32, (tq, tq), 0), CHUNK)
           <= _div_pow2(lax.broadcasted_iota(jnp.int32, (tq, tq), 1), CHUNK))
    qs = [q_ref[h] for h in range(MLA_HEADS)]

    heads = range(MLA_HEADS)

    def step(start, width, st, mask):
        ss = [_dot_nt(k_ref[h, pl.ds(start, width), :], qs[h]) for h in heads]
        if mask:
            ss = [jnp.where(vis, s, MASK_NEG) for s in ss]
        out = []
        for h in heads:
            bm = jnp.max(ss[h], axis=0, keepdims=True)
            m_new = bm if st is None else jnp.maximum(st[3 * h], bm)
            p = jnp.exp2(ss[h] - m_new)
            l_blk = jnp.sum(p, axis=0, keepdims=True)
            pv = _dot(vt_ref[h, :, pl.ds(start, width)], p.astype(BF16))
            if st is None:
                out += [m_new, l_blk, pv]
            else:
                a = jnp.exp2(st[3 * h] - m_new)
                out += [m_new, a * st[3 * h + 1] + l_blk, a * st[3 * h + 2] + pv]
        return tuple(out)

    st = step(q0, tq, None, True)
    st = lax.fori_loop(0, lax.shift_right_logical(qi, 1),
                       lambda kb, st: step(pl.multiple_of(kb * (2 * tq), 2 * tq), 2 * tq, st, False), st)
    st = lax.cond((qi & 1) == 1, lambda st: step(pl.multiple_of((qi - 1) * tq, tq), tq, st, False),
                  lambda st: st, st)
    g = g_ref[...]
    outs = []
    for h in heads:
        o = st[3 * h + 2] / st[3 * h + 1]
        ms = jnp.mean(o * o, axis=0, keepdims=True)
        outs.append(o * lax.rsqrt(ms + NORM_EPS) * g)
    o_ref[...] = jnp.concatenate(outs, axis=0).T.astype(BF16)


def _mla_attn(qcat, kcat, v_t, wl, tq):
    B, _, T, _ = qcat.shape
    tq = min(tq, T)
    return pl.pallas_call(
        functools.partial(_mla_attn_kernel, tq=tq),
        grid=(B, T // tq),
        in_specs=[pl.BlockSpec((None, MLA_HEADS, tq, LANES), lambda b, i: (b, 0, i, 0)),
                  pl.BlockSpec((None, MLA_HEADS, T, LANES), lambda b, i: (b, 0, 0, 0)),
                  pl.BlockSpec((None, MLA_HEADS, MLA_V, T), lambda b, i: (b, 0, 0, 0)),
                  _full((MLA_V, 1))],
        out_specs=pl.BlockSpec((None, tq, MLA_W), lambda b, i: (b, i, 0)),
        out_shape=jax.ShapeDtypeStruct((B, T, MLA_W), BF16),
        compiler_params=_params(("parallel", "parallel")),
        name="mla_attn",
    )(qcat, kcat, v_t, wl['g_mla_out_col'])


def _mla_dec_kernel(qn_ref, qr_ref, lat_ref, krt_ref, latn_ref, krn_ref, wknt_ref, wv_ref, gkn_ref,
                    g_ref, seg_ref, o_ref, s_ref, *, kc):
    nb, P = lat_ref.shape[:2]
    T = qn_ref.shape[0] // nb
    H = MLA_HEADS
    wknt = wknt_ref[...]
    gkn = gkn_ref[...]

    def queries(b):
        rows = slice(b * T, (b + 1) * T)
        qn = (qn_ref[rows, :].astype(F32) * gkn).astype(BF16)
        qabs = jnp.concatenate(
            [_dot(qn[:, MLA_NOPE * h:MLA_NOPE * (h + 1)], wknt[MLA_NOPE * h:MLA_NOPE * (h + 1), :])
             for h in range(H)], axis=0).astype(BF16)
        qr = qr_ref[rows, :]
        qrs = jnp.concatenate([qr[:, MLA_ROPE * h:MLA_ROPE * (h + 1)] for h in range(H)], axis=0)
        return qabs, qrs

    def scores(qabs, lat_b, sr, n):
        kn_t = _dot_nt(wknt, lat_b)
        ss = jnp.sum((kn_t * kn_t).reshape(H, MLA_NOPE, n), axis=1)
        r = lax.rsqrt(ss * (1.0 / MLA_NOPE) + NORM_EPS)
        sn = _dot_nt(qabs, lat_b).reshape(H, T, n)
        return (sn * r[:, None, :]).reshape(H * T, n) + sr

    def all_scores(b, qabs, qrs):
        m = None
        for c in range(P // kc):
            cols = slice(c * kc, (c + 1) * kc)
            s = scores(qabs, lat_ref[b, cols, :].astype(BF16), _dot(qrs, krt_ref[b, :, cols].astype(BF16)), kc)
            s_ref[b, :, cols] = s
            mc = jnp.max(s, axis=-1, keepdims=True)
            m = mc if m is None else jnp.maximum(m, mc)
        rows = slice(b * T, (b + 1) * T)
        s_new = scores(qabs, latn_ref[rows, :].astype(BF16), _dot_nt(qrs, krn_ref[rows, :].astype(BF16)), T)
        return jnp.maximum(m, jnp.max(s_new, axis=-1, keepdims=True)), s_new

    def context(b, m, s_new):
        p = jnp.exp(s_new - m)
        l = jnp.sum(p, axis=-1, keepdims=True)
        ctx = _dot(p.astype(BF16), latn_ref[b * T:(b + 1) * T, :].astype(BF16))
        for c in range(P // kc):
            cols = slice(c * kc, (c + 1) * kc)
            p = jnp.exp(s_ref[b, :, cols] - m)
            l = l + jnp.sum(p, axis=-1, keepdims=True)
            ctx = ctx + _dot(p.astype(BF16), lat_ref[b, cols, :].astype(BF16))
        return (ctx / l).astype(BF16)

    qs = [queries(b) for b in range(nb)]
    stats = [all_scores(b, *qs[b]) for b in range(nb)]
    ctxs = [context(b, *stats[b]) for b in range(nb)]

    wv = wv_ref[...]
    lane = lax.broadcasted_iota(jnp.int32, (1, MLA_W), 1)
    for b in range(nb):
        o = jnp.zeros((T, MLA_W), F32)
        for h in range(H):
            oh = _dot(ctxs[b][T * h:T * (h + 1)], wv)
            o = jnp.where((lane >= MLA_V * h) & (lane < MLA_V * (h + 1)), oh, o)
        o_ref[b * T:(b + 1) * T, :] = _seg_rms(o, g_ref[...], seg_ref[...], MLA_V).astype(BF16)


def _mla_dec(qn, qr, lat_cache, kr_t_cache, layer, lat_new, kr_new, T, wl, kc, nb):
    n = qn.shape[0]
    B = n // T
    P = lat_cache.shape[2]
    row = lambda w: pl.BlockSpec((nb * T, w), lambda b: (b, 0))
    return pl.pallas_call(
        functools.partial(_mla_dec_kernel, kc=kc),
        grid=(B // nb,),
        in_specs=[row(MLA_HEADS * MLA_NOPE), row(MLA_HEADS * MLA_ROPE),
                  pl.BlockSpec((None, nb, P, MLA_KV_LORA), lambda b: (layer, b, 0, 0)),
                  pl.BlockSpec((None, nb, MLA_ROPE, P), lambda b: (layer, b, 0, 0)),
                  row(MLA_KV_LORA), row(MLA_ROPE),
                  _full(wl['w_ukn_t'].shape), _full(wl['w_ukv_v'].shape), _full((1, MLA_HEADS * MLA_NOPE)),
                  _full((1, MLA_W)), _full((MLA_W, MLA_W))],
        out_specs=row(MLA_W),
        out_shape=jax.ShapeDtypeStruct((n, MLA_W), BF16),
        scratch_shapes=[pltpu.VMEM((nb, MLA_HEADS * T, P), F32)],
        compiler_params=_params(("parallel",)),
        name="mla_dec",
    )(qn, qr, lat_cache, kr_t_cache, lat_new, kr_new, wl['w_ukn_t'], wl['w_ukv_v'], wl['g_kn'],
      wl['g_mla_out_512'], wl['seg64_512'])


def _post_kernel(x_ref, oa_ref, ob_ref, oc_ref, mk_ref, mv_ref, wo_ref, gcross_ref, wcq_ref, gcqn_ref,
                 wco_ref, o_ref, att_ref, *, nb, tt):
    wo = wo_ref
    mix = (_dot(oa_ref[...], wo[:SB_W, :]) + _dot(ob_ref[...], wo[SB_W:SB_W + GLA_W, :])
           + _dot(oc_ref[...], wo[SB_W + GLA_W:, :]))
    x1 = x_ref[...] + mix
    h = _rms(x1, gcross_ref[...]).astype(BF16)
    q = _dot(h, wcq_ref[...])
    gq = gcqn_ref[...]
    qs = []
    for hd in range(MEM_HEADS):
        qh = _rms(q[:, MEM_HD * hd:MEM_HD * (hd + 1)], gq).astype(BF16)
        qs.append(qh)
    pairs = [(bi, hd) for bi in range(nb) for hd in range(MEM_HEADS)]
    rows = lambda bi: slice(bi * tt, (bi + 1) * tt)
    cols = lambda hd: slice(MEM_HD * hd, MEM_HD * (hd + 1))
    ss = [_dot_nt(qs[hd][rows(bi)], mk_ref[bi, :, cols(hd)].astype(BF16)) * MEM_SCALE for bi, hd in pairs]
    ps, ls = [], []
    for s in ss:
        p = jnp.exp(s - jnp.max(s, axis=-1, keepdims=True))
        ls.append(jnp.sum(p, axis=-1, keepdims=True))
        ps.append(p.astype(BF16))
    pvs = [_dot(ps[i], mv_ref[bi, :, cols(hd)].astype(BF16)) for i, (bi, hd) in enumerate(pairs)]
    for i, (bi, hd) in enumerate(pairs):
        att_ref[rows(bi), cols(hd)] = pvs[i] / ls[i]
    o_ref[...] = x1 + _dot(att_ref[...].astype(BF16), wco_ref[...])


def _post(x2d, oa, ob, oc, mem_k, mem_v, layer, T, wl, tm):
    n = x2d.shape[0]
    tm = min(tm, n)
    tt = min(T, tm)
    nb = tm // tt
    tiles_per_b = T // tt
    row = lambda w: pl.BlockSpec((tm, w), lambda i: (i, 0))
    mem = pl.BlockSpec((None, nb, MEM_LEN, MEM_W),
                       lambda i: (layer, i // tiles_per_b if nb == 1 else i, 0, 0))
    weights = (wl['w_out'], wl['g_cross'], wl['w_cq'], wl['g_cqn'], wl['w_co'])
    return pl.pallas_call(
        functools.partial(_post_kernel, nb=nb, tt=tt),
        grid=(n // tm,),
        in_specs=[row(D_MODEL), row(SB_W), row(GLA_W), row(MLA_W), mem, mem] + [_full(w.shape) for w in weights],
        out_specs=row(D_MODEL),
        out_shape=jax.ShapeDtypeStruct((n, D_MODEL), F32),
        scratch_shapes=[pltpu.VMEM((tm, MEM_W), F32)],
        compiler_params=_params(("parallel",)),
        name="post",
    )(x2d, oa, ob, oc, mem_k, mem_v, *weights)


def _ffn_kernel(x_ref, g_ref, wg_ref, wu_ref, wd_ref, o_ref, *, n_chunks):
    x = x_ref[...]
    h = _rms(x, g_ref[...]).astype(BF16)
    tf = wg_ref.shape[1] // n_chunks
    acc = x
    for c in range(n_chunks):
        cols = slice(c * tf, (c + 1) * tf)
        gate = _dot(h, wg_ref[:, cols])
        up = _dot(h, wu_ref[:, cols])
        act = (gate * _sigmoid(gate) * up).astype(BF16)
        acc = acc + _dot(act, wd_ref[cols, :])
    o_ref[...] = acc


def _ffn(x2d, wl, tm, n_chunks):
    n = x2d.shape[0]
    tm = min(tm, n)
    resident = lambda shape: pl.BlockSpec(shape, lambda i: (0, 0), pipeline_mode=pl.Buffered(1))
    return pl.pallas_call(
        functools.partial(_ffn_kernel, n_chunks=n_chunks),
        grid=(n // tm,),
        in_specs=[pl.BlockSpec((tm, D_MODEL), lambda i: (i, 0)), _full((1, D_MODEL)),
                  resident((D_MODEL, D_FF)), resident((D_MODEL, D_FF)), resident((D_FF, D_MODEL))],
        out_specs=pl.BlockSpec((tm, D_MODEL), lambda i: (i, 0)),
        out_shape=jax.ShapeDtypeStruct((n, D_MODEL), F32),
        compiler_params=_params(("parallel",)),
        name="ffn",
    )(x2d, wl['g_ffn'], wl['w_gate_ffn'], wl['w_up'], wl['w_down'])


def _memkv_kernel(*refs, n_prev):
    m_ref, g_ref, wk_ref, wv_ref, gk_ref = refs[:5]
    prev_refs = refs[5:5 + 2 * n_prev]
    k_ref, v_ref, k4_ref, v4_ref = refs[5 + 2 * n_prev:]
    nb = k4_ref.shape[1]
    for j in range(n_prev):
        k4_ref[j] = prev_refs[2 * j][...]
        v4_ref[j] = prev_refs[2 * j + 1][...]
    h = _rms(m_ref[...], g_ref[...]).astype(BF16)
    k = _dot(h, wk_ref[...])
    v = _dot(h, wv_ref[...])
    gk = gk_ref[...]
    v_ref[...] = v
    for hd in range(MEM_HEADS):
        cols = slice(MEM_HD * hd, MEM_HD * (hd + 1))
        kh = _rms(k[:, cols], gk)
        k_ref[:, cols] = kh
        k4_ref[n_prev, :, :, hd, :] = kh.reshape(nb, MEM_LEN, MEM_HD)
        v4_ref[n_prev, :, :, hd, :] = v[:, cols].reshape(nb, MEM_LEN, MEM_HD)


def _memkv(mem2d, wl, nb, prev=()):
    B = mem2d.shape[0] // MEM_LEN
    nb = min(nb, B)
    d = len(prev) + 1
    row = lambda w: pl.BlockSpec((nb * MEM_LEN, w), lambda i: (i, 0))
    weights = (wl['g_mem'], wl['w_ck'], wl['w_cv'], wl['g_ckn'])
    prev_spec = pl.BlockSpec((nb, MEM_LEN, MEM_HEADS, MEM_HD), lambda i: (i, 0, 0, 0))
    out4 = (pl.BlockSpec((d, nb, MEM_LEN, MEM_HEADS, MEM_HD), lambda i: (0, i, 0, 0, 0)),
            jax.ShapeDtypeStruct((d, B, MEM_LEN, MEM_HEADS, MEM_HD), F32))
    out2 = (row(MEM_W), jax.ShapeDtypeStruct((B * MEM_LEN, MEM_W), F32))
    outs = [out2, out2, out4, out4]
    return pl.pallas_call(
        functools.partial(_memkv_kernel, n_prev=len(prev)),
        grid=(B // nb,),
        in_specs=[row(D_MODEL)] + [_full(w.shape) for w in weights] + [prev_spec] * (2 * len(prev)),
        out_specs=[o[0] for o in outs],
        out_shape=[o[1] for o in outs],
        compiler_params=_params(("parallel",)),
        name="memkv",
    )(mem2d, *weights, *[a for pair in prev for a in pair])


def _prep_layer(p, l):
    w_in = p['w_in'][l]
    o = np.cumsum([0, SB_W, SB_W, SB_W, GLA_QK_W, GLA_QK_W, GLA_W, GLA_GATE_RANK, GLA_W,
                   MLA_Q_LORA, MLA_KV_LORA, MLA_ROPE])
    seg = lambda i, j: w_in[:, o[i]:o[j]]
    pad_cols = lambda w, n: jnp.pad(w, ((0, 0), (0, n - w.shape[1])))
    row = lambda g, reps=1: jnp.tile(g, reps)[None, :].astype(F32)
    w_uq = p['w_uq'][l].reshape(MLA_Q_LORA, MLA_HEADS, MLA_NOPE + MLA_ROPE)
    w_ukv = p['w_ukv'][l].reshape(MLA_KV_LORA, MLA_HEADS, MLA_NOPE + MLA_V)
    place = np.zeros((MLA_ROPE, LANES), np.float32)
    place[np.arange(MLA_ROPE), MLA_NOPE + np.arange(MLA_ROPE)] = 1.0
    return {
        'g_mix': row(p['g_mix_norm'][l]),
        'w_sb': seg(0, 3).astype(BF16),
        'w_gl': jnp.concatenate([seg(3, 6), seg(7, 8)], axis=1).astype(BF16),
        'w_ag': pad_cols(seg(6, 7), LANES).astype(BF16),
        'w_ml': pad_cols(seg(8, 11), 4 * LANES).astype(BF16),
        'w_gate': jnp.pad(p['w_gla_gate'][l], ((0, LANES - GLA_GATE_RANK), (0, 0))).astype(BF16),
        'b_gate': row(p['b_gla_gate'][l]),
        'g_cq': row(p['g_cq'][l]),
        'w_uq': jnp.concatenate([w_uq[:, :, :MLA_NOPE].reshape(MLA_Q_LORA, -1),
                                 w_uq[:, :, MLA_NOPE:].reshape(MLA_Q_LORA, -1)], axis=1).astype(BF16),
        'g_qn': row(p['g_qn'][l], MLA_HEADS),
        'g_qr': row(p['g_qr'][l], MLA_HEADS),
        'g_kr': jnp.pad(p['g_kr'][l], (0, LANES - MLA_ROPE))[None, :],
        'g_ckv': row(p['g_ckv'][l]),
        'w_ukn': w_ukv[:, :, :MLA_NOPE].reshape(MLA_KV_LORA, -1).astype(BF16),
        'w_ukv_v': w_ukv[:, :, MLA_NOPE:].reshape(MLA_KV_LORA, -1).astype(BF16),
        'g_kn': row(p['g_kn'][l], MLA_HEADS),
        'w_ukn_t': w_ukv[:, :, :MLA_NOPE].reshape(MLA_KV_LORA, -1).T.astype(BF16),
        'w_ukv_v_t': w_ukv[:, :, MLA_NOPE:].reshape(MLA_KV_LORA, -1).T.astype(BF16),
        'g_mla_out_512': row(p['g_mla_out'][l], MLA_HEADS),
        'g_mla_out_col': p['g_mla_out'][l][:, None].astype(F32),
        'g_sb_out': row(p['g_sb_out'][l], SB_HEADS),
        'g_gla_out': row(p['g_gla_out'][l], GLA_HEADS),
        'w_out': p['w_out'][l].astype(BF16),
        'g_cross': row(p['g_cross_norm'][l]),
        'g_mem': row(p['g_mem_norm'][l]),
        'w_cq': p['w_cq'][l].astype(BF16),
        'w_ck': p['w_ck'][l].astype(BF16),
        'w_cv': p['w_cv'][l].astype(BF16),
        'g_cqn': row(p['g_cqn'][l]),
        'g_ckn': row(p['g_ckn'][l]),
        'w_co': p['w_co'][l].astype(BF16),
        'g_ffn': row(p['g_ffn_norm'][l]),
        'w_gate_ffn': p['w_gate'][l].astype(BF16),
        'w_up': p['w_up'][l].astype(BF16),
        'w_down': p['w_down'][l].astype(BF16),
        'seg64_512': _seg_matrix(512, 64),
        'seg32_256': _seg_matrix(256, 32),
        'seg64_256': _seg_matrix(256, 64),
        'rope_place': jnp.asarray(place, dtype=BF16),
    }


def _time_minor_caches(cache_sb_k, cache_sb_v, cache_mla_latent, cache_mla_krope):
    return (jnp.transpose(cache_sb_k, (0, 1, 3, 4, 2)), jnp.transpose(cache_sb_v, (0, 1, 3, 4, 2)),
            cache_mla_latent, jnp.transpose(cache_mla_krope, (0, 1, 3, 2)))


def _trunk_layer(x, caches, layer, gla_s0, mem_k, mem_v, wl, prev=()):
    B, T, _ = x.shape
    has_past = caches is not None
    past = caches[0].shape[-1] if has_past else 0
    x2d = x.reshape(B * T, D_MODEL)
    qa, ka, va, qg, kg, vg, la, rg, *q_mla, lat, kr = _inproj(x2d, B, T, past, wl, tm=512,
                                                              time_minor=not has_past, prev=prev)
    r3 = lambda a: a.reshape(B, T, a.shape[-1])

    if has_past:
        sb_kt, sb_vt, lat_cache, kr_t_cache = caches
        o_a = _sb_attn(r3(qa), r3(ka), r3(va), sb_kt, sb_vt, layer, wl, tq=256, tk=256, window=512)
    else:
        o_a = _sb_attn(r3(qa), ka, va, None, None, len(prev), wl, tq=256, tk=256, window=None)

    if gla_s0 is None:
        s0 = jnp.zeros((B, GLA_QK_W, GLA_W), F32)
    else:
        s0 = _state_to_blockdiag(gla_s0)
    o_b, s_bd = _gla(r3(qg), r3(kg), r3(vg), r3(la), r3(rg), s0, wl, tc=128, nb=4)
    gla_s = _state_from_blockdiag(s_bd)

    if has_past:
        o_c = _mla_dec(*q_mla, lat_cache, kr_t_cache, layer, lat, kr, T, wl, kc=1024, nb=2)
    else:
        lat_b = lat.reshape(lat.shape[:-2] + (B, T, MLA_KV_LORA))
        kcat, v_t = _mla_kv(lat_b, kr, len(prev), wl, tr=512)
        o_c = _mla_attn(*q_mla, kcat, v_t, wl, tq=256)

    x2d = _post(x2d, o_a.reshape(B * T, SB_W), o_b.reshape(B * T, GLA_W), o_c.reshape(B * T, MLA_W),
                mem_k, mem_v, layer if has_past else 0, T, wl, tm=512)
    x2d = _ffn(x2d, wl, tm=512, n_chunks=2)
    return x2d.reshape(B, T, D_MODEL), gla_s, (ka, va, lat, kr)


def kernel(x_prompt, x_sample, mem_prompt, cache_sb_k, cache_sb_v, state_gla, cache_mla_latent,
           cache_mla_krope, cache_mem_k, cache_mem_v, g_mix_norm, w_in, w_gla_gate, b_gla_gate,
           g_gla_out, g_sb_out, g_cq, w_uq, g_qn, g_qr, g_kr, g_ckv, w_ukv, g_kn, g_mla_out, w_out,
           g_cross_norm, g_mem_norm, w_cq, w_ck, w_cv, g_cqn, g_ckn, w_co, g_ffn_norm, w_gate,
           w_up, w_down):
    p = dict(g_mix_norm=g_mix_norm, w_in=w_in, w_gla_gate=w_gla_gate, b_gla_gate=b_gla_gate,
             g_gla_out=g_gla_out, g_sb_out=g_sb_out, g_cq=g_cq, w_uq=w_uq, g_qn=g_qn, g_qr=g_qr,
             g_kr=g_kr, g_ckv=g_ckv, w_ukv=w_ukv, g_kn=g_kn, g_mla_out=g_mla_out, w_out=w_out,
             g_cross_norm=g_cross_norm, g_mem_norm=g_mem_norm, w_cq=w_cq, w_ck=w_ck, w_cv=w_cv,
             g_cqn=g_cqn, g_ckn=g_ckn, w_co=w_co, g_ffn_norm=g_ffn_norm, w_gate=w_gate, w_up=w_up,
             w_down=w_down)
    depth = w_in.shape[0]
    B, T = x_prompt.shape[:2]
    Bs, Ts = x_sample.shape[:2]
    xp, xs = x_prompt, x_sample
    states_p, prev_mem, prev_p = [], [], []
    states_s, kas, vas, lats, krs = [], [], [], [], []
    caches = _time_minor_caches(cache_sb_k, cache_sb_v, cache_mla_latent, cache_mla_krope)
    mem_k_cache = cache_mem_k.reshape(depth, -1, MEM_LEN, MEM_W)
    mem_v_cache = cache_mem_v.reshape(depth, -1, MEM_LEN, MEM_W)
    for l in range(depth):
        wl = _prep_layer(p, l)
        last = l == depth - 1
        mk, mv, mk4, mv4 = _memkv(mem_prompt.reshape(B * MEM_LEN, D_MODEL), wl, nb=2,
                                  prev=tuple(prev_mem) if last else ())
        prev_mem.append((mk4[0], mv4[0]))
        xp, st, new_p = _trunk_layer(xp, None, l, None, mk.reshape(1, B, MEM_LEN, MEM_W),
                                     mv.reshape(1, B, MEM_LEN, MEM_W), wl, prev=tuple(prev_p) if last else ())
        prev_p.append(new_p)
        states_p.append(st)
        xs, st, (ka, va, lat, kr) = _trunk_layer(xs, caches, l, state_gla[l], mem_k_cache, mem_v_cache, wl)
        states_s.append(st)
        kas.append(ka.reshape(Bs, Ts, SB_HEADS, SB_HD))
        vas.append(va.reshape(Bs, Ts, SB_HEADS, SB_HD))
        lats.append(lat.reshape(Bs, Ts, MLA_KV_LORA))
        krs.append(kr.reshape(Bs, Ts, MLA_ROPE))
    ka, va, lat, kr = prev_p[-1]
    if depth == 1:
        ka, va, lat, kr = ka[None], va[None], lat[None], kr[None]
    stack = jnp.stack
    return (xp, xs, jnp.transpose(ka, (0, 1, 4, 2, 3)), jnp.transpose(va, (0, 1, 4, 2, 3)), stack(states_p),
            lat.reshape(depth, B, T, MLA_KV_LORA), jnp.transpose(kr, (0, 1, 3, 2)), mk4, mv4,
            stack(kas), stack(vas), stack(states_s), stack(lats), stack(krs))
```

```python
import functools
import math

import jax
import jax.numpy as jnp
import numpy as np
from jax import lax
from jax.experimental import pallas as pl
from jax.experimental.pallas import tpu as pltpu

F32 = jnp.float32
BF16 = jnp.bfloat16

D_MODEL = 1024
CHUNK = 64
SB_HEADS, SB_HD = 4, 64
SB_W = SB_HEADS * SB_HD
GLA_HEADS, GLA_DK, GLA_DV = 4, 32, 64
GLA_QK_W = GLA_HEADS * GLA_DK
GLA_W = GLA_HEADS * GLA_DV
GLA_GATE_RANK = 16
GLA_TAU = 16.0
GLA_BLOCK = 16
MLA_HEADS = 8
MLA_Q_LORA, MLA_KV_LORA = 256, 128
MLA_NOPE, MLA_ROPE, MLA_V = 64, 32, 64
MLA_W = MLA_HEADS * MLA_V
ROPE_THETA = 10000.0
MEM_LEN, MEM_HEADS, MEM_HD = 256, 4, 128
MEM_W = MEM_HEADS * MEM_HD
D_FF = 2816
NORM_EPS = 1e-6

LANES = 128
VMEM_LIMIT = 56 * 1024 * 1024
SB_SCALE = 1.0 / math.sqrt(SB_HD)
GLA_SCALE = GLA_DK ** -0.5
MLA_SCALE = (MLA_NOPE + MLA_ROPE) ** -0.5
MEM_SCALE = MEM_HD ** -0.5
LOG2E = math.log2(math.e)
SB_Q_SCALE = SB_SCALE * LOG2E
MLA_Q_SCALE2 = MLA_SCALE * LOG2E
SB_DEAD_LOG2 = -105.0 * LOG2E
MASK_NEG = -1e30


def _dot(a, b):
    return jnp.dot(a, b, preferred_element_type=F32)


def _dot_nt(a, b):
    return lax.dot_general(a, b, (((1,), (1,)), ((), ())), preferred_element_type=F32)


def _dot_tn(a, b):
    return lax.dot_general(a, b, (((0,), (0,)), ((), ())), preferred_element_type=F32)


def _split(x):
    hi = x.astype(BF16)
    lo = (x - hi.astype(F32)).astype(BF16)
    return hi, lo


def _dot_split_lhs(x, m):
    hi, lo = _split(x)
    return _dot(hi, m) + _dot(lo, m)


def _rms(x, g):
    ms = jnp.mean(x * x, axis=-1, keepdims=True)
    return x * lax.rsqrt(ms + NORM_EPS) * g


def _seg_rms(x, g, seg, width):
    ss = _dot_split_lhs(x * x, seg)
    return x * lax.rsqrt(ss * (1.0 / width) + NORM_EPS) * g


def _softplus(z):
    return jnp.maximum(z, 0.0) + jnp.log1p(jnp.exp(-jnp.abs(z)))


def _softplus2(z2):
    return jnp.maximum(z2, 0.0) + jnp.log2(1.0 + jnp.exp2(-jnp.abs(z2)))


def _sigmoid(z):
    return 1.0 / (1.0 + jnp.exp(-z))


def _swap_halves(x, seg):
    n = x.shape[-1]
    half = seg // 2
    lane = lax.broadcasted_iota(jnp.int32, x.shape, x.ndim - 1)
    first = (lane & (seg - 1)) < half
    return jnp.where(first, pltpu.roll(x, n - half, x.ndim - 1), pltpu.roll(x, half, x.ndim - 1))


def _div_pow2(x, d):
    return lax.shift_right_logical(x, int(math.log2(d)))


def _seg_matrix(n, width):
    i = np.arange(n) // width
    return jnp.asarray(i[:, None] == i[None, :], dtype=BF16)


def _full(shape):
    nd = len(shape)
    return pl.BlockSpec(shape, lambda *_: (0,) * nd)


def _params(sem):
    return pltpu.CompilerParams(dimension_semantics=sem, vmem_limit_bytes=VMEM_LIMIT)


N_INPROJ_INPUTS = 18
N_CACHE_OUTPUTS = 4


def _inproj_kernel(*refs, time_minor, n_prev):
    (x_ref, gmix_ref, wsb_ref, wgl_ref, wag_ref, wml_ref, wgate_ref, bgate_ref, gcq_ref, wuq_ref,
     gqn_ref, gqr_ref, gkr_ref, gckv_ref, seg64_ref, seg32_ref, cos_ref, sin_ref) = refs[:N_INPROJ_INPUTS]
    prev_refs = refs[N_INPROJ_INPUTS:N_INPROJ_INPUTS + N_CACHE_OUTPUTS * n_prev]
    qa_ref, ka_ref, va_ref, qg_ref, kg_ref, vg_ref, la_ref, rg_ref, *mla_refs = refs[N_INPROJ_INPUTS + len(prev_refs):]
    if time_minor:
        qcat_ref, lat_ref, kr_ref = mla_refs
    else:
        qn_ref, qr_ref, lat_ref, kr_ref = mla_refs
    if n_prev:
        cache_refs = (ka_ref, va_ref, lat_ref, kr_ref)
        for j in range(n_prev):
            for out_ref, prev_ref in zip(cache_refs, prev_refs[N_CACHE_OUTPUTS * j:N_CACHE_OUTPUTS * (j + 1)]):
                out_ref[j] = prev_ref[...]
        ka_ref, va_ref, lat_ref, kr_ref = (r.at[n_prev] for r in cache_refs)
    tm = x_ref.shape[0]
    h = _rms(x_ref[...], gmix_ref[...]).astype(BF16)

    ml = _dot(h, wml_ref[...])
    ag = _dot(h, wag_ref[...])
    sb = _dot(h, wsb_ref[...])
    gl = _dot(h, wgl_ref[...])
    cq = ml[:, :MLA_Q_LORA]
    q = _dot(_rms(cq, gcq_ref[...]).astype(BF16), wuq_ref[...])
    gate = _dot(ag.astype(BF16), wgate_ref[...]) + bgate_ref[...]

    qa_ref[...] = (sb[:, :SB_W] * SB_Q_SCALE).astype(BF16)
    ka = sb[:, SB_W:2 * SB_W]
    va = sb[:, 2 * SB_W:]
    if time_minor:
        ka_ref[...] = ka.T.reshape(SB_HEADS, SB_HD, tm)
        va_ref[...] = va.T.reshape(SB_HEADS, SB_HD, tm)
    else:
        ka_ref[...] = ka
        va_ref[...] = va

    qg_ref[...] = gl[:, :GLA_QK_W]
    kg_ref[...] = gl[:, GLA_QK_W:2 * GLA_QK_W]
    vg_ref[...] = gl[:, 2 * GLA_QK_W:2 * GLA_QK_W + GLA_W]
    rg_ref[...] = gl[:, 2 * GLA_QK_W + GLA_W:]
    la_ref[...] = -_softplus(-gate) * (1.0 / GLA_TAU)

    ckv = ml[:, MLA_Q_LORA:MLA_Q_LORA + MLA_KV_LORA]
    krp = ml[:, MLA_Q_LORA + MLA_KV_LORA:]
    lat_ref[...] = _rms(ckv, gckv_ref[...])
    cos = cos_ref[...]
    sin = sin_ref[...]
    kr_ms = jnp.sum(krp * krp, axis=-1, keepdims=True) * (1.0 / MLA_ROPE)
    krn = krp * lax.rsqrt(kr_ms + NORM_EPS) * gkr_ref[...]
    kr_rot = krn * cos[:, :LANES] + _swap_halves(krn, MLA_ROPE) * sin[:, :LANES]
    if time_minor:
        kr_ref[...] = kr_rot.T[:MLA_ROPE, :]
    else:
        kr_ref[...] = kr_rot[:, :MLA_ROPE]

    nope_w = MLA_HEADS * MLA_NOPE
    q_scale = MLA_Q_SCALE2 if time_minor else MLA_SCALE
    qn = _seg_rms(q[:, :nope_w], gqn_ref[...], seg64_ref[...], MLA_NOPE) * q_scale
    qr = _seg_rms(q[:, nope_w:], gqr_ref[...], seg32_ref[...], MLA_ROPE)
    qr = (qr * cos + _swap_halves(qr, MLA_ROPE) * sin) * q_scale
    if not time_minor:
        qn_ref[...] = qn.astype(BF16)
        qr_ref[...] = qr.astype(BF16)
        return
    lane = lax.broadcasted_iota(jnp.int32, (1, LANES), 1)
    rope_heads_per_tile = LANES // MLA_ROPE
    for hd in range(MLA_HEADS):
        pair = qn[:, LANES * (hd // 2):LANES * (hd // 2 + 1)]
        nope = pair if hd % 2 == 0 else pltpu.roll(pair, MLA_NOPE, 1)
        tile = qr[:, LANES * (hd // rope_heads_per_tile):LANES * (hd // rope_heads_per_tile + 1)]
        shift = (MLA_NOPE - MLA_ROPE * (hd % rope_heads_per_tile)) % LANES
        rope = tile if shift == 0 else pltpu.roll(tile, shift, 1)
        qcat_ref[hd] = jnp.where(lane < MLA_NOPE, nope,
                                 jnp.where(lane < MLA_NOPE + MLA_ROPE, rope, 0.0)).astype(BF16)


def _rope_tables(pos):
    half = MLA_ROPE // 2
    freqs = ROPE_THETA ** (-jnp.arange(half, dtype=F32) / half)
    ang = pos.astype(F32)[:, None] * freqs[None, :]
    cos = jnp.cos(ang)
    sin = jnp.sin(ang)
    cos_t = jnp.tile(jnp.concatenate([cos, cos], axis=-1), (1, MLA_HEADS))
    sin_t = jnp.tile(jnp.concatenate([-sin, sin], axis=-1), (1, MLA_HEADS))
    return cos_t, sin_t


def _inproj(x2d, B, T, past, wl, tm, time_minor, prev=()):
    n = x2d.shape[0]
    tm = min(tm, n)
    cos_t, sin_t = _rope_tables(past + jnp.arange(T, dtype=jnp.int32))
    if T < tm:
        cos_t = jnp.tile(cos_t, (tm // T, 1))
        sin_t = jnp.tile(sin_t, (tm // T, 1))
    n_tab = cos_t.shape[0] // tm
    row = lambda w: pl.BlockSpec((tm, w), lambda i: (i, 0))
    tab = pl.BlockSpec((tm, 2 * LANES), lambda i: (i % n_tab, 0))
    weights = (wl['g_mix'], wl['w_sb'], wl['w_gl'], wl['w_ag'], wl['w_ml'], wl['w_gate'], wl['b_gate'],
               wl['g_cq'], wl['w_uq'], wl['g_qn'], wl['g_qr'], wl['g_kr'], wl['g_ckv'],
               wl['seg64_512'], wl['seg32_256'])
    rowout = lambda w, dt: (row(w), jax.ShapeDtypeStruct((n, w), dt))
    n_prev = len(prev)
    prev_specs = []
    if time_minor:
        assert T % tm == 0
        tpb = T // tm
        kv_spec = pl.BlockSpec((None, SB_HEADS, SB_HD, tm), lambda i: (i // tpb, 0, 0, i % tpb))
        kr_spec = pl.BlockSpec((None, MLA_ROPE, tm), lambda i: (i // tpb, 0, i % tpb))
        kv_out = (kv_spec, jax.ShapeDtypeStruct((B, SB_HEADS, SB_HD, T), F32))
        kr_out = (kr_spec, jax.ShapeDtypeStruct((B, MLA_ROPE, T), F32))
        lat_out = rowout(MLA_KV_LORA, F32)
        if n_prev:
            d = n_prev + 1
            prev_specs = [kv_spec, kv_spec, row(MLA_KV_LORA), kr_spec] * n_prev
            kv_out = (pl.BlockSpec((d, None, SB_HEADS, SB_HD, tm), lambda i: (0, i // tpb, 0, 0, i % tpb)),
                      jax.ShapeDtypeStruct((d, B, SB_HEADS, SB_HD, T), F32))
            kr_out = (pl.BlockSpec((d, None, MLA_ROPE, tm), lambda i: (0, i // tpb, 0, i % tpb)),
                      jax.ShapeDtypeStruct((d, B, MLA_ROPE, T), F32))
            lat_out = (pl.BlockSpec((d, tm, MLA_KV_LORA), lambda i: (0, i, 0)),
                       jax.ShapeDtypeStruct((d, n, MLA_KV_LORA), F32))
        q_out = [(pl.BlockSpec((None, MLA_HEADS, tm, LANES), lambda i: (i // tpb, 0, i % tpb, 0)),
                  jax.ShapeDtypeStruct((B, MLA_HEADS, T, LANES), BF16))]
    else:
        assert not n_prev
        kv_out = rowout(SB_W, F32)
        kr_out = rowout(MLA_ROPE, F32)
        lat_out = rowout(MLA_KV_LORA, F32)
        q_out = [rowout(MLA_HEADS * MLA_NOPE, BF16), rowout(MLA_HEADS * MLA_ROPE, BF16)]
    outs = [rowout(SB_W, BF16), kv_out, kv_out,
            rowout(GLA_QK_W, F32), rowout(GLA_QK_W, F32), rowout(GLA_W, F32), rowout(GLA_QK_W, F32),
            rowout(GLA_W, F32), *q_out, lat_out, kr_out]
    return pl.pallas_call(
        functools.partial(_inproj_kernel, time_minor=time_minor, n_prev=n_prev),
        grid=(n // tm,),
        in_specs=[row(D_MODEL)] + [_full(w.shape) for w in weights] + [tab, tab] + prev_specs,
        out_specs=[o[0] for o in outs],
        out_shape=[o[1] for o in outs],
        compiler_params=_params(("parallel",)),
        name="inproj",
    )(x2d, *weights, cos_t, sin_t, *[a for layer_arrays in prev for a in layer_arrays])


def _sb_kernel(*refs, tq, tk, has_past, n_far_blocks, layer):
    if has_past:
        (q_ref, kn_ref, vn_ref, kt_ref, vt_ref, kt_all_ref, vt_all_ref, g_ref, seg_ref, o_ref,
         kbuf_ref, vbuf_ref) = refs
    else:
        q_ref, kt_ref, vt_ref, g_ref, seg_ref, o_ref = refs
    qi = pl.program_id(1)
    q = q_ref[...]
    lane = lax.broadcasted_iota(jnp.int32, (1, SB_W), 1)
    hmask = [(lane >= SB_HD * h) & (lane < SB_HD * (h + 1)) for h in range(SB_HEADS)]
    qh = [jnp.where(hmask[h], q, jnp.zeros_like(q)) for h in range(SB_HEADS)]

    def suffix_matrix(n):
        return (lax.broadcasted_iota(jnp.int32, (n, n), 0)
                > lax.broadcasted_iota(jnp.int32, (n, n), 1)).astype(BF16)

    def time_minor_block(ref, start, n):
        return ref[:, :, pl.ds(start, n)].reshape(SB_W, n).astype(BF16)

    def block(kblk, vblk, time_minor, vis, umat, cs, acc):
        heads = range(SB_HEADS)
        zs = [_dot(qh[h], kblk) if time_minor else _dot_nt(qh[h], kblk) for h in heads]
        lks = [-_softplus2(z) for z in zs]
        if vis is not None:
            lks = [jnp.where(vis, lk, 0.0) for lk in lks]
        splits = [_split(lk) for lk in lks]
        laters = [_dot(hi, umat) + _dot(lo, umat) for hi, lo in splits]
        ws = [jnp.exp2(zs[h] + lks[h] + laters[h] + cs[h]) for h in heads]
        if vis is not None:
            ws = [jnp.where(vis, w, 0.0) for w in ws]
        ws = [w.astype(BF16) for w in ws]
        pvs = [_dot_nt(w, vblk) if time_minor else _dot(w, vblk) for w in ws]
        for h in heads:
            acc = acc + jnp.where(hmask[h], pvs[h], 0.0)
        new_cs = [cs[h] + laters[h][:, :1] + lks[h][:, :1] for h in heads]
        return new_cs, acc

    def dead(cs):
        m = cs[0]
        for c in cs[1:]:
            m = jnp.maximum(m, c)
        return jnp.max(m)

    vis = (lax.broadcasted_iota(jnp.int32, (tq, tq), 1)
           < lax.broadcasted_iota(jnp.int32, (tq, tq), 0))
    zero_c = jnp.zeros((tq, 1), F32)
    init = ([zero_c] * SB_HEADS, jnp.zeros((tq, SB_W), F32))
    if has_past:
        cs, acc = block(kn_ref[...].astype(BF16), vn_ref[...].astype(BF16), False, vis,
                        suffix_matrix(tq), *init)
        kb0 = jnp.int32(kt_ref.shape[-1] // tk - 1)
    else:
        q0 = pl.multiple_of(qi * tq, tq)
        cs, acc = block(time_minor_block(kt_ref, q0, tq), time_minor_block(vt_ref, q0, tq), True, vis,
                        suffix_matrix(tq), *init)
        kb0 = qi - 1

    umat = suffix_matrix(tk)

    def walk(fetch, kb0, cs, acc):
        def cond(st):
            return jnp.logical_and(st[0] >= 0, st[1] > SB_DEAD_LOG2)

        def body(st):
            kb = st[0]
            cs, acc = list(st[2:2 + SB_HEADS]), st[2 + SB_HEADS]
            kblk, vblk = fetch(pl.multiple_of(kb * tk, tk))
            cs, acc = block(kblk, vblk, True, None, umat, cs, acc)
            return (kb - 1, dead(cs), *cs, acc)

        st = lax.while_loop(cond, body, (kb0, dead(cs), *cs, acc))
        return list(st[2:2 + SB_HEADS]), st[2 + SB_HEADS]

    cs, acc = walk(lambda start: (time_minor_block(kt_ref, start, tk), time_minor_block(vt_ref, start, tk)),
                   kb0, cs, acc)
    if has_past and n_far_blocks:
        b = pl.program_id(0)

        def fetch_far(start):
            pltpu.sync_copy(kt_all_ref.at[layer, b, :, :, pl.ds(start, tk)], kbuf_ref)
            pltpu.sync_copy(vt_all_ref.at[layer, b, :, :, pl.ds(start, tk)], vbuf_ref)
            return (kbuf_ref[...].reshape(SB_W, tk).astype(BF16), vbuf_ref[...].reshape(SB_W, tk).astype(BF16))

        cs, acc = walk(fetch_far, jnp.int32(n_far_blocks - 1), cs, acc)
    o_ref[...] = _seg_rms(acc, g_ref[...], seg_ref[...], SB_HD).astype(BF16)


def _sb_attn(q, k_new, v_new, kt_cache, vt_cache, layer, wl, tq, tk, window):
    B, T, _ = q.shape
    tq = min(tq, T)
    has_past = kt_cache is not None
    n_far_blocks = 0
    scratch = []
    in_specs = [pl.BlockSpec((None, tq, SB_W), lambda b, i: (b, i, 0))]
    if has_past:
        P = kt_cache.shape[-1]
        window = min(window, P)
        assert tq == T and P % window == 0 and window % tk == 0
        n_far_blocks = (P - window) // tk
        new = pl.BlockSpec((None, T, SB_W), lambda b, i: (b, 0, 0))
        near = pl.BlockSpec((None, None, SB_HEADS, SB_HD, window), lambda b, i: (layer, b, 0, 0, P // window - 1))
        hbm = pl.BlockSpec(memory_space=pl.ANY)
        in_specs += [new, new, near, near, hbm, hbm]
        args = [q, k_new, v_new, kt_cache, vt_cache, kt_cache, vt_cache]
        scratch = [pltpu.VMEM((SB_HEADS, SB_HD, tk), F32)] * 2
    else:
        assert tq == tk
        if k_new.ndim == 4:
            new = pl.BlockSpec((None, SB_HEADS, SB_HD, T), lambda b, i: (b, 0, 0, 0))
        else:
            new = pl.BlockSpec((None, None, SB_HEADS, SB_HD, T), lambda b, i: (layer, b, 0, 0, 0))
        in_specs += [new, new]
        args = [q, k_new, v_new]
    in_specs += [_full((1, SB_W)), _full((SB_W, SB_W))]
    args += [wl['g_sb_out'], wl['seg64_256']]
    return pl.pallas_call(
        functools.partial(_sb_kernel, tq=tq, tk=tk, has_past=has_past, n_far_blocks=n_far_blocks, layer=layer),
        grid=(B, T // tq),
        in_specs=in_specs,
        out_specs=pl.BlockSpec((None, tq, SB_W), lambda b, i: (b, i, 0)),
        out_shape=jax.ShapeDtypeStruct((B, T, SB_W), BF16),
        scratch_shapes=scratch,
        compiler_params=_params(("parallel", "parallel")),
        name="sb_attn",
    )(*args)


def _gla_kernel(q_ref, k_ref, v_ref, la_ref, rg_ref, s0_ref, g_ref, seg_ref, o_ref, s_ref, *, tc, nb):
    ci = pl.program_id(1)

    @pl.when(ci == 0)
    def _():
        s_ref[...] = s0_ref[...]

    nsub = tc // GLA_BLOCK
    r_i = lax.broadcasted_iota(jnp.int32, (tc, tc), 0)
    c_i = lax.broadcasted_iota(jnp.int32, (tc, tc), 1)
    same = _div_pow2(r_i, GLA_BLOCK) == _div_pow2(c_i, GLA_BLOCK)
    causal = jnp.logical_and(same, c_i <= r_i)
    causal_b = causal.astype(BF16)
    lane_k = lax.broadcasted_iota(jnp.int32, (1, GLA_QK_W), 1)
    lane_v = lax.broadcasted_iota(jnp.int32, (1, GLA_W), 1)
    kms = [(lane_k >= GLA_DK * h) & (lane_k < GLA_DK * (h + 1)) for h in range(GLA_HEADS)]
    vms = [(lane_v >= GLA_DV * h) & (lane_v < GLA_DV * (h + 1)) for h in range(GLA_HEADS)]
    sr = _div_pow2(lax.broadcasted_iota(jnp.int32, (GLA_QK_W, GLA_W), 0), GLA_DK)
    sc = _div_pow2(lax.broadcasted_iota(jnp.int32, (GLA_QK_W, GLA_W), 1), GLA_DV)
    bd = sr == sc
    eye = (lax.broadcasted_iota(jnp.int32, (GLA_QK_W, GLA_QK_W), 0)
           == lax.broadcasted_iota(jnp.int32, (GLA_QK_W, GLA_QK_W), 1))
    g = g_ref[...]
    seg = seg_ref[...]

    rows_b = range(nb)
    heads = range(GLA_HEADS)
    las = [_split(la_ref[bi]) for bi in rows_b]
    bs = [_dot(causal_b, hi) + _dot(causal_b, lo) for hi, lo in las]
    bls = [jnp.broadcast_to(b.reshape(nsub, GLA_BLOCK, GLA_QK_W)[:, GLA_BLOCK - 1:, :],
                            (nsub, GLA_BLOCK, GLA_QK_W)).reshape(tc, GLA_QK_W) for b in bs]
    qts = [(q_ref[bi] * jnp.exp(bs[bi]) * GLA_SCALE).astype(BF16) for bi in rows_b]
    kts = [(k_ref[bi] * jnp.exp(-bs[bi])).astype(BF16) for bi in rows_b]
    kds = [(k_ref[bi] * jnp.exp(bls[bi] - bs[bi])).astype(BF16) for bi in rows_b]
    vbs = [v_ref[bi].astype(BF16) for bi in rows_b]
    decays = [jnp.exp(bl) for bl in bls]

    atts = [[_dot_nt(jnp.where(kms[h], qts[bi], jnp.zeros_like(qts[bi])), kts[bi]) for h in heads]
            for bi in rows_b]
    atts = [[jnp.where(causal, a, 0.0).astype(BF16) for a in row] for row in atts]
    pvs = [[_dot(atts[bi][h], vbs[bi]) for h in heads] for bi in rows_b]
    intras = []
    for bi in rows_b:
        intra = jnp.zeros((tc, GLA_W), F32)
        for h in heads:
            intra = intra + jnp.where(vms[h], pvs[bi][h], 0.0)
        intras.append(intra)

    states = [s_ref[bi] for bi in rows_b]
    inters = [[] for _ in rows_b]
    for j in range(nsub):
        rows = slice(j * GLA_BLOCK, (j + 1) * GLA_BLOCK)
        for bi in rows_b:
            inters[bi].append(_dot(qts[bi][rows], states[bi].astype(BF16)))
        upds = [_dot_tn(kds[bi][rows], vbs[bi][rows]) for bi in rows_b]
        for bi in rows_b:
            d_row = decays[bi][j * GLA_BLOCK:j * GLA_BLOCK + 1]
            d_col = jnp.sum(jnp.where(eye, d_row, 0.0), axis=1, keepdims=True)
            states[bi] = d_col * states[bi] + jnp.where(bd, upds[bi], 0.0)
    os_ = [intras[bi] + jnp.concatenate(inters[bi], axis=0) for bi in rows_b]
    splits = [_split(o * o) for o in os_]
    sss = [_dot(hi, seg) + _dot(lo, seg) for hi, lo in splits]
    for bi in rows_b:
        s_ref[bi] = states[bi]
        o = os_[bi] * lax.rsqrt(sss[bi] * (1.0 / GLA_DV) + NORM_EPS) * g
        rg = rg_ref[bi]
        o_ref[bi] = (o * (rg * _sigmoid(rg))).astype(BF16)


def _gla(qg, kg, vg, la, rg, s0_bd, wl, tc, nb):
    B, T, _ = qg.shape
    tc = min(tc, T)
    nb = min(nb, B)
    blk = lambda w: pl.BlockSpec((nb, tc, w), lambda b, c: (b, c, 0))
    st = pl.BlockSpec((nb, GLA_QK_W, GLA_W), lambda b, c: (b, 0, 0))
    return pl.pallas_call(
        functools.partial(_gla_kernel, tc=tc, nb=nb),
        grid=(B // nb, T // tc),
        in_specs=[blk(GLA_QK_W), blk(GLA_QK_W), blk(GLA_W), blk(GLA_QK_W), blk(GLA_W), st,
                  _full((1, GLA_W)), _full((GLA_W, GLA_W))],
        out_specs=[blk(GLA_W), st],
        out_shape=[jax.ShapeDtypeStruct((B, T, GLA_W), BF16),
                   jax.ShapeDtypeStruct((B, GLA_QK_W, GLA_W), F32)],
        compiler_params=_params(("parallel", "arbitrary")),
        name="gla",
    )(qg, kg, vg, la, rg, s0_bd, wl['g_gla_out'], wl['seg64_256'])


def _state_to_blockdiag(s):
    B = s.shape[0]
    eye = jnp.eye(GLA_HEADS, dtype=s.dtype)
    return jnp.einsum('bhkv,hg->bhkgv', s, eye).reshape(B, GLA_QK_W, GLA_W)


def _state_from_blockdiag(s):
    B = s.shape[0]
    s5 = s.reshape(B, GLA_HEADS, GLA_DK, GLA_HEADS, GLA_DV)
    return jnp.stack([s5[:, h, :, h, :] for h in range(GLA_HEADS)], axis=1)


def _mla_kv_kernel(lat_ref, kr_ref, wn_ref, wvt_ref, gkn_ref, seg_ref, place_ref, kcat_ref, vt_ref):
    tr = lat_ref.shape[0]
    lat = lat_ref[...].astype(BF16)
    kn = _seg_rms(_dot(lat, wn_ref[...]), gkn_ref[...], seg_ref[...], MLA_NOPE)
    vt_ref[...] = _dot_nt(wvt_ref[...], lat).reshape(MLA_HEADS, MLA_V, tr).astype(BF16)
    krp = _dot_tn(kr_ref[...].astype(BF16), place_ref[...])
    lane = lax.broadcasted_iota(jnp.int32, (1, LANES), 1)
    for p in range(MLA_HEADS // 2):
        pair = kn[:, LANES * p:LANES * (p + 1)]
        kcat_ref[2 * p] = jnp.where(lane < MLA_NOPE, pair, krp).astype(BF16)
        kcat_ref[2 * p + 1] = jnp.where(lane < MLA_NOPE, pltpu.roll(pair, MLA_NOPE, 1), krp).astype(BF16)


def _mla_kv(lat, kr_t, layer, wl, tr):
    B, L, _ = lat.shape[-3:]
    tr = min(tr, L)
    if lat.ndim == 3:
        lat_spec = pl.BlockSpec((None, tr, MLA_KV_LORA), lambda b, i: (b, i, 0))
        kr_spec = pl.BlockSpec((None, MLA_ROPE, tr), lambda b, i: (b, 0, i))
    else:
        lat_spec = pl.BlockSpec((None, None, tr, MLA_KV_LORA), lambda b, i: (layer, b, i, 0))
        kr_spec = pl.BlockSpec((None, None, MLA_ROPE, tr), lambda b, i: (layer, b, 0, i))
    return pl.pallas_call(
        _mla_kv_kernel,
        grid=(B, L // tr),
        in_specs=[lat_spec, kr_spec,
                  _full(wl['w_ukn'].shape), _full(wl['w_ukv_v_t'].shape), _full((1, MLA_HEADS * MLA_NOPE)),
                  _full(wl['seg64_512'].shape), _full((MLA_ROPE, LANES))],
        out_specs=[pl.BlockSpec((None, MLA_HEADS, tr, LANES), lambda b, i: (b, 0, i, 0)),
                   pl.BlockSpec((None, MLA_HEADS, MLA_V, tr), lambda b, i: (b, 0, 0, i))],
        out_shape=[jax.ShapeDtypeStruct((B, MLA_HEADS, L, LANES), BF16),
                   jax.ShapeDtypeStruct((B, MLA_HEADS, MLA_V, L), BF16)],
        compiler_params=_params(("parallel", "parallel")),
        name="mla_kv",
    )(lat, kr_t, wl['w_ukn'], wl['w_ukv_v_t'], wl['g_kn'], wl['seg64_512'], wl['rope_place'])


def _mla_attn_kernel(q_ref, k_ref, vt_ref, g_ref, o_ref, *, tq):
    qi = pl.program_id(1)
    q0 = pl.multiple_of(qi * tq, tq)
    vis = (_div_pow2(lax.broadcasted_iota(jnp.int32, (tq, tq), 0), CHUNK)
           <= _div_pow2(lax.broadcasted_iota(jnp.int32, (tq, tq), 1), CHUNK))
    qs = [q_ref[h] for h in range(MLA_HEADS)]

    heads = range(MLA_HEADS)

    def step(start, width, st, mask):
        ss = [_dot_nt(k_ref[h, pl.ds(start, width), :], qs[h]) for h in heads]
        if mask:
            ss = [jnp.where(vis, s, MASK_NEG) for s in ss]
        out = []
        for h in heads:
            bm = jnp.max(ss[h], axis=0, keepdims=True)
            m_new = bm if st is None else jnp.maximum(st[3 * h], bm)
            p = jnp.exp2(ss[h] - m_new)
            l_blk = jnp.sum(p, axis=0, keepdims=True)
            pv = _dot(vt_ref[h, :, pl.ds(start, width)], p.astype(BF16))
            if st is None:
                out += [m_new, l_blk, pv]
            else:
                a = jnp.exp2(st[3 * h] - m_new)
                out += [m_new, a * st[3 * h + 1] + l_blk, a * st[3 * h + 2] + pv]
        return tuple(out)

    st = step(q0, tq, None, True)
    st = lax.fori_loop(0, lax.shift_right_logical(qi, 1),
                       lambda kb, st: step(pl.multiple_of(kb * (2 * tq), 2 * tq), 2 * tq, st, False), st)
    st = lax.cond((qi & 1) == 1, lambda st: step(pl.multiple_of((qi - 1) * tq, tq), tq, st, False),
                  lambda st: st, st)
    g = g_ref[...]
    outs = []
    for h in heads:
        o = st[3 * h + 2] / st[3 * h + 1]
        ms = jnp.mean(o * o, axis=0, keepdims=True)
        outs.append(o * lax.rsqrt(ms + NORM_EPS) * g)
    o_ref[...] = jnp.concatenate(outs, axis=0).T.astype(BF16)


def _mla_attn(qcat, kcat, v_t, wl, tq):
    B, _, T, _ = qcat.shape
    tq = min(tq, T)
    return pl.pallas_call(
        functools.partial(_mla_attn_kernel, tq=tq),
        grid=(B, T // tq),
        in_specs=[pl.BlockSpec((None, MLA_HEADS, tq, LANES), lambda b, i: (b, 0, i, 0)),
                  pl.BlockSpec((None, MLA_HEADS, T, LANES), lambda b, i: (b, 0, 0, 0)),
                  pl.BlockSpec((None, MLA_HEADS, MLA_V, T), lambda b, i: (b, 0, 0, 0)),
                  _full((MLA_V, 1))],
        out_specs=pl.BlockSpec((None, tq, MLA_W), lambda b, i: (b, i, 0)),
        out_shape=jax.ShapeDtypeStruct((B, T, MLA_W), BF16),
        compiler_params=_params(("parallel", "parallel")),
        name="mla_attn",
    )(qcat, kcat, v_t, wl['g_mla_out_col'])


def _mla_dec_kernel(qn_ref, qr_ref, lat_ref, krt_ref, latn_ref, krn_ref, wknt_ref, wv_ref, gkn_ref,
                    g_ref, seg_ref, o_ref, s_ref, *, kc):
    nb, P = lat_ref.shape[:2]
    T = qn_ref.shape[0] // nb
    H = MLA_HEADS
    wknt = wknt_ref[...]
    gkn = gkn_ref[...]

    def queries(b):
        rows = slice(b * T, (b + 1) * T)
        qn = (qn_ref[rows, :].astype(F32) * gkn).astype(BF16)
        qabs = jnp.concatenate(
            [_dot(qn[:, MLA_NOPE * h:MLA_NOPE * (h + 1)], wknt[MLA_NOPE * h:MLA_NOPE * (h + 1), :])
             for h in range(H)], axis=0).astype(BF16)
        qr = qr_ref[rows, :]
        qrs = jnp.concatenate([qr[:, MLA_ROPE * h:MLA_ROPE * (h + 1)] for h in range(H)], axis=0)
        return qabs, qrs

    def scores(qabs, lat_b, sr, n):
        kn_t = _dot_nt(wknt, lat_b)
        ss = jnp.sum((kn_t * kn_t).reshape(H, MLA_NOPE, n), axis=1)
        r = lax.rsqrt(ss * (1.0 / MLA_NOPE) + NORM_EPS)
        sn = _dot_nt(qabs, lat_b).reshape(H, T, n)
        return (sn * r[:, None, :]).reshape(H * T, n) + sr

    def all_scores(b, qabs, qrs):
        m = None
        for c in range(P // kc):
            cols = slice(c * kc, (c + 1) * kc)
            s = scores(qabs, lat_ref[b, cols, :].astype(BF16), _dot(qrs, krt_ref[b, :, cols].astype(BF16)), kc)
            s_ref[b, :, cols] = s
            mc = jnp.max(s, axis=-1, keepdims=True)
            m = mc if m is None else jnp.maximum(m, mc)
        rows = slice(b * T, (b + 1) * T)
        s_new = scores(qabs, latn_ref[rows, :].astype(BF16), _dot_nt(qrs, krn_ref[rows, :].astype(BF16)), T)
        return jnp.maximum(m, jnp.max(s_new, axis=-1, keepdims=True)), s_new

    def context(b, m, s_new):
        p = jnp.exp(s_new - m)
        l = jnp.sum(p, axis=-1, keepdims=True)
        ctx = _dot(p.astype(BF16), latn_ref[b * T:(b + 1) * T, :].astype(BF16))
        for c in range(P // kc):
            cols = slice(c * kc, (c + 1) * kc)
            p = jnp.exp(s_ref[b, :, cols] - m)
            l = l + jnp.sum(p, axis=-1, keepdims=True)
            ctx = ctx + _dot(p.astype(BF16), lat_ref[b, cols, :].astype(BF16))
        return (ctx / l).astype(BF16)

    qs = [queries(b) for b in range(nb)]
    stats = [all_scores(b, *qs[b]) for b in range(nb)]
    ctxs = [context(b, *stats[b]) for b in range(nb)]

    wv = wv_ref[...]
    lane = lax.broadcasted_iota(jnp.int32, (1, MLA_W), 1)
    for b in range(nb):
        o = jnp.zeros((T, MLA_W), F32)
        for h in range(H):
            oh = _dot(ctxs[b][T * h:T * (h + 1)], wv)
            o = jnp.where((lane >= MLA_V * h) & (lane < MLA_V * (h + 1)), oh, o)
        o_ref[b * T:(b + 1) * T, :] = _seg_rms(o, g_ref[...], seg_ref[...], MLA_V).astype(BF16)


def _mla_dec(qn, qr, lat_cache, kr_t_cache, layer, lat_new, kr_new, T, wl, kc, nb):
    n = qn.shape[0]
    B = n // T
    P = lat_cache.shape[2]
    row = lambda w: pl.BlockSpec((nb * T, w), lambda b: (b, 0))
    return pl.pallas_call(
        functools.partial(_mla_dec_kernel, kc=kc),
        grid=(B // nb,),
        in_specs=[row(MLA_HEADS * MLA_NOPE), row(MLA_HEADS * MLA_ROPE),
                  pl.BlockSpec((None, nb, P, MLA_KV_LORA), lambda b: (layer, b, 0, 0)),
                  pl.BlockSpec((None, nb, MLA_ROPE, P), lambda b: (layer, b, 0, 0)),
                  row(MLA_KV_LORA), row(MLA_ROPE),
                  _full(wl['w_ukn_t'].shape), _full(wl['w_ukv_v'].shape), _full((1, MLA_HEADS * MLA_NOPE)),
                  _full((1, MLA_W)), _full((MLA_W, MLA_W))],
        out_specs=row(MLA_W),
        out_shape=jax.ShapeDtypeStruct((n, MLA_W), BF16),
        scratch_shapes=[pltpu.VMEM((nb, MLA_HEADS * T, P), F32)],
        compiler_params=_params(("parallel",)),
        name="mla_dec",
    )(qn, qr, lat_cache, kr_t_cache, lat_new, kr_new, wl['w_ukn_t'], wl['w_ukv_v'], wl['g_kn'],
      wl['g_mla_out_512'], wl['seg64_512'])


def _post_kernel(x_ref, oa_ref, ob_ref, oc_ref, mk_ref, mv_ref, wo_ref, gcross_ref, wcq_ref, gcqn_ref,
                 wco_ref, o_ref, att_ref, *, nb, tt):
    wo = wo_ref
    mix = (_dot(oa_ref[...], wo[:SB_W, :]) + _dot(ob_ref[...], wo[SB_W:SB_W + GLA_W, :])
           + _dot(oc_ref[...], wo[SB_W + GLA_W:, :]))
    x1 = x_ref[...] + mix
    h = _rms(x1, gcross_ref[...]).astype(BF16)
    q = _dot(h, wcq_ref[...])
    gq = gcqn_ref[...]
    qs = []
    for hd in range(MEM_HEADS):
        qh = _rms(q[:, MEM_HD * hd:MEM_HD * (hd + 1)], gq).astype(BF16)
        qs.append(qh)
    pairs = [(bi, hd) for bi in range(nb) for hd in range(MEM_HEADS)]
    rows = lambda bi: slice(bi * tt, (bi + 1) * tt)
    cols = lambda hd: slice(MEM_HD * hd, MEM_HD * (hd + 1))
    ss = [_dot_nt(qs[hd][rows(bi)], mk_ref[bi, :, cols(hd)].astype(BF16)) * MEM_SCALE for bi, hd in pairs]
    ps, ls = [], []
    for s in ss:
        p = jnp.exp(s - jnp.max(s, axis=-1, keepdims=True))
        ls.append(jnp.sum(p, axis=-1, keepdims=True))
        ps.append(p.astype(BF16))
    pvs = [_dot(ps[i], mv_ref[bi, :, cols(hd)].astype(BF16)) for i, (bi, hd) in enumerate(pairs)]
    for i, (bi, hd) in enumerate(pairs):
        att_ref[rows(bi), cols(hd)] = pvs[i] / ls[i]
    o_ref[...] = x1 + _dot(att_ref[...].astype(BF16), wco_ref[...])


def _post(x2d, oa, ob, oc, mem_k, mem_v, layer, T, wl, tm):
    n = x2d.shape[0]
    tm = min(tm, n)
    tt = min(T, tm)
    nb = tm // tt
    tiles_per_b = T // tt
    row = lambda w: pl.BlockSpec((tm, w), lambda i: (i, 0))
    mem = pl.BlockSpec((None, nb, MEM_LEN, MEM_W),
                       lambda i: (layer, i // tiles_per_b if nb == 1 else i, 0, 0))
    weights = (wl['w_out'], wl['g_cross'], wl['w_cq'], wl['g_cqn'], wl['w_co'])
    return pl.pallas_call(
        functools.partial(_post_kernel, nb=nb, tt=tt),
        grid=(n // tm,),
        in_specs=[row(D_MODEL), row(SB_W), row(GLA_W), row(MLA_W), mem, mem] + [_full(w.shape) for w in weights],
        out_specs=row(D_MODEL),
        out_shape=jax.ShapeDtypeStruct((n, D_MODEL), F32),
        scratch_shapes=[pltpu.VMEM((tm, MEM_W), F32)],
        compiler_params=_params(("parallel",)),
        name="post",
    )(x2d, oa, ob, oc, mem_k, mem_v, *weights)


def _ffn_kernel(x_ref, g_ref, wg_ref, wu_ref, wd_ref, o_ref, *, n_chunks):
    x = x_ref[...]
    h = _rms(x, g_ref[...]).astype(BF16)
    tf = wg_ref.shape[1] // n_chunks
    acc = x
    for c in range(n_chunks):
        cols = slice(c * tf, (c + 1) * tf)
        gate = _dot(h, wg_ref[:, cols])
        up = _dot(h, wu_ref[:, cols])
        act = (gate * _sigmoid(gate) * up).astype(BF16)
        acc = acc + _dot(act, wd_ref[cols, :])
    o_ref[...] = acc


def _ffn(x2d, wl, tm, n_chunks):
    n = x2d.shape[0]
    tm = min(tm, n)
    resident = lambda shape: pl.BlockSpec(shape, lambda i: (0, 0), pipeline_mode=pl.Buffered(1))
    return pl.pallas_call(
        functools.partial(_ffn_kernel, n_chunks=n_chunks),
        grid=(n // tm,),
        in_specs=[pl.BlockSpec((tm, D_MODEL), lambda i: (i, 0)), _full((1, D_MODEL)),
                  resident((D_MODEL, D_FF)), resident((D_MODEL, D_FF)), resident((D_FF, D_MODEL))],
        out_specs=pl.BlockSpec((tm, D_MODEL), lambda i: (i, 0)),
        out_shape=jax.ShapeDtypeStruct((n, D_MODEL), F32),
        compiler_params=_params(("parallel",)),
        name="ffn",
    )(x2d, wl['g_ffn'], wl['w_gate_ffn'], wl['w_up'], wl['w_down'])


def _memkv_kernel(*refs, n_prev):
    m_ref, g_ref, wk_ref, wv_ref, gk_ref = refs[:5]
    prev_refs = refs[5:5 + 2 * n_prev]
    k_ref, v_ref, k4_ref, v4_ref = refs[5 + 2 * n_prev:]
    nb = k4_ref.shape[1]
    for j in range(n_prev):
        k4_ref[j] = prev_refs[2 * j][...]
        v4_ref[j] = prev_refs[2 * j + 1][...]
    h = _rms(m_ref[...], g_ref[...]).astype(BF16)
    k = _dot(h, wk_ref[...])
    v = _dot(h, wv_ref[...])
    gk = gk_ref[...]
    v_ref[...] = v
    for hd in range(MEM_HEADS):
        cols = slice(MEM_HD * hd, MEM_HD * (hd + 1))
        kh = _rms(k[:, cols], gk)
        k_ref[:, cols] = kh
        k4_ref[n_prev, :, :, hd, :] = kh.reshape(nb, MEM_LEN, MEM_HD)
        v4_ref[n_prev, :, :, hd, :] = v[:, cols].reshape(nb, MEM_LEN, MEM_HD)


def _memkv(mem2d, wl, nb, prev=()):
    B = mem2d.shape[0] // MEM_LEN
    nb = min(nb, B)
    d = len(prev) + 1
    row = lambda w: pl.BlockSpec((nb * MEM_LEN, w), lambda i: (i, 0))
    weights = (wl['g_mem'], wl['w_ck'], wl['w_cv'], wl['g_ckn'])
    prev_spec = pl.BlockSpec((nb, MEM_LEN, MEM_HEADS, MEM_HD), lambda i: (i, 0, 0, 0))
    out4 = (pl.BlockSpec((d, nb, MEM_LEN, MEM_HEADS, MEM_HD), lambda i: (0, i, 0, 0, 0)),
            jax.ShapeDtypeStruct((d, B, MEM_LEN, MEM_HEADS, MEM_HD), F32))
    out2 = (row(MEM_W), jax.ShapeDtypeStruct((B * MEM_LEN, MEM_W), F32))
    outs = [out2, out2, out4, out4]
    return pl.pallas_call(
        functools.partial(_memkv_kernel, n_prev=len(prev)),
        grid=(B // nb,),
        in_specs=[row(D_MODEL)] + [_full(w.shape) for w in weights] + [prev_spec] * (2 * len(prev)),
        out_specs=[o[0] for o in outs],
        out_shape=[o[1] for o in outs],
        compiler_params=_params(("parallel",)),
        name="memkv",
    )(mem2d, *weights, *[a for pair in prev for a in pair])


def _prep_layer(p, l):
    w_in = p['w_in'][l]
    o = np.cumsum([0, SB_W, SB_W, SB_W, GLA_QK_W, GLA_QK_W, GLA_W, GLA_GATE_RANK, GLA_W,
                   MLA_Q_LORA, MLA_KV_LORA, MLA_ROPE])
    seg = lambda i, j: w_in[:, o[i]:o[j]]
    pad_cols = lambda w, n: jnp.pad(w, ((0, 0), (0, n - w.shape[1])))
    row = lambda g, reps=1: jnp.tile(g, reps)[None, :].astype(F32)
    w_uq = p['w_uq'][l].reshape(MLA_Q_LORA, MLA_HEADS, MLA_NOPE + MLA_ROPE)
    w_ukv = p['w_ukv'][l].reshape(MLA_KV_LORA, MLA_HEADS, MLA_NOPE + MLA_V)
    place = np.zeros((MLA_ROPE, LANES), np.float32)
    place[np.arange(MLA_ROPE), MLA_NOPE + np.arange(MLA_ROPE)] = 1.0
    return {
        'g_mix': row(p['g_mix_norm'][l]),
        'w_sb': seg(0, 3).astype(BF16),
        'w_gl': jnp.concatenate([seg(3, 6), seg(7, 8)], axis=1).astype(BF16),
        'w_ag': pad_cols(seg(6, 7), LANES).astype(BF16),
        'w_ml': pad_cols(seg(8, 11), 4 * LANES).astype(BF16),
        'w_gate': jnp.pad(p['w_gla_gate'][l], ((0, LANES - GLA_GATE_RANK), (0, 0))).astype(BF16),
        'b_gate': row(p['b_gla_gate'][l]),
        'g_cq': row(p['g_cq'][l]),
        'w_uq': jnp.concatenate([w_uq[:, :, :MLA_NOPE].reshape(MLA_Q_LORA, -1),
                                 w_uq[:, :, MLA_NOPE:].reshape(MLA_Q_LORA, -1)], axis=1).astype(BF16),
        'g_qn': row(p['g_qn'][l], MLA_HEADS),
        'g_qr': row(p['g_qr'][l], MLA_HEADS),
        'g_kr': jnp.pad(p['g_kr'][l], (0, LANES - MLA_ROPE))[None, :],
        'g_ckv': row(p['g_ckv'][l]),
        'w_ukn': w_ukv[:, :, :MLA_NOPE].reshape(MLA_KV_LORA, -1).astype(BF16),
        'w_ukv_v': w_ukv[:, :, MLA_NOPE:].reshape(MLA_KV_LORA, -1).astype(BF16),
        'g_kn': row(p['g_kn'][l], MLA_HEADS),
        'w_ukn_t': w_ukv[:, :, :MLA_NOPE].reshape(MLA_KV_LORA, -1).T.astype(BF16),
        'w_ukv_v_t': w_ukv[:, :, MLA_NOPE:].reshape(MLA_KV_LORA, -1).T.astype(BF16),
        'g_mla_out_512': row(p['g_mla_out'][l], MLA_HEADS),
        'g_mla_out_col': p['g_mla_out'][l][:, None].astype(F32),
        'g_sb_out': row(p['g_sb_out'][l], SB_HEADS),
        'g_gla_out': row(p['g_gla_out'][l], GLA_HEADS),
        'w_out': p['w_out'][l].astype(BF16),
        'g_cross': row(p['g_cross_norm'][l]),
        'g_mem': row(p['g_mem_norm'][l]),
        'w_cq': p['w_cq'][l].astype(BF16),
        'w_ck': p['w_ck'][l].astype(BF16),
        'w_cv': p['w_cv'][l].astype(BF16),
        'g_cqn': row(p['g_cqn'][l]),
        'g_ckn': row(p['g_ckn'][l]),
        'w_co': p['w_co'][l].astype(BF16),
        'g_ffn': row(p['g_ffn_norm'][l]),
        'w_gate_ffn': p['w_gate'][l].astype(BF16),
        'w_up': p['w_up'][l].astype(BF16),
        'w_down': p['w_down'][l].astype(BF16),
        'seg64_512': _seg_matrix(512, 64),
        'seg32_256': _seg_matrix(256, 32),
        'seg64_256': _seg_matrix(256, 64),
        'rope_place': jnp.asarray(place, dtype=BF16),
    }


def _time_minor_caches(cache_sb_k, cache_sb_v, cache_mla_latent, cache_mla_krope):
    return (jnp.transpose(cache_sb_k, (0, 1, 3, 4, 2)), jnp.transpose(cache_sb_v, (0, 1, 3, 4, 2)),
            cache_mla_latent, jnp.transpose(cache_mla_krope, (0, 1, 3, 2)))


def _trunk_layer(x, caches, layer, gla_s0, mem_k, mem_v, wl, prev=()):
    B, T, _ = x.shape
    has_past = caches is not None
    past = caches[0].shape[-1] if has_past else 0
    x2d = x.reshape(B * T, D_MODEL)
    qa, ka, va, qg, kg, vg, la, rg, *q_mla, lat, kr = _inproj(x2d, B, T, past, wl, tm=512,
                                                              time_minor=not has_past, prev=prev)
    r3 = lambda a: a.reshape(B, T, a.shape[-1])

    if has_past:
        sb_kt, sb_vt, lat_cache, kr_t_cache = caches
        o_a = _sb_attn(r3(qa), r3(ka), r3(va), sb_kt, sb_vt, layer, wl, tq=256, tk=256, window=512)
    else:
        o_a = _sb_attn(r3(qa), ka, va, None, None, len(prev), wl, tq=256, tk=256, window=None)

    if gla_s0 is None:
        s0 = jnp.zeros((B, GLA_QK_W, GLA_W), F32)
    else:
        s0 = _state_to_blockdiag(gla_s0)
    o_b, s_bd = _gla(r3(qg), r3(kg), r3(vg), r3(la), r3(rg), s0, wl, tc=128, nb=4)
    gla_s = _state_from_blockdiag(s_bd)

    if has_past:
        o_c = _mla_dec(*q_mla, lat_cache, kr_t_cache, layer, lat, kr, T, wl, kc=1024, nb=2)
    else:
        lat_b = lat.reshape(lat.shape[:-2] + (B, T, MLA_KV_LORA))
        kcat, v_t = _mla_kv(lat_b, kr, len(prev), wl, tr=512)
        o_c = _mla_attn(*q_mla, kcat, v_t, wl, tq=256)

    x2d = _post(x2d, o_a.reshape(B * T, SB_W), o_b.reshape(B * T, GLA_W), o_c.reshape(B * T, MLA_W),
                mem_k, mem_v, layer if has_past else 0, T, wl, tm=512)
    x2d = _ffn(x2d, wl, tm=512, n_chunks=2)
    return x2d.reshape(B, T, D_MODEL), gla_s, (ka, va, lat, kr)


def kernel(x_prompt, x_sample, mem_prompt, cache_sb_k, cache_sb_v, state_gla, cache_mla_latent,
           cache_mla_krope, cache_mem_k, cache_mem_v, g_mix_norm, w_in, w_gla_gate, b_gla_gate,
           g_gla_out, g_sb_out, g_cq, w_uq, g_qn, g_qr, g_kr, g_ckv, w_ukv, g_kn, g_mla_out, w_out,
           g_cross_norm, g_mem_norm, w_cq, w_ck, w_cv, g_cqn, g_ckn, w_co, g_ffn_norm, w_gate,
           w_up, w_down):
    p = dict(g_mix_norm=g_mix_norm, w_in=w_in, w_gla_gate=w_gla_gate, b_gla_gate=b_gla_gate,
             g_gla_out=g_gla_out, g_sb_out=g_sb_out, g_cq=g_cq, w_uq=w_uq, g_qn=g_qn, g_qr=g_qr,
             g_kr=g_kr, g_ckv=g_ckv, w_ukv=w_ukv, g_kn=g_kn, g_mla_out=g_mla_out, w_out=w_out,
             g_cross_norm=g_cross_norm, g_mem_norm=g_mem_norm, w_cq=w_cq, w_ck=w_ck, w_cv=w_cv,
             g_cqn=g_cqn, g_ckn=g_ckn, w_co=w_co, g_ffn_norm=g_ffn_norm, w_gate=w_gate, w_up=w_up,
             w_down=w_down)
    depth = w_in.shape[0]
    B, T = x_prompt.shape[:2]
    Bs, Ts = x_sample.shape[:2]
    xp, xs = x_prompt, x_sample
    states_p, prev_mem, prev_p = [], [], []
    states_s, kas, vas, lats, krs = [], [], [], [], []
    caches = _time_minor_caches(cache_sb_k, cache_sb_v, cache_mla_latent, cache_mla_krope)
    mem_k_cache = cache_mem_k.reshape(depth, -1, MEM_LEN, MEM_W)
    mem_v_cache = cache_mem_v.reshape(depth, -1, MEM_LEN, MEM_W)
    for l in range(depth):
        wl = _prep_layer(p, l)
        last = l == depth - 1
        mk, mv, mk4, mv4 = _memkv(mem_prompt.reshape(B * MEM_LEN, D_MODEL), wl, nb=2,
                                  prev=tuple(prev_mem) if last else ())
        prev_mem.append((mk4[0], mv4[0]))
        xp, st, new_p = _trunk_layer(xp, None, l, None, mk.reshape(1, B, MEM_LEN, MEM_W),
                                     mv.reshape(1, B, MEM_LEN, MEM_W), wl, prev=tuple(prev_p) if last else ())
        prev_p.append(new_p)
        states_p.append(st)
        xs, st, (ka, va, lat, kr) = _trunk_layer(xs, caches, l, state_gla[l], mem_k_cache, mem_v_cache, wl)
        states_s.append(st)
        kas.append(ka.reshape(Bs, Ts, SB_HEADS, SB_HD))
        vas.append(va.reshape(Bs, Ts, SB_HEADS, SB_HD))
        lats.append(lat.reshape(Bs, Ts, MLA_KV_LORA))
        krs.append(kr.reshape(Bs, Ts, MLA_ROPE))
    ka, va, lat, kr = prev_p[-1]
    if depth == 1:
        ka, va, lat, kr = ka[None], va[None], lat[None], kr[None]
    stack = jnp.stack
    return (xp, xs, jnp.transpose(ka, (0, 1, 4, 2, 3)), jnp.transpose(va, (0, 1, 4, 2, 3)), stack(states_p),
            lat.reshape(depth, B, T, MLA_KV_LORA), jnp.transpose(kr, (0, 1, 3, 2)), mk4, mv4,
            stack(kas), stack(vas), stack(states_s), stack(lats), stack(krs))
```
